```python
import math
import jax, jax.numpy as jnp
from jax import lax
import numpy as np

D_MODEL = 2048
BATCH = 16
SEQ = 2048
DEPTH = 4

MEM_LEN = 256
XA_HEADS = 4
XA_HEAD_DIM = D_MODEL // XA_HEADS
MIX_WIDTH = D_MODEL
POOL_WIDTH = MIX_WIDTH // 2
POOL_WINDOWS = (2, 4, 8, 16)
POOL_GROUP = POOL_WIDTH // len(POOL_WINDOWS)
DN_HEAD_DIM = 128
DN_WIDTH = MIX_WIDTH - POOL_WIDTH
DN_HEADS = DN_WIDTH // DN_HEAD_DIM
DN_CONV = 4
DN_CHUNK = 64
IN_WIDTH = POOL_WIDTH + 4 * DN_WIDTH + 2 * DN_HEADS
D_FF = 256 * ((8 * D_MODEL // 3 + 255) // 256)
FFN_CONV = 3
EPS = 1e-6

kernel_name = "hybrid_pool_deltanet_memxattn_convglu"


def rms_norm(x, g):
    xf = x.astype(jnp.float32)
    y = xf * lax.rsqrt(jnp.mean(xf * xf, axis=-1, keepdims=True) + EPS)
    return (y * g.astype(jnp.float32)).astype(x.dtype)


def l2_normalize(t):
    return t * lax.rsqrt(jnp.sum(t * t, axis=-1, keepdims=True) + EPS)


def causal_dwconv(x, w, b=None):
    K = w.shape[0]
    S = x.shape[1]
    xp = jnp.pad(x, ((0, 0), (K - 1, 0), (0, 0)))
    y = sum(xp[:, k:k + S] * w[k] for k in range(K))
    if b is not None:
        y = y + b
    return y


def pool_mixer(u, w_pool, pool_scale):
    B, S, _ = u.shape
    uf = u.astype(jnp.float32)
    cs = jnp.pad(jnp.cumsum(uf, axis=1), ((0, 0), (1, 0), (0, 0)))
    pos = jnp.arange(1, S + 1, dtype=jnp.float32)
    groups = []
    for i, w in enumerate(POOL_WINDOWS):
        sl = slice(i * POOL_GROUP, (i + 1) * POOL_GROUP)
        c = cs[:, :, sl]
        lagged = jnp.pad(c, ((0, 0), (w - 1, 0), (0, 0)))[:, :S]
        count = jnp.minimum(pos, float(w))
        mean = (c[:, 1:] - lagged) / count[None, :, None]
        groups.append(mean - uf[:, :, sl])
    mixed = jnp.stack(groups, axis=2).astype(u.dtype)
    y = jnp.einsum('bsng,ngh->bsnh', mixed, w_pool).reshape(B, S, POOL_WIDTH)
    return y * pool_scale


def gated_delta_net(q, k, v, z, b_logit, a_logit, conv_w, a_log, dt_bias, norm_g):
    B, S, _ = q.shape
    H, Dh, C = DN_HEADS, DN_HEAD_DIM, DN_CHUNK
    N = S // C
    f32 = jnp.float32
    qkv = jax.nn.silu(causal_dwconv(jnp.concatenate([q, k, v], axis=-1), conv_w)).astype(f32)
    q, k, v = [t.reshape(B, S, H, Dh) for t in jnp.split(qkv, 3, axis=-1)]
    q = l2_normalize(q) * (Dh ** -0.5)
    k = l2_normalize(k)
    beta = jax.nn.sigmoid(b_logit.astype(f32))
    g = -jnp.exp(a_log.astype(f32)) * jax.nn.softplus(a_logit.astype(f32) + dt_bias.astype(f32))

    def to_chunks(t):
        return t.reshape(B, N, C, H, -1).transpose(0, 3, 1, 2, 4)

    q, k, v = to_chunks(q), to_chunks(k), to_chunks(v)
    beta = to_chunks(beta[..., None])[..., 0]
    gc = jnp.cumsum(to_chunks(g[..., None])[..., 0], axis=-1)
    idx = jnp.arange(C)
    causal = idx[:, None] >= idx[None, :]
    strict = idx[:, None] > idx[None, :]
    decay = jnp.exp(jnp.where(causal, gc[..., :, None] - gc[..., None, :], -jnp.inf))
    kk = jnp.einsum('bhncd,bhnjd->bhncj', k, k)
    l_mat = jnp.where(strict, beta[..., :, None] * kk * decay, 0.0)
    rhs = jnp.concatenate([v * beta[..., None], k * (beta * jnp.exp(gc))[..., None]], axis=-1)
    sol = lax.linalg.triangular_solve(l_mat, rhs, left_side=True, lower=True, unit_diagonal=True)
    u_c, w_c = sol[..., :Dh], sol[..., Dh:]
    attn = jnp.einsum('bhncd,bhnjd->bhncj', q, k) * decay
    q_dec = q * jnp.exp(gc)[..., None]
    k_dec = k * jnp.exp(gc[..., -1:] - gc)[..., None]
    chunk_decay = jnp.exp(gc[..., -1])
    xs = tuple(jnp.moveaxis(t, 2, 0) for t in (u_c, w_c, q_dec, k_dec, attn, chunk_decay))

    def step(state, inp):
        u_i, w_i, qd_i, kd_i, a_i, dec_i = inp
        v_new = u_i - jnp.einsum('bhcd,bhde->bhce', w_i, state)
        o_i = jnp.einsum('bhcd,bhde->bhce', qd_i, state) + jnp.einsum('bhcj,bhje->bhce', a_i, v_new)
        state = state * dec_i[..., None, None] + jnp.einsum('bhcd,bhce->bhde', kd_i, v_new)
        return state, o_i

    _, o = lax.scan(step, jnp.zeros((B, H, Dh, Dh), f32), xs)
    o = o.transpose(1, 0, 3, 2, 4).reshape(B, S, H, Dh)
    o = rms_norm(o, norm_g) * jax.nn.silu(z.astype(f32).reshape(B, S, H, Dh))
    return o.reshape(B, S, DN_WIDTH).astype(z.dtype)


def hybrid_mixer(h, w_in, w_pool, pool_scale, dn_conv_w, dn_a_log, dn_dt_bias, dn_norm_g, w_mix_out):
    p = h @ w_in
    o0 = POOL_WIDTH
    u = p[..., :o0]
    q = p[..., o0:o0 + DN_WIDTH]
    k = p[..., o0 + DN_WIDTH:o0 + 2 * DN_WIDTH]
    v = p[..., o0 + 2 * DN_WIDTH:o0 + 3 * DN_WIDTH]
    z = p[..., o0 + 3 * DN_WIDTH:o0 + 4 * DN_WIDTH]
    o1 = o0 + 4 * DN_WIDTH
    b_logit = p[..., o1:o1 + DN_HEADS]
    a_logit = p[..., o1 + DN_HEADS:o1 + 2 * DN_HEADS]
    y_pool = pool_mixer(u, w_pool, pool_scale)
    y_dn = gated_delta_net(q, k, v, z, b_logit, a_logit, dn_conv_w, dn_a_log, dn_dt_bias, dn_norm_g)
    return jnp.concatenate([y_pool, y_dn], axis=-1) @ w_mix_out


def memory_cross_attention(h, mem_h, w_xq, w_xkv, w_xo):
    B, S, _ = h.shape
    M = mem_h.shape[1]
    q = (h @ w_xq).reshape(B, S, XA_HEADS, XA_HEAD_DIM)
    kv = mem_h @ w_xkv
    k = kv[..., :D_MODEL].reshape(B, M, XA_HEADS, XA_HEAD_DIM)
    v = kv[..., D_MODEL:].reshape(B, M, XA_HEADS, XA_HEAD_DIM)
    s = jnp.einsum('bshd,bmhd->bhsm', q, k).astype(jnp.float32) * (XA_HEAD_DIM ** -0.5)
    pr = jax.nn.softmax(s, axis=-1).astype(v.dtype)
    o = jnp.einsum('bhsm,bmhd->bshd', pr, v).reshape(B, S, D_MODEL)
    return o @ w_xo


def conv_glu_ffn(h, w_gate, w_up, conv_w, conv_b, w_down):
    gate = causal_dwconv(h @ w_gate, conv_w, conv_b)
    return (jax.nn.silu(gate) * (h @ w_up)) @ w_down


def setup_inputs(seed: int = 0) -> dict:
    key = jax.random.key(seed)
    ks = jax.random.split(key, 32)
    f32 = jnp.float32

    def nrm(i, shape, scale):
        return jax.random.normal(ks[i], shape, f32) * scale

    def gain(i, shape):
        return 1.0 + 0.02 * jax.random.normal(ks[i], shape, f32)

    dt = jnp.exp(jax.random.uniform(ks[7], (DEPTH, DN_HEADS), f32) * (math.log(0.1) - math.log(0.001)) + math.log(0.001))
    return {
        "x": nrm(0, (BATCH, SEQ, D_MODEL), 1.0),
        "mem": nrm(1, (BATCH, MEM_LEN, D_MODEL), 1.0),
        "mix_norm_g": gain(2, (DEPTH, D_MODEL)),
        "w_in": nrm(3, (DEPTH, D_MODEL, IN_WIDTH), D_MODEL ** -0.5),
        "w_pool": nrm(4, (DEPTH, len(POOL_WINDOWS), POOL_GROUP, POOL_GROUP), POOL_GROUP ** -0.5),
        "pool_scale": gain(5, (DEPTH, POOL_WIDTH)),
        "dn_conv_w": nrm(6, (DEPTH, DN_CONV, 3 * DN_WIDTH), DN_CONV ** -0.5),
        "dn_a_log": jnp.log(jax.random.uniform(ks[8], (DEPTH, DN_HEADS), f32, 1.0, 16.0)),
        "dn_dt_bias": dt + jnp.log(-jnp.expm1(-dt)),
        "dn_norm_g": gain(9, (DEPTH, DN_HEAD_DIM)),
        "w_mix_out": nrm(10, (DEPTH, MIX_WIDTH, D_MODEL), MIX_WIDTH ** -0.5),
        "xa_norm_g": gain(11, (DEPTH, D_MODEL)),
        "mem_norm_g": gain(12, (D_MODEL,)),
        "w_xq": nrm(13, (DEPTH, D_MODEL, D_MODEL), D_MODEL ** -0.5),
        "w_xkv": nrm(14, (DEPTH, D_MODEL, 2 * D_MODEL), D_MODEL ** -0.5),
        "w_xo": nrm(15, (DEPTH, D_MODEL, D_MODEL), D_MODEL ** -0.5),
        "ffn_norm_g": gain(16, (DEPTH, D_MODEL)),
        "w_gate": nrm(17, (DEPTH, D_MODEL, D_FF), D_MODEL ** -0.5),
        "w_up": nrm(18, (DEPTH, D_MODEL, D_FF), D_MODEL ** -0.5),
        "ffn_conv_w": nrm(19, (DEPTH, FFN_CONV, D_FF), FFN_CONV ** -0.5),
        "ffn_conv_b": nrm(20, (DEPTH, D_FF), 0.01),
        "w_down": nrm(21, (DEPTH, D_FF, D_MODEL), D_FF ** -0.5),
        "final_norm_g": gain(22, (D_MODEL,)),
    }


def reference(x, mem, mix_norm_g, w_in, w_pool, pool_scale, dn_conv_w, dn_a_log, dn_dt_bias, dn_norm_g,
              w_mix_out, xa_norm_g, mem_norm_g, w_xq, w_xkv, w_xo, ffn_norm_g, w_gate, w_up,
              ffn_conv_w, ffn_conv_b, w_down, final_norm_g):
    mem_h = rms_norm(mem, mem_norm_g)
    for l in range(DEPTH):
        x = x + hybrid_mixer(rms_norm(x, mix_norm_g[l]), w_in[l], w_pool[l], pool_scale[l], dn_conv_w[l],
                             dn_a_log[l], dn_dt_bias[l], dn_norm_g[l], w_mix_out[l])
        x = x + memory_cross_attention(rms_norm(x, xa_norm_g[l]), mem_h, w_xq[l], w_xkv[l], w_xo[l])
        x = x + conv_glu_ffn(rms_norm(x, ffn_norm_g[l]), w_gate[l], w_up[l], ffn_conv_w[l], ffn_conv_b[l], w_down[l])
    return rms_norm(x, final_norm_g)
```

```python
import functools
import math

import jax
import jax.numpy as jnp
from jax import lax
from jax.experimental import pallas as pl
from jax.experimental.pallas import tpu as pltpu

F32 = jnp.float32
BF16 = jnp.bfloat16

EPS = 1e-6
POOL_WINDOWS = (2, 4, 8, 16)
POOL_HALO = 16
DN_HEAD_DIM = 128
DN_CHUNK = 128
XA_HEADS = 4

V7X_VMEM_BYTES = 64 * 1024 * 1024
V7X_LANES = 128
V7X_SUBLANES = 8
NORM_ROWS = 64


def _compiler_params(n_axes, vmem_bytes):
    limit = min(int(vmem_bytes * 1.25) + (4 << 20), V7X_VMEM_BYTES * 7 // 8)
    return pltpu.CompilerParams(dimension_semantics=("arbitrary",) * n_axes, vmem_limit_bytes=limit)


def _dot(a, b):
    return jnp.dot(a, b, preferred_element_type=F32)


def _dot_nt(a, b):
    return lax.dot_general(a, b, (((1,), (1,)), ((), ())), preferred_element_type=F32)


def _split_bf16(x):
    hi = x.astype(BF16)
    lo = (x - hi.astype(F32)).astype(BF16)
    return hi, lo


def _dot3(a, b):
    ah, al = _split_bf16(a)
    bh, bl = _split_bf16(b)
    return _dot(ah, bh) + _dot(ah, bl) + _dot(al, bh)


def _silu(x):
    return x * (1.0 / (1.0 + jnp.exp(-x)))


def _rms_rows(x, g):
    ms = jnp.mean(x * x, axis=-1, keepdims=True)
    return x * lax.rsqrt(ms + EPS) * g


def _norm_rows_into(h_ref, x_ref, g_ref):
    g = g_ref[...]

    def body(r, c):
        rows = pl.ds(pl.multiple_of(r * NORM_ROWS, NORM_ROWS), NORM_ROWS)
        h_ref[rows, :] = _rms_rows(x_ref[rows, :], g).astype(h_ref.dtype)
        return c

    lax.fori_loop(0, x_ref.shape[0] // NORM_ROWS, body, 0)


def _norm_matmul_body(x_ref, g_ref, w_ref, o_ref, h_ref):
    @pl.when((pl.program_id(1) == 0) & (pl.program_id(2) == 0))
    def _():
        _norm_rows_into(h_ref, x_ref, g_ref)

    o_ref[...] = _dot(h_ref[...], w_ref[...]).astype(o_ref.dtype)


def norm_matmul(x, g, w, out_dtype, tm, tn):
    R, K = x.shape
    L, _, N = w.shape
    osz = jnp.dtype(out_dtype).itemsize
    vmem = 2 * tm * K * 4 + tm * K * 2 + 2 * K * tn * 2 + 2 * tm * tn * osz + tm * tn * 4
    return pl.pallas_call(
        _norm_matmul_body,
        grid=(R // tm, L, N // tn),
        in_specs=[
            pl.BlockSpec((tm, K), lambda i, l, j: (i, 0)),
            pl.BlockSpec((1, K), lambda i, l, j: (0, 0)),
            pl.BlockSpec((None, K, tn), lambda i, l, j: (l, 0, j)),
        ],
        out_specs=pl.BlockSpec((None, tm, tn), lambda i, l, j: (l, i, j)),
        out_shape=jax.ShapeDtypeStruct((L, R, N), out_dtype),
        scratch_shapes=[pltpu.VMEM((tm, K), BF16)],
        compiler_params=_compiler_params(3, vmem),
        name="norm_matmul",
    )(x, g, w)


def _in_proj_body(x_ref, g_ref, w_ref, wba_ref, p_ref, ba_ref, h_ref):
    @pl.when(pl.program_id(1) == 0)
    def _():
        _norm_rows_into(h_ref, x_ref, g_ref)
        ba_ref[...] = _dot(h_ref[...], wba_ref[...])

    p_ref[...] = _dot(h_ref[...], w_ref[...])


def in_proj(x, g, w_main, w_ba, l, tm, tn):
    R, K = x.shape
    N = w_main.shape[2]
    NB = w_ba.shape[2]
    vmem = 2 * tm * K * 4 + tm * K * 2 + 2 * K * tn * 2 + 2 * K * NB * 2 + 3 * tm * tn * 4 + 3 * tm * NB * 4
    return pl.pallas_call(
        _in_proj_body,
        grid=(R // tm, N // tn),
        in_specs=[
            pl.BlockSpec((tm, K), lambda i, j: (i, 0)),
            pl.BlockSpec((1, K), lambda i, j: (0, 0)),
            pl.BlockSpec((None, K, tn), lambda i, j: (l, 0, j)),
            pl.BlockSpec((None, K, NB), lambda i, j: (l, 0, 0)),
        ],
        out_specs=[
            pl.BlockSpec((tm, tn), lambda i, j: (i, j)),
            pl.BlockSpec((tm, NB), lambda i, j: (i, 0)),
        ],
        out_shape=[jax.ShapeDtypeStruct((R, N), F32), jax.ShapeDtypeStruct((R, NB), F32)],
        scratch_shapes=[pltpu.VMEM((tm, K), BF16)],
        compiler_params=_compiler_params(2, vmem),
        name="in_proj",
    )(x, g, w_main, w_ba)


def _pool_body(u_ref, wp_ref, sc_ref, o_ref, buf):
    s = pl.program_id(1)
    ts = u_ref.shape[0]
    G = wp_ref.shape[1]

    @pl.when(s == 0)
    def _():
        buf[0:POOL_HALO, :] = jnp.zeros((POOL_HALO, buf.shape[1]), F32)

    buf[POOL_HALO:POOL_HALO + ts, :] = u_ref[...]
    pos = (s * ts + 1 + lax.broadcasted_iota(jnp.int32, (ts, 1), 0)).astype(F32)
    for i, w in enumerate(POOL_WINDOWS):
        cols = slice(i * G, (i + 1) * G)
        u = buf[POOL_HALO:POOL_HALO + ts, cols]
        acc = u
        for k in range(1, w):
            acc = acc + buf[POOL_HALO - k:POOL_HALO - k + ts, cols]
        mixed = acc / jnp.minimum(pos, float(w)) - u
        y = _dot(mixed.astype(BF16), wp_ref[i]) * sc_ref[:, cols]
        o_ref[:, cols] = y.astype(o_ref.dtype)
    buf[0:POOL_HALO, :] = buf[ts:ts + POOL_HALO, :]


def pool_mixer(p, w_pool, pool_scale, l, B, S, ts):
    n_win, G, _ = w_pool.shape[1:]
    W = n_win * G
    nt = S // ts
    vmem = 2 * ts * W * 4 + (ts + POOL_HALO) * W * 4 + 2 * n_win * G * G * 2 + 2 * ts * W * 2 + 4 * ts * G * 4
    return pl.pallas_call(
        _pool_body,
        grid=(B, nt),
        in_specs=[
            pl.BlockSpec((ts, W), lambda b, s: (b * nt + s, 0)),
            pl.BlockSpec((None, n_win, G, G), lambda b, s: (l, 0, 0, 0)),
            pl.BlockSpec((None, 1, W), lambda b, s: (l, 0, 0)),
        ],
        out_specs=pl.BlockSpec((ts, W), lambda b, s: (b * nt + s, 0)),
        out_shape=jax.ShapeDtypeStruct((B * S, W), BF16),
        scratch_shapes=[pltpu.VMEM((ts + POOL_HALO, W), F32)],
        compiler_params=_compiler_params(2, vmem),
        name="pool_mixer",
    )(p, w_pool, pool_scale)


def _unit_lower_inverse(L, eye):
    C = L.shape[0]
    n_iter = int(math.log2(C)) - 1
    x = eye - L
    lb = L.astype(BF16)
    for it in range(n_iter):
        if it < n_iter - 1:
            lx = _dot(lb, x.astype(BF16))
        else:
            lx = _dot3(L, x)
        e = eye - x - lx
        x = x + _dot(x.astype(BF16), e.astype(BF16))
    return x


def _dn_body(q_ref, k_ref, v_ref, z_ref, ba_ref, cw_ref, alog_ref, dtb_ref, ng_ref, o_ref,
             buf, state, beta_rep, gc_rep, grow_s, cdec_s, u_s, w_s, qd_s, at_s, kdt_s):
    s = pl.program_id(1)
    ts, W = q_ref.shape
    H = W // DN_HEAD_DIM
    C = DN_CHUNK
    n_chunk = ts // C
    K = cw_ref.shape[0]
    halo = V7X_SUBLANES

    @pl.when(s == 0)
    def _():
        buf[0:halo, :] = jnp.zeros((halo, 3 * W), F32)
        state[...] = jnp.zeros(state.shape, F32)

    buf[halo:halo + ts, 0:W] = q_ref[...]
    buf[halo:halo + ts, W:2 * W] = k_ref[...]
    buf[halo:halo + ts, 2 * W:3 * W] = v_ref[...]

    ba = ba_ref[...]
    beta = 1.0 / (1.0 + jnp.exp(-ba))
    xs = ba + dtb_ref[...]
    softplus = jnp.maximum(xs, 0.0) + jnp.log1p(jnp.exp(-jnp.abs(xs)))
    g = -jnp.exp(alog_ref[...]) * softplus
    row_in_chunk = lax.broadcasted_iota(jnp.int32, (ts, V7X_LANES), 0) % C
    gc = g
    sh = 1
    while sh < C:
        gc = gc + jnp.where(row_in_chunk >= sh, pltpu.roll(gc, sh, axis=0), 0.0)
        sh *= 2
    gct = gc.T
    lane = lax.broadcasted_iota(jnp.int32, (ts, V7X_LANES), 1)
    for h in range(H):
        beta_rep[h] = jnp.broadcast_to(
            jnp.sum(jnp.where(lane == h, beta, 0.0), axis=-1, keepdims=True), (ts, V7X_LANES))
        gc_rep[h] = jnp.broadcast_to(
            jnp.sum(jnp.where(lane == H + h, gc, 0.0), axis=-1, keepdims=True), (ts, V7X_LANES))
        grow_s[h] = gct[H + h:H + h + 1, :]

    ii = lax.broadcasted_iota(jnp.int32, (C, C), 0)
    jj = lax.broadcasted_iota(jnp.int32, (C, C), 1)
    eye = (ii == jj).astype(F32)

    def conv_silu(col):
        cols = pl.ds(col, DN_HEAD_DIM)
        acc = buf[halo - K + 1:halo - K + 1 + ts, cols] * cw_ref[0:1, cols]
        for t in range(1, K):
            acc = acc + buf[halo - K + 1 + t:halo - K + 1 + t + ts, cols] * cw_ref[t:t + 1, cols]
        return _silu(acc)

    def head_body(h, carry):
        c0 = pl.multiple_of(h * DN_HEAD_DIM, DN_HEAD_DIM)
        hcols = pl.ds(c0, DN_HEAD_DIM)
        qc = conv_silu(c0)
        kc = conv_silu(pl.multiple_of(W + c0, DN_HEAD_DIM))
        vc = conv_silu(pl.multiple_of(2 * W + c0, DN_HEAD_DIM))
        qn = qc * lax.rsqrt(jnp.sum(qc * qc, axis=-1, keepdims=True) + EPS) * (DN_HEAD_DIM ** -0.5)
        kn = kc * lax.rsqrt(jnp.sum(kc * kc, axis=-1, keepdims=True) + EPS)
        bcol = beta_rep[h]
        gcol = gc_rep[h]
        egc = jnp.exp(gcol)
        grow_all = grow_s[h]
        for c in range(n_chunk):
            r = slice(c * C, (c + 1) * C)
            k_c, q_c, v_c, b_c, g_c, e_c = kn[r], qn[r], vc[r], bcol[r], gcol[r], egc[r]
            diff = g_c - grow_all[:, r]
            dec = jnp.exp(jnp.minimum(diff, 0.0))
            kb = k_c.astype(BF16)
            kk = _dot_nt(kb, kb)
            l_mat = jnp.where(ii > jj, kk * dec, 0.0) * b_c
            t_inv = _unit_lower_inverse(l_mat, eye)
            u_c = _dot3(t_inv, v_c * b_c)
            w_c = _dot3(t_inv, k_c * (b_c * e_c))
            attn = jnp.where(ii >= jj, _dot_nt(q_c.astype(BF16), kb) * dec, 0.0)
            kd = k_c * jnp.exp(g_c[C - 1:C, :] - g_c)
            u_s[r, hcols] = u_c
            w_s[r, hcols] = w_c.astype(BF16)
            qd_s[r, hcols] = (q_c * e_c).astype(BF16)
            at_s[r, hcols] = attn.astype(BF16)
            kdt_s[r, hcols] = kd.T.astype(BF16)
            cdec_s[c, h] = jnp.exp(g_c[C - 1:C, :])
        return carry

    lax.fori_loop(0, H, head_body, 0)

    def chunk_body(c, carry):
        r0 = pl.multiple_of(c * C, C)
        rows = pl.ds(r0, C)
        for h in range(H):
            cols = slice(h * DN_HEAD_DIM, (h + 1) * DN_HEAD_DIM)
            st = state[h]
            sb = st.astype(BF16)
            v_new = u_s[rows, cols] - _dot(w_s[rows, cols], sb)
            vb = v_new.astype(BF16)
            o = _dot(qd_s[rows, cols], sb) + _dot(at_s[rows, cols], vb)
            state[h] = st * cdec_s[c, h] + _dot(kdt_s[rows, cols], vb)
            gated = _rms_rows(o, ng_ref[...]) * _silu(z_ref[rows, cols])
            o_ref[rows, cols] = gated.astype(o_ref.dtype)
        return carry

    lax.fori_loop(0, n_chunk, chunk_body, 0)
    buf[0:halo, :] = buf[ts:ts + halo, :]


def gated_delta_net(p, ba, conv_w, a_log, dt_bias, norm_g, l, B, S, ts, col_block0):
    H = conv_w.shape[2] // (3 * DN_HEAD_DIM)
    W = H * DN_HEAD_DIM
    nt = S // ts
    NB = ba.shape[1]
    C = DN_CHUNK
    vmem = (2 * 4 * ts * W * 4 + 2 * ts * NB * 4 + 2 * ts * W * 2 + (ts + V7X_SUBLANES) * 3 * W * 4
            + H * DN_HEAD_DIM * DN_HEAD_DIM * 4 + 2 * H * ts * V7X_LANES * 4 + H * V7X_SUBLANES * ts * 4
            + ts * W * 4 + 4 * ts * W * 2 + 16 * ts * DN_HEAD_DIM * 4 + 32 * C * C * 4)

    def pblock(k):
        return pl.BlockSpec((ts, W), lambda b, s: (b * nt + s, col_block0 + k))

    return pl.pallas_call(
        _dn_body,
        grid=(B, nt),
        in_specs=[
            pblock(0), pblock(1), pblock(2), pblock(3),
            pl.BlockSpec((ts, NB), lambda b, s: (b * nt + s, 0)),
            pl.BlockSpec((None,) + conv_w.shape[1:], lambda b, s: (l, 0, 0)),
            pl.BlockSpec((None, 1, NB), lambda b, s: (l, 0, 0)),
            pl.BlockSpec((None, 1, NB), lambda b, s: (l, 0, 0)),
            pl.BlockSpec((None, 1, DN_HEAD_DIM), lambda b, s: (l, 0, 0)),
        ],
        out_specs=pl.BlockSpec((ts, W), lambda b, s: (b * nt + s, 0)),
        out_shape=jax.ShapeDtypeStruct((B * S, W), BF16),
        scratch_shapes=[
            pltpu.VMEM((ts + V7X_SUBLANES, 3 * W), F32),
            pltpu.VMEM((H, DN_HEAD_DIM, DN_HEAD_DIM), F32),
            pltpu.VMEM((H, ts, V7X_LANES), F32),
            pltpu.VMEM((H, ts, V7X_LANES), F32),
            pltpu.VMEM((H, 1, ts), F32),
            pltpu.VMEM((ts // C, H, 1, V7X_LANES), F32),
            pltpu.VMEM((ts, W), F32),
            pltpu.VMEM((ts, W), BF16),
            pltpu.VMEM((ts, W), BF16),
            pltpu.VMEM((ts, W), BF16),
            pltpu.VMEM((ts, W), BF16),
        ],
        compiler_params=_compiler_params(2, vmem),
        name="gated_delta_net",
    )(p, p, p, p, ba, conv_w, a_log, dt_bias, norm_g)


def _mix_out_body(x_ref, yp_ref, yd_ref, w1_ref, w2_ref, o_ref):
    o_ref[...] = x_ref[...] + _dot(yp_ref[...], w1_ref[...]) + _dot(yd_ref[...], w2_ref[...])


def mix_out(x, y_pool, y_dn, w, l, tm, tn):
    R, D = x.shape
    K1, K2 = y_pool.shape[1], y_dn.shape[1]
    assert K1 == K2
    vmem = 2 * (2 * tm * tn * 4 + tm * (K1 + K2) * 2 + (K1 + K2) * tn * 2) + 2 * tm * tn * 4
    return pl.pallas_call(
        _mix_out_body,
        grid=(R // tm, D // tn),
        in_specs=[
            pl.BlockSpec((tm, tn), lambda i, j: (i, j)),
            pl.BlockSpec((tm, K1), lambda i, j: (i, 0)),
            pl.BlockSpec((tm, K2), lambda i, j: (i, 0)),
            pl.BlockSpec((None, K1, tn), lambda i, j: (l, 0, j)),
            pl.BlockSpec((None, K2, tn), lambda i, j: (l, 1, j)),
        ],
        out_specs=pl.BlockSpec((tm, tn), lambda i, j: (i, j)),
        out_shape=jax.ShapeDtypeStruct((R, D), F32),
        compiler_params=_compiler_params(2, vmem),
        name="mix_out",
    )(x, y_pool, y_dn, w, w)


def _xattn_body(x_ref, g_ref, wq_ref, k_ref, v_ref, wo_ref, o_ref, h_ref):
    @pl.when(pl.program_id(1) == 0)
    def _():
        _norm_rows_into(h_ref, x_ref, g_ref)
        o_ref[...] = x_ref[...]

    dh = wq_ref.shape[1]
    q = _dot(h_ref[...], wq_ref[...])
    sc = _dot_nt(q.astype(BF16), k_ref[...]) * (dh ** -0.5)
    e = jnp.exp(sc - jnp.max(sc, axis=-1, keepdims=True))
    pr = e / jnp.sum(e, axis=-1, keepdims=True)
    o = _dot(pr.astype(BF16), v_ref[...])
    o_ref[...] += _dot(o.astype(BF16), wo_ref[...])


def cross_attention(x, g, w_xq, kv, w_xo, l, B, S, M, tm):
    R, D = x.shape
    dh = D // XA_HEADS
    tiles_per_batch = S // tm
    vmem = 4 * tm * D * 4 + tm * D * 2 + 4 * D * dh * 2 + 4 * M * dh * 2 + 3 * tm * dh * 4 + 3 * tm * M * 4 + tm * D * 4
    return pl.pallas_call(
        _xattn_body,
        grid=(R // tm, XA_HEADS),
        in_specs=[
            pl.BlockSpec((tm, D), lambda i, j: (i, 0)),
            pl.BlockSpec((None, 1, D), lambda i, j: (l, 0, 0)),
            pl.BlockSpec((None, D, dh), lambda i, j: (l, 0, j)),
            pl.BlockSpec((None, M, dh), lambda i, j: (l, i // tiles_per_batch, j)),
            pl.BlockSpec((None, M, dh), lambda i, j: (l, i // tiles_per_batch, XA_HEADS + j)),
            pl.BlockSpec((None, dh, D), lambda i, j: (l, j, 0)),
        ],
        out_specs=pl.BlockSpec((tm, D), lambda i, j: (i, 0)),
        out_shape=jax.ShapeDtypeStruct((R, D), F32),
        scratch_shapes=[pltpu.VMEM((tm, D), BF16)],
        compiler_params=_compiler_params(2, vmem),
        name="cross_attention",
    )(x, g, w_xq, kv, kv, w_xo)


def _ffn_body(x_ref, g_ref, wg_ref, wu_ref, cw_ref, cb_ref, wd_ref, o_ref, h_ref, gbuf, carry, *, tiles_per_seq):
    i = pl.program_id(0)
    j = pl.program_id(1)
    tm = x_ref.shape[0]
    K = cw_ref.shape[0]
    halo = V7X_SUBLANES

    @pl.when(j == 0)
    def _():
        _norm_rows_into(h_ref, x_ref, g_ref)
        o_ref[...] = x_ref[...]

    h = h_ref[...]
    gate = _dot(h, wg_ref[...])
    up = _dot(h, wu_ref[...])
    first = (i % tiles_per_seq) == 0
    gbuf[0:halo, :] = jnp.where(first, 0.0, carry[j])
    gbuf[halo:halo + tm, :] = gate
    carry[j] = gate[tm - halo:tm, :]
    conv = cb_ref[...] + gbuf[halo - K + 1:halo - K + 1 + tm, :] * cw_ref[0:1, :]
    for t in range(1, K):
        conv = conv + gbuf[halo - K + 1 + t:halo - K + 1 + t + tm, :] * cw_ref[t:t + 1, :]
    act = _silu(conv) * up
    o_ref[...] += _dot(act.astype(BF16), wd_ref[...])


def conv_glu_ffn(x, g, w_gate, w_up, conv_w, conv_b, w_down, l, S, tm, tf):
    R, D = x.shape
    F = w_gate.shape[2]
    K = conv_w.shape[1]
    nf = F // tf
    vmem = (4 * tm * D * 4 + tm * D * 2 + 2 * 3 * D * tf * 2 + (tm + V7X_SUBLANES) * tf * 4
            + nf * V7X_SUBLANES * tf * 4 + 4 * tm * tf * 4 + tm * D * 4)
    return pl.pallas_call(
        functools.partial(_ffn_body, tiles_per_seq=S // tm),
        grid=(R // tm, nf),
        in_specs=[
            pl.BlockSpec((tm, D), lambda i, j: (i, 0)),
            pl.BlockSpec((None, 1, D), lambda i, j: (l, 0, 0)),
            pl.BlockSpec((None, D, tf), lambda i, j: (l, 0, j)),
            pl.BlockSpec((None, D, tf), lambda i, j: (l, 0, j)),
            pl.BlockSpec((None, K, tf), lambda i, j: (l, 0, j)),
            pl.BlockSpec((None, 1, tf), lambda i, j: (l, 0, j)),
            pl.BlockSpec((None, tf, D), lambda i, j: (l, j, 0)),
        ],
        out_specs=pl.BlockSpec((tm, D), lambda i, j: (i, 0)),
        out_shape=jax.ShapeDtypeStruct((R, D), F32),
        scratch_shapes=[
            pltpu.VMEM((tm, D), BF16),
            pltpu.VMEM((tm + V7X_SUBLANES, tf), F32),
            pltpu.VMEM((nf, V7X_SUBLANES, tf), F32),
        ],
        compiler_params=_compiler_params(2, vmem),
        name="conv_glu_ffn",
    )(x, g, w_gate, w_up, conv_w, conv_b, w_down)


def _final_norm_body(x_ref, g_ref, o_ref):
    _norm_rows_into(o_ref, x_ref, g_ref)


def final_norm(x, g, tm):
    R, D = x.shape
    return pl.pallas_call(
        _final_norm_body,
        grid=(R // tm,),
        in_specs=[pl.BlockSpec((tm, D), lambda i: (i, 0)), pl.BlockSpec((1, D), lambda i: (0, 0))],
        out_specs=pl.BlockSpec((tm, D), lambda i: (i, 0)),
        out_shape=jax.ShapeDtypeStruct((R, D), F32),
        compiler_params=_compiler_params(1, 4 * tm * D * 4),
        name="final_norm",
    )(x, g)


def _tiles(S):
    return dict(tm=min(512, S), ts_pool=min(512, S), ts_dn=min(256, S))


def kernel(x, mem, mix_norm_g, w_in, w_pool, pool_scale, dn_conv_w, dn_a_log, dn_dt_bias, dn_norm_g, w_mix_out,
           xa_norm_g, mem_norm_g, w_xq, w_xkv, w_xo, ffn_norm_g, w_gate, w_up, ffn_conv_w, ffn_conv_b, w_down,
           final_norm_g):
    B, S, D = x.shape
    M = mem.shape[1]
    depth = w_in.shape[0]
    H = dn_a_log.shape[1]
    dn_w = H * DN_HEAD_DIM
    pool_w = w_pool.shape[1] * w_pool.shape[2]
    main = pool_w + 4 * dn_w
    assert pool_w == dn_w and w_in.shape[2] == main + 2 * H and 2 * H <= V7X_LANES
    t = _tiles(S)

    w_in_main = w_in[:, :, :main].astype(BF16)
    w_ba = jnp.pad(w_in[:, :, main:], ((0, 0), (0, 0), (0, V7X_LANES - 2 * H))).astype(BF16)
    pad_gate = lambda a: jnp.pad(a, ((0, 0), (H, V7X_LANES - 2 * H)))[:, None, :]
    a_log_p, dt_bias_p = pad_gate(dn_a_log), pad_gate(dn_dt_bias)
    w_pool_b = w_pool.astype(BF16)
    w_mix_b = w_mix_out.astype(BF16)
    w_xq_b, w_xkv_b, w_xo_b = w_xq.astype(BF16), w_xkv.astype(BF16), w_xo.astype(BF16)
    w_gate_b, w_up_b, w_down_b = w_gate.astype(BF16), w_up.astype(BF16), w_down.astype(BF16)
    row = lambda a: a[:, None, :]

    xr = x.reshape(B * S, D)
    kv = norm_matmul(mem.reshape(B * M, D), mem_norm_g[None, :], w_xkv_b, BF16, tm=min(1024, B * M), tn=512)
    for l in range(depth):
        p, ba = in_proj(xr, mix_norm_g[l][None, :], w_in_main, w_ba, l, tm=t["tm"], tn=512)
        y_pool = pool_mixer(p, w_pool_b, row(pool_scale), l, B, S, t["ts_pool"])
        y_dn = gated_delta_net(p, ba, dn_conv_w, a_log_p, dt_bias_p, row(dn_norm_g), l, B, S, t["ts_dn"],
                               col_block0=pool_w // dn_w)
        xr = mix_out(xr, y_pool, y_dn, w_mix_b, l, tm=t["tm"], tn=512)
        xr = cross_attention(xr, row(xa_norm_g), w_xq_b, kv, w_xo_b, l, B, S, M, tm=t["tm"])
        xr = conv_glu_ffn(xr, row(ffn_norm_g), w_gate_b, w_up_b, ffn_conv_w, row(ffn_conv_b), w_down_b, l, S,
                          tm=t["tm"], tf=512)
    return final_norm(xr, final_norm_g[None, :], tm=t["tm"]).reshape(B, S, D)
```

```python
import functools
import math

import jax
import jax.numpy as jnp
from jax import lax
from jax.experimental import pallas as pl
from jax.experimental.pallas import tpu as pltpu

F32 = jnp.float32
BF16 = jnp.bfloat16

EPS = 1e-6
POOL_WINDOWS = (2, 4, 8, 16)
POOL_HALO = 16
DN_HEAD_DIM = 128
DN_CHUNK = 128
XA_HEADS = 4

V7X_VMEM_BYTES = 64 * 1024 * 1024
V7X_LANES = 128
V7X_SUBLANES = 8
NORM_ROWS = 64


def _compiler_params(n_axes, vmem_bytes):
    limit = min(int(vmem_bytes * 1.25) + (4 << 20), V7X_VMEM_BYTES * 7 // 8)
    return pltpu.CompilerParams(dimension_semantics=("arbitrary",) * n_axes, vmem_limit_bytes=limit)


def _dot(a, b):
    return jnp.dot(a, b, preferred_element_type=F32)


def _dot_nt(a, b):
    return lax.dot_general(a, b, (((1,), (1,)), ((), ())), preferred_element_type=F32)


def _split_bf16(x):
    hi = x.astype(BF16)
    lo = (x - hi.astype(F32)).astype(BF16)
    return hi, lo


def _dot3(a, b):
    ah, al = _split_bf16(a)
    bh, bl = _split_bf16(b)
    return _dot(ah, bh) + _dot(ah, bl) + _dot(al, bh)


def _silu(x):
    return x * (1.0 / (1.0 + jnp.exp(-x)))


def _rms_rows(x, g):
    ms = jnp.mean(x * x, axis=-1, keepdims=True)
    return x * lax.rsqrt(ms + EPS) * g


def _norm_rows_into(h_ref, x_ref, g_ref):
    g = g_ref[...]

    def body(r, c):
        rows = pl.ds(pl.multiple_of(r * NORM_ROWS, NORM_ROWS), NORM_ROWS)
        h_ref[rows, :] = _rms_rows(x_ref[rows, :], g).astype(h_ref.dtype)
        return c

    lax.fori_loop(0, x_ref.shape[0] // NORM_ROWS, body, 0)


def _norm_matmul_body(x_ref, g_ref, w_ref, o_ref, h_ref):
    @pl.when((pl.program_id(1) == 0) & (pl.program_id(2) == 0))
    def _():
        _norm_rows_into(h_ref, x_ref, g_ref)

    o_ref[...] = _dot(h_ref[...], w_ref[...]).astype(o_ref.dtype)


def norm_matmul(x, g, w, out_dtype, tm, tn):
    R, K = x.shape
    L, _, N = w.shape
    osz = jnp.dtype(out_dtype).itemsize
    vmem = 2 * tm * K * 4 + tm * K * 2 + 2 * K * tn * 2 + 2 * tm * tn * osz + tm * tn * 4
    return pl.pallas_call(
        _norm_matmul_body,
        grid=(R // tm, L, N // tn),
        in_specs=[
            pl.BlockSpec((tm, K), lambda i, l, j: (i, 0)),
            pl.BlockSpec((1, K), lambda i, l, j: (0, 0)),
            pl.BlockSpec((None, K, tn), lambda i, l, j: (l, 0, j)),
        ],
        out_specs=pl.BlockSpec((None, tm, tn), lambda i, l, j: (l, i, j)),
        out_shape=jax.ShapeDtypeStruct((L, R, N), out_dtype),
        scratch_shapes=[pltpu.VMEM((tm, K), BF16)],
        compiler_params=_compiler_params(3, vmem),
        name="norm_matmul",
    )(x, g, w)


def _in_proj_body(x_ref, g_ref, w_ref, wba_ref, p_ref, ba_ref, h_ref):
    @pl.when(pl.program_id(1) == 0)
    def _():
        _norm_rows_into(h_ref, x_ref, g_ref)
        ba_ref[...] = _dot(h_ref[...], wba_ref[...])

    p_ref[...] = _dot(h_ref[...], w_ref[...])


def in_proj(x, g, w_main, w_ba, l, tm, tn):
    R, K = x.shape
    N = w_main.shape[2]
    NB = w_ba.shape[2]
    vmem = 2 * tm * K * 4 + tm * K * 2 + 2 * K * tn * 2 + 2 * K * NB * 2 + 3 * tm * tn * 4 + 3 * tm * NB * 4
    return pl.pallas_call(
        _in_proj_body,
        grid=(R // tm, N // tn),
        in_specs=[
            pl.BlockSpec((tm, K), lambda i, j: (i, 0)),
            pl.BlockSpec((1, K), lambda i, j: (0, 0)),
            pl.BlockSpec((None, K, tn), lambda i, j: (l, 0, j)),
            pl.BlockSpec((None, K, NB), lambda i, j: (l, 0, 0)),
        ],
        out_specs=[
            pl.BlockSpec((tm, tn), lambda i, j: (i, j)),
            pl.BlockSpec((tm, NB), lambda i, j: (i, 0)),
        ],
        out_shape=[jax.ShapeDtypeStruct((R, N), F32), jax.ShapeDtypeStruct((R, NB), F32)],
        scratch_shapes=[pltpu.VMEM((tm, K), BF16)],
        compiler_params=_compiler_params(2, vmem),
        name="in_proj",
    )(x, g, w_main, w_ba)


def _pool_body(u_ref, wp_ref, sc_ref, o_ref, buf):
    s = pl.program_id(1)
    ts = u_ref.shape[0]
    G = wp_ref.shape[1]

    @pl.when(s == 0)
    def _():
        buf[0:POOL_HALO, :] = jnp.zeros((POOL_HALO, buf.shape[1]), F32)

    buf[POOL_HALO:POOL_HALO + ts, :] = u_ref[...]
    pos = (s * ts + 1 + lax.broadcasted_iota(jnp.int32, (ts, 1), 0)).astype(F32)
    for i, w in enumerate(POOL_WINDOWS):
        cols = slice(i * G, (i + 1) * G)
        u = buf[POOL_HALO:POOL_HALO + ts, cols]
        acc = u
        for k in range(1, w):
            acc = acc + buf[POOL_HALO - k:POOL_HALO - k + ts, cols]
        mixed = acc / jnp.minimum(pos, float(w)) - u
        y = _dot(mixed.astype(BF16), wp_ref[i]) * sc_ref[:, cols]
        o_ref[:, cols] = y.astype(o_ref.dtype)
    buf[0:POOL_HALO, :] = buf[ts:ts + POOL_HALO, :]


def pool_mixer(p, w_pool, pool_scale, l, B, S, ts):
    n_win, G, _ = w_pool.shape[1:]
    W = n_win * G
    nt = S // ts
    vmem = 2 * ts * W * 4 + (ts + POOL_HALO) * W * 4 + 2 * n_win * G * G * 2 + 2 * ts * W * 2 + 4 * ts * G * 4
    return pl.pallas_call(
        _pool_body,
        grid=(B, nt),
        in_specs=[
            pl.BlockSpec((ts, W), lambda b, s: (b * nt + s, 0)),
            pl.BlockSpec((None, n_win, G, G), lambda b, s: (l, 0, 0, 0)),
            pl.BlockSpec((None, 1, W), lambda b, s: (l, 0, 0)),
        ],
        out_specs=pl.BlockSpec((ts, W), lambda b, s: (b * nt + s, 0)),
        out_shape=jax.ShapeDtypeStruct((B * S, W), BF16),
        scratch_shapes=[pltpu.VMEM((ts + POOL_HALO, W), F32)],
        compiler_params=_compiler_params(2, vmem),
        name="pool_mixer",
    )(p, w_pool, pool_scale)


def _unit_lower_inverses(ls, eye):
    C = ls[0].shape[0]
    n_iter = int(math.log2(C)) - 1
    xs = [eye - l_mat for l_mat in ls]
    lbs = [l_mat.astype(BF16) for l_mat in ls]
    for it in range(n_iter):
        if it < n_iter - 1:
            lxs = [_dot(lb, x.astype(BF16)) for lb, x in zip(lbs, xs)]
        else:
            lxs = [_dot3(l_mat, x) for l_mat, x in zip(ls, xs)]
        es = [eye - x - lx for x, lx in zip(xs, lxs)]
        xs = [x + _dot(x.astype(BF16), e.astype(BF16)) for x, e in zip(xs, es)]
    return xs


def _dn_body(q_ref, k_ref, v_ref, z_ref, ba_ref, cw_ref, alog_ref, dtb_ref, ng_ref, o_ref,
             buf, state, beta_rep, gc_rep, grow_s, cdec_s, u_s, w_s, qd_s, at_s, kdt_s):
    s = pl.program_id(1)
    ts, W = q_ref.shape
    H = W // DN_HEAD_DIM
    C = DN_CHUNK
    n_chunk = ts // C
    K = cw_ref.shape[0]
    halo = V7X_SUBLANES

    @pl.when(s == 0)
    def _():
        buf[0:halo, :] = jnp.zeros((halo, 3 * W), F32)
        state[...] = jnp.zeros(state.shape, F32)

    buf[halo:halo + ts, 0:W] = q_ref[...]
    buf[halo:halo + ts, W:2 * W] = k_ref[...]
    buf[halo:halo + ts, 2 * W:3 * W] = v_ref[...]

    ba = ba_ref[...]
    beta = 1.0 / (1.0 + jnp.exp(-ba))
    xs = ba + dtb_ref[...]
    softplus = jnp.maximum(xs, 0.0) + jnp.log1p(jnp.exp(-jnp.abs(xs)))
    g = -jnp.exp(alog_ref[...]) * softplus
    row_in_chunk = lax.broadcasted_iota(jnp.int32, (ts, V7X_LANES), 0) % C
    gc = g
    sh = 1
    while sh < C:
        gc = gc + jnp.where(row_in_chunk >= sh, pltpu.roll(gc, sh, axis=0), 0.0)
        sh *= 2
    gct = gc.T
    lane = lax.broadcasted_iota(jnp.int32, (ts, V7X_LANES), 1)
    for h in range(H):
        beta_rep[h] = jnp.broadcast_to(
            jnp.sum(jnp.where(lane == h, beta, 0.0), axis=-1, keepdims=True), (ts, V7X_LANES))
        gc_rep[h] = jnp.broadcast_to(
            jnp.sum(jnp.where(lane == H + h, gc, 0.0), axis=-1, keepdims=True), (ts, V7X_LANES))
        grow_s[h] = gct[H + h:H + h + 1, :]

    ii = lax.broadcasted_iota(jnp.int32, (C, C), 0)
    jj = lax.broadcasted_iota(jnp.int32, (C, C), 1)
    eye = (ii == jj).astype(F32)

    def conv_silu(col):
        cols = pl.ds(col, DN_HEAD_DIM)
        acc = buf[halo - K + 1:halo - K + 1 + ts, cols] * cw_ref[0:1, cols]
        for t in range(1, K):
            acc = acc + buf[halo - K + 1 + t:halo - K + 1 + t + ts, cols] * cw_ref[t:t + 1, cols]
        return _silu(acc)

    def head_body(h, carry):
        c0 = pl.multiple_of(h * DN_HEAD_DIM, DN_HEAD_DIM)
        hcols = pl.ds(c0, DN_HEAD_DIM)
        qc = conv_silu(c0)
        kc = conv_silu(pl.multiple_of(W + c0, DN_HEAD_DIM))
        vc = conv_silu(pl.multiple_of(2 * W + c0, DN_HEAD_DIM))
        qn = qc * lax.rsqrt(jnp.sum(qc * qc, axis=-1, keepdims=True) + EPS) * (DN_HEAD_DIM ** -0.5)
        kn = kc * lax.rsqrt(jnp.sum(kc * kc, axis=-1, keepdims=True) + EPS)
        bcol = beta_rep[h]
        gcol = gc_rep[h]
        egc = jnp.exp(gcol)
        grow_all = grow_s[h]
        rs = [slice(c * C, (c + 1) * C) for c in range(n_chunk)]
        kbs = [kn[r].astype(BF16) for r in rs]
        decs = [jnp.exp(jnp.minimum(gcol[r] - grow_all[:, r], 0.0)) for r in rs]
        kks = [_dot_nt(kb, kb) for kb in kbs]
        qks = [_dot_nt(qn[r].astype(BF16), kb) for r, kb in zip(rs, kbs)]
        ls = [jnp.where(ii > jj, kk * dec, 0.0) * bcol[r] for kk, dec, r in zip(kks, decs, rs)]
        for c, r in enumerate(rs):
            at_s[r, hcols] = jnp.where(ii >= jj, qks[c] * decs[c], 0.0).astype(BF16)
            qd_s[r, hcols] = (qn[r] * egc[r]).astype(BF16)
            kd = kn[r] * jnp.exp(gcol[r][C - 1:C, :] - gcol[r])
            kdt_s[r, hcols] = kd.T.astype(BF16)
            cdec_s[c, h] = egc[r][C - 1:C, :]
        t_invs = _unit_lower_inverses(ls, eye)
        us = [_dot3(t_inv, vc[r] * bcol[r]) for t_inv, r in zip(t_invs, rs)]
        ws = [_dot3(t_inv, kn[r] * (bcol[r] * egc[r])) for t_inv, r in zip(t_invs, rs)]
        for c, r in enumerate(rs):
            u_s[r, hcols] = us[c]
            w_s[r, hcols] = ws[c].astype(BF16)
        return carry

    lax.fori_loop(0, H, head_body, 0)

    def chunk_body(c, carry):
        r0 = pl.multiple_of(c * C, C)
        rows = pl.ds(r0, C)
        hs = range(H)
        cols = [slice(h * DN_HEAD_DIM, (h + 1) * DN_HEAD_DIM) for h in hs]
        sts = [state[h] for h in hs]
        sbs = [st.astype(BF16) for st in sts]
        wss = [_dot(w_s[rows, cols[h]], sbs[h]) for h in hs]
        qss = [_dot(qd_s[rows, cols[h]], sbs[h]) for h in hs]
        vbs = [(u_s[rows, cols[h]] - wss[h]).astype(BF16) for h in hs]
        avs = [_dot(at_s[rows, cols[h]], vbs[h]) for h in hs]
        kvs = [_dot(kdt_s[rows, cols[h]], vbs[h]) for h in hs]
        for h in hs:
            state[h] = sts[h] * cdec_s[c, h] + kvs[h]
            gated = _rms_rows(qss[h] + avs[h], ng_ref[...]) * _silu(z_ref[rows, cols[h]])
            o_ref[rows, cols[h]] = gated.astype(o_ref.dtype)
        return carry

    lax.fori_loop(0, n_chunk, chunk_body, 0)
    buf[0:halo, :] = buf[ts:ts + halo, :]


def gated_delta_net(p, ba, conv_w, a_log, dt_bias, norm_g, l, B, S, ts, col_block0):
    H = conv_w.shape[2] // (3 * DN_HEAD_DIM)
    W = H * DN_HEAD_DIM
    nt = S // ts
    NB = ba.shape[1]
    C = DN_CHUNK
    vmem = (2 * 4 * ts * W * 4 + 2 * ts * NB * 4 + 2 * ts * W * 2 + (ts + V7X_SUBLANES) * 3 * W * 4
            + H * DN_HEAD_DIM * DN_HEAD_DIM * 4 + 2 * H * ts * V7X_LANES * 4 + H * V7X_SUBLANES * ts * 4
            + ts * W * 4 + 4 * ts * W * 2 + 16 * ts * DN_HEAD_DIM * 4 + 32 * C * C * 4)

    def pblock(k):
        return pl.BlockSpec((ts, W), lambda b, s: (b * nt + s, col_block0 + k))

    return pl.pallas_call(
        _dn_body,
        grid=(B, nt),
        in_specs=[
            pblock(0), pblock(1), pblock(2), pblock(3),
            pl.BlockSpec((ts, NB), lambda b, s: (b * nt + s, 0)),
            pl.BlockSpec((None,) + conv_w.shape[1:], lambda b, s: (l, 0, 0)),
            pl.BlockSpec((None, 1, NB), lambda b, s: (l, 0, 0)),
            pl.BlockSpec((None, 1, NB), lambda b, s: (l, 0, 0)),
            pl.BlockSpec((None, 1, DN_HEAD_DIM), lambda b, s: (l, 0, 0)),
        ],
        out_specs=pl.BlockSpec((ts, W), lambda b, s: (b * nt + s, 0)),
        out_shape=jax.ShapeDtypeStruct((B * S, W), BF16),
        scratch_shapes=[
            pltpu.VMEM((ts + V7X_SUBLANES, 3 * W), F32),
            pltpu.VMEM((H, DN_HEAD_DIM, DN_HEAD_DIM), F32),
            pltpu.VMEM((H, ts, V7X_LANES), F32),
            pltpu.VMEM((H, ts, V7X_LANES), F32),
            pltpu.VMEM((H, 1, ts), F32),
            pltpu.VMEM((ts // C, H, 1, V7X_LANES), F32),
            pltpu.VMEM((ts, W), F32),
            pltpu.VMEM((ts, W), BF16),
            pltpu.VMEM((ts, W), BF16),
            pltpu.VMEM((ts, W), BF16),
            pltpu.VMEM((ts, W), BF16),
        ],
        compiler_params=_compiler_params(2, vmem),
        name="gated_delta_net",
    )(p, p, p, p, ba, conv_w, a_log, dt_bias, norm_g)


def _mix_out_body(x_ref, yp_ref, yd_ref, w1_ref, w2_ref, o_ref):
    o_ref[...] = x_ref[...] + _dot(yp_ref[...], w1_ref[...]) + _dot(yd_ref[...], w2_ref[...])


def mix_out(x, y_pool, y_dn, w, l, tm, tn):
    R, D = x.shape
    K1, K2 = y_pool.shape[1], y_dn.shape[1]
    assert K1 == K2
    vmem = 2 * (2 * tm * tn * 4 + tm * (K1 + K2) * 2 + (K1 + K2) * tn * 2) + 2 * tm * tn * 4
    return pl.pallas_call(
        _mix_out_body,
        grid=(R // tm, D // tn),
        in_specs=[
            pl.BlockSpec((tm, tn), lambda i, j: (i, j)),
            pl.BlockSpec((tm, K1), lambda i, j: (i, 0)),
            pl.BlockSpec((tm, K2), lambda i, j: (i, 0)),
            pl.BlockSpec((None, K1, tn), lambda i, j: (l, 0, j)),
            pl.BlockSpec((None, K2, tn), lambda i, j: (l, 1, j)),
        ],
        out_specs=pl.BlockSpec((tm, tn), lambda i, j: (i, j)),
        out_shape=jax.ShapeDtypeStruct((R, D), F32),
        compiler_params=_compiler_params(2, vmem),
        name="mix_out",
    )(x, y_pool, y_dn, w, w)


def _xattn_body(x_ref, g_ref, wq_ref, k_ref, v_ref, wo_ref, o_ref, h_ref):
    @pl.when(pl.program_id(1) == 0)
    def _():
        _norm_rows_into(h_ref, x_ref, g_ref)
        o_ref[...] = x_ref[...]

    dh = wq_ref.shape[1]
    q = _dot(h_ref[...], wq_ref[...])
    sc = _dot_nt(q.astype(BF16), k_ref[...]) * (dh ** -0.5)
    e = jnp.exp(sc - jnp.max(sc, axis=-1, keepdims=True))
    pr = e / jnp.sum(e, axis=-1, keepdims=True)
    o = _dot(pr.astype(BF16), v_ref[...])
    o_ref[...] += _dot(o.astype(BF16), wo_ref[...])


def cross_attention(x, g, w_xq, kv, w_xo, l, B, S, M, tm):
    R, D = x.shape
    dh = D // XA_HEADS
    tiles_per_batch = S // tm
    vmem = 3 * tm * D * 4 + tm * D * 2 + 4 * D * dh * 2 + 4 * M * dh * 2 + 3 * tm * dh * 4 + 3 * tm * M * 4 + tm * D * 4
    return pl.pallas_call(
        _xattn_body,
        grid=(R // tm, XA_HEADS),
        in_specs=[
            pl.BlockSpec((tm, D), lambda i, j: (i, 0), pipeline_mode=pl.Buffered(1)),
            pl.BlockSpec((None, 1, D), lambda i, j: (l, 0, 0)),
            pl.BlockSpec((None, D, dh), lambda i, j: (l, 0, j)),
            pl.BlockSpec((None, M, dh), lambda i, j: (l, i // tiles_per_batch, j)),
            pl.BlockSpec((None, M, dh), lambda i, j: (l, i // tiles_per_batch, XA_HEADS + j)),
            pl.BlockSpec((None, dh, D), lambda i, j: (l, j, 0)),
        ],
        out_specs=pl.BlockSpec((tm, D), lambda i, j: (i, 0)),
        out_shape=jax.ShapeDtypeStruct((R, D), F32),
        scratch_shapes=[pltpu.VMEM((tm, D), BF16)],
        compiler_params=_compiler_params(2, vmem),
        name="cross_attention",
    )(x, g, w_xq, kv, kv, w_xo)


def _ffn_body(x_ref, g_ref, wg_ref, wu_ref, cw_ref, cb_ref, wd_ref, o_ref, h_ref, gbuf, carry, *, tiles_per_seq):
    i = pl.program_id(0)
    j = pl.program_id(1)
    tm = x_ref.shape[0]
    K = cw_ref.shape[0]
    halo = V7X_SUBLANES

    @pl.when(j == 0)
    def _():
        _norm_rows_into(h_ref, x_ref, g_ref)
        o_ref[...] = x_ref[...]

    h = h_ref[...]
    gate = _dot(h, wg_ref[...])
    up = _dot(h, wu_ref[...])
    first = (i % tiles_per_seq) == 0
    gbuf[0:halo, :] = jnp.where(first, 0.0, carry[j])
    gbuf[halo:halo + tm, :] = gate
    carry[j] = gate[tm - halo:tm, :]
    conv = cb_ref[...] + gbuf[halo - K + 1:halo - K + 1 + tm, :] * cw_ref[0:1, :]
    for t in range(1, K):
        conv = conv + gbuf[halo - K + 1 + t:halo - K + 1 + t + tm, :] * cw_ref[t:t + 1, :]
    act = _silu(conv) * up
    o_ref[...] += _dot(act.astype(BF16), wd_ref[...])


def conv_glu_ffn(x, g, w_gate, w_up, conv_w, conv_b, w_down, l, S, tm, tf):
    R, D = x.shape
    F = w_gate.shape[2]
    K = conv_w.shape[1]
    nf = F // tf
    vmem = (3 * tm * D * 4 + tm * D * 2 + 2 * 3 * D * tf * 2 + (tm + V7X_SUBLANES) * tf * 4
            + nf * V7X_SUBLANES * tf * 4 + 4 * tm * tf * 4)
    return pl.pallas_call(
        functools.partial(_ffn_body, tiles_per_seq=S // tm),
        grid=(R // tm, nf),
        in_specs=[
            pl.BlockSpec((tm, D), lambda i, j: (i, 0), pipeline_mode=pl.Buffered(1)),
            pl.BlockSpec((None, 1, D), lambda i, j: (l, 0, 0)),
            pl.BlockSpec((None, D, tf), lambda i, j: (l, 0, j)),
            pl.BlockSpec((None, D, tf), lambda i, j: (l, 0, j)),
            pl.BlockSpec((None, K, tf), lambda i, j: (l, 0, j)),
            pl.BlockSpec((None, 1, tf), lambda i, j: (l, 0, j)),
            pl.BlockSpec((None, tf, D), lambda i, j: (l, j, 0)),
        ],
        out_specs=pl.BlockSpec((tm, D), lambda i, j: (i, 0)),
        out_shape=jax.ShapeDtypeStruct((R, D), F32),
        scratch_shapes=[
            pltpu.VMEM((tm, D), BF16),
            pltpu.VMEM((tm + V7X_SUBLANES, tf), F32),
            pltpu.VMEM((nf, V7X_SUBLANES, tf), F32),
        ],
        compiler_params=_compiler_params(2, vmem),
        name="conv_glu_ffn",
    )(x, g, w_gate, w_up, conv_w, conv_b, w_down)


def _final_norm_body(x_ref, g_ref, o_ref):
    _norm_rows_into(o_ref, x_ref, g_ref)


def final_norm(x, g, tm):
    R, D = x.shape
    return pl.pallas_call(
        _final_norm_body,
        grid=(R // tm,),
        in_specs=[pl.BlockSpec((tm, D), lambda i: (i, 0)), pl.BlockSpec((1, D), lambda i: (0, 0))],
        out_specs=pl.BlockSpec((tm, D), lambda i: (i, 0)),
        out_shape=jax.ShapeDtypeStruct((R, D), F32),
        compiler_params=_compiler_params(1, 4 * tm * D * 4),
        name="final_norm",
    )(x, g)


def _tiles(S):
    return dict(tm=min(1024, S), ts_pool=min(512, S), ts_dn=min(512, S))


def kernel(x, mem, mix_norm_g, w_in, w_pool, pool_scale, dn_conv_w, dn_a_log, dn_dt_bias, dn_norm_g, w_mix_out,
           xa_norm_g, mem_norm_g, w_xq, w_xkv, w_xo, ffn_norm_g, w_gate, w_up, ffn_conv_w, ffn_conv_b, w_down,
           final_norm_g):
    B, S, D = x.shape
    M = mem.shape[1]
    depth = w_in.shape[0]
    H = dn_a_log.shape[1]
    dn_w = H * DN_HEAD_DIM
    pool_w = w_pool.shape[1] * w_pool.shape[2]
    main = pool_w + 4 * dn_w
    assert pool_w == dn_w and w_in.shape[2] == main + 2 * H and 2 * H <= V7X_LANES
    t = _tiles(S)

    w_in_main = w_in[:, :, :main].astype(BF16)
    w_ba = jnp.pad(w_in[:, :, main:], ((0, 0), (0, 0), (0, V7X_LANES - 2 * H))).astype(BF16)
    pad_gate = lambda a: jnp.pad(a, ((0, 0), (H, V7X_LANES - 2 * H)))[:, None, :]
    a_log_p, dt_bias_p = pad_gate(dn_a_log), pad_gate(dn_dt_bias)
    w_pool_b = w_pool.astype(BF16)
    w_mix_b = w_mix_out.astype(BF16)
    w_xq_b, w_xkv_b, w_xo_b = w_xq.astype(BF16), w_xkv.astype(BF16), w_xo.astype(BF16)
    w_gate_b, w_up_b, w_down_b = w_gate.astype(BF16), w_up.astype(BF16), w_down.astype(BF16)
    row = lambda a: a[:, None, :]

    xr = x.reshape(B * S, D)
    kv = norm_matmul(mem.reshape(B * M, D), mem_norm_g[None, :], w_xkv_b, BF16, tm=min(1024, B * M), tn=512)
    for l in range(depth):
        p, ba = in_proj(xr, mix_norm_g[l][None, :], w_in_main, w_ba, l, tm=t["tm"], tn=512)
        y_pool = pool_mixer(p, w_pool_b, row(pool_scale), l, B, S, t["ts_pool"])
        y_dn = gated_delta_net(p, ba, dn_conv_w, a_log_p, dt_bias_p, row(dn_norm_g), l, B, S, t["ts_dn"],
                               col_block0=pool_w // dn_w)
        xr = mix_out(xr, y_pool, y_dn, w_mix_b, l, tm=t["tm"], tn=1024)
        xr = cross_attention(xr, row(xa_norm_g), w_xq_b, kv, w_xo_b, l, B, S, M, tm=t["tm"])
        xr = conv_glu_ffn(xr, row(ffn_norm_g), w_gate_b, w_up_b, ffn_conv_w, row(ffn_conv_b), w_down_b, l, S,
                          tm=t["tm"], tf=512)
    return final_norm(xr, final_norm_g[None, :], tm=t["tm"]).reshape(B, S, D)
```

```python
import functools
import math

import jax
import jax.numpy as jnp
from jax import lax
from jax.experimental import pallas as pl
from jax.experimental.pallas import tpu as pltpu

F32 = jnp.float32
BF16 = jnp.bfloat16

EPS = 1e-6
POOL_WINDOWS = (2, 4, 8, 16)
POOL_HALO = 16
DN_HEAD_DIM = 128
DN_CHUNK = 128
DN_HEADS_PER_STEP = 2
XA_HEADS = 4

V7X_VMEM_BYTES = 64 * 1024 * 1024
V7X_LANES = 128
V7X_SUBLANES = 8
NORM_ROWS = 64
IN_PROJ_SUB_ROWS = 128


def _compiler_params(n_axes, vmem_bytes):
    limit = min(int(vmem_bytes * 1.25) + (4 << 20), V7X_VMEM_BYTES * 7 // 8)
    return pltpu.CompilerParams(dimension_semantics=("arbitrary",) * n_axes, vmem_limit_bytes=limit)


def _dot(a, b):
    return jnp.dot(a, b, preferred_element_type=F32)


def _dot_nt(a, b):
    return lax.dot_general(a, b, (((1,), (1,)), ((), ())), preferred_element_type=F32)


def _split_bf16(x):
    hi = x.astype(BF16)
    lo = (x - hi.astype(F32)).astype(BF16)
    return hi, lo


def _dot3(a, b):
    ah, al = _split_bf16(a)
    bh, bl = _split_bf16(b)
    return _dot(ah, bh) + _dot(ah, bl) + _dot(al, bh)


def _silu(x):
    return x * (1.0 / (1.0 + jnp.exp(-x)))


def _rms_rows(x, g):
    ms = jnp.mean(x * x, axis=-1, keepdims=True)
    return x * lax.rsqrt(ms + EPS) * g


def _norm_rows_into(h_ref, x_ref, g_ref):
    g = g_ref[...]

    def body(r, c):
        rows = pl.ds(pl.multiple_of(r * NORM_ROWS, NORM_ROWS), NORM_ROWS)
        h_ref[rows, :] = _rms_rows(x_ref[rows, :], g).astype(h_ref.dtype)
        return c

    lax.fori_loop(0, x_ref.shape[0] // NORM_ROWS, body, 0)


def _norm_matmul_body(x_ref, g_ref, w_ref, o_ref, h_ref):
    @pl.when((pl.program_id(1) == 0) & (pl.program_id(2) == 0))
    def _():
        _norm_rows_into(h_ref, x_ref, g_ref)

    o_ref[...] = _dot(h_ref[...], w_ref[...]).astype(o_ref.dtype)


def norm_matmul(x, g, w, out_dtype, tm, tn):
    R, K = x.shape
    L, _, N = w.shape
    osz = jnp.dtype(out_dtype).itemsize
    vmem = 2 * tm * K * 4 + tm * K * 2 + 2 * K * tn * 2 + 2 * tm * tn * osz + tm * tn * 4
    return pl.pallas_call(
        _norm_matmul_body,
        grid=(R // tm, L, N // tn),
        in_specs=[
            pl.BlockSpec((tm, K), lambda i, l, j: (i, 0)),
            pl.BlockSpec((1, K), lambda i, l, j: (0, 0)),
            pl.BlockSpec((None, K, tn), lambda i, l, j: (l, 0, j)),
        ],
        out_specs=pl.BlockSpec((None, tm, tn), lambda i, l, j: (l, i, j)),
        out_shape=jax.ShapeDtypeStruct((L, R, N), out_dtype),
        scratch_shapes=[pltpu.VMEM((tm, K), BF16)],
        compiler_params=_compiler_params(3, vmem),
        name="norm_matmul",
    )(x, g, w)


def _in_proj_body(x_ref, g_ref, w_ref, cw_ref, p_ref, h_ref, cbuf, carry, *, tiles_per_seq, conv0, tiles_per_part):
    i = pl.program_id(0)
    j = pl.program_id(1)
    tm, tn = p_ref.shape
    K = cw_ref.shape[0]
    halo = V7X_SUBLANES

    @pl.when(j == 0)
    def _():
        _norm_rows_into(h_ref, x_ref, g_ref)

    is_conv = (j >= conv0) & (j < conv0 + 3 * tiles_per_part)

    @pl.when(jnp.logical_not(is_conv))
    def _():
        p_ref[...] = _dot(h_ref[...], w_ref[...])

    @pl.when(is_conv)
    def _():
        cj = j - conv0
        part = cj // tiles_per_part
        q_scale = jnp.where(part == 0, DN_HEAD_DIM ** -0.5, 1.0)
        cbuf[0:halo, :] = jnp.where((i % tiles_per_seq) == 0, 0.0, carry[cj])
        for r0 in range(0, tm, IN_PROJ_SUB_ROWS):
            acc = _dot(h_ref[r0:r0 + IN_PROJ_SUB_ROWS, :], w_ref[...])
            cbuf[halo + r0:halo + r0 + IN_PROJ_SUB_ROWS, :] = acc
            y = acc * cw_ref[K - 1:K, :]
            for t in range(K - 2, -1, -1):
                y = y + cbuf[halo - K + 1 + t + r0:halo - K + 1 + t + r0 + IN_PROJ_SUB_ROWS, :] * cw_ref[t:t + 1, :]
            y = _silu(y)
            for c in range(0, tn, DN_HEAD_DIM):
                yh = y[:, c:c + DN_HEAD_DIM]
                inv = lax.rsqrt(jnp.sum(yh * yh, axis=-1, keepdims=True) + EPS) * q_scale
                p_ref[r0:r0 + IN_PROJ_SUB_ROWS, c:c + DN_HEAD_DIM] = yh * jnp.where(part == 2, 1.0, inv)
        carry[cj] = cbuf[tm:tm + halo, :]


def in_proj(x, g, w, conv_w, l, S, tm, tn, conv_col0):
    R, D = x.shape
    N = w.shape[2]
    K, conv_cols = conv_w.shape[1:]
    assert conv_col0 % tn == 0 and (conv_cols // 3) % tn == 0 and tn % DN_HEAD_DIM == 0
    conv0, n_conv = conv_col0 // tn, conv_cols // tn
    vmem = (2 * tm * D * 4 + tm * D * 2 + 2 * D * tn * 2 + 3 * tm * tn * 4 + (tm + V7X_SUBLANES) * tn * 4
            + n_conv * V7X_SUBLANES * tn * 4 + 3 * tm * tn * 4)
    return pl.pallas_call(
        functools.partial(_in_proj_body, tiles_per_seq=S // tm, conv0=conv0, tiles_per_part=n_conv // 3),
        grid=(R // tm, N // tn),
        in_specs=[
            pl.BlockSpec((tm, D), lambda i, j: (i, 0)),
            pl.BlockSpec((1, D), lambda i, j: (0, 0)),
            pl.BlockSpec((None, D, tn), lambda i, j: (l, 0, j)),
            pl.BlockSpec((None, K, tn), lambda i, j: (l, 0, jnp.clip(j - conv0, 0, n_conv - 1))),
        ],
        out_specs=pl.BlockSpec((tm, tn), lambda i, j: (i, j)),
        out_shape=jax.ShapeDtypeStruct((R, N), F32),
        scratch_shapes=[
            pltpu.VMEM((tm, D), BF16),
            pltpu.VMEM((tm + V7X_SUBLANES, tn), F32),
            pltpu.VMEM((n_conv, V7X_SUBLANES, tn), F32),
        ],
        compiler_params=_compiler_params(2, vmem),
        name="in_proj",
    )(x, g, w, conv_w)


def _pool_body(u_ref, wp_ref, sc_ref, o_ref, buf):
    s = pl.program_id(1)
    ts = u_ref.shape[0]
    G = wp_ref.shape[1]

    @pl.when(s == 0)
    def _():
        buf[0:POOL_HALO, :] = jnp.zeros((POOL_HALO, buf.shape[1]), F32)

    buf[POOL_HALO:POOL_HALO + ts, :] = u_ref[...]
    pos = (s * ts + 1 + lax.broadcasted_iota(jnp.int32, (ts, 1), 0)).astype(F32)
    for i, w in enumerate(POOL_WINDOWS):
        cols = slice(i * G, (i + 1) * G)
        u = buf[POOL_HALO:POOL_HALO + ts, cols]
        acc = u
        for k in range(1, w):
            acc = acc + buf[POOL_HALO - k:POOL_HALO - k + ts, cols]
        mixed = acc / jnp.minimum(pos, float(w)) - u
        y = _dot(mixed.astype(BF16), wp_ref[i]) * sc_ref[:, cols]
        o_ref[:, cols] = y.astype(o_ref.dtype)
    buf[0:POOL_HALO, :] = buf[ts:ts + POOL_HALO, :]


def pool_mixer(p, w_pool, pool_scale, l, B, S, ts):
    n_win, G, _ = w_pool.shape[1:]
    W = n_win * G
    nt = S // ts
    vmem = 2 * ts * W * 4 + (ts + POOL_HALO) * W * 4 + 2 * n_win * G * G * 2 + 2 * ts * W * 2 + 4 * ts * G * 4
    return pl.pallas_call(
        _pool_body,
        grid=(B, nt),
        in_specs=[
            pl.BlockSpec((ts, W), lambda b, s: (b * nt + s, 0)),
            pl.BlockSpec((None, n_win, G, G), lambda b, s: (l, 0, 0, 0)),
            pl.BlockSpec((None, 1, W), lambda b, s: (l, 0, 0)),
        ],
        out_specs=pl.BlockSpec((ts, W), lambda b, s: (b * nt + s, 0)),
        out_shape=jax.ShapeDtypeStruct((B * S, W), BF16),
        scratch_shapes=[pltpu.VMEM((ts + POOL_HALO, W), F32)],
        compiler_params=_compiler_params(2, vmem),
        name="pool_mixer",
    )(p, w_pool, pool_scale)


def _unit_lower_inverses(ls, eye):
    C = ls[0].shape[0]
    n_iter = int(math.log2(C)) - 1
    xs = [eye - l_mat for l_mat in ls]
    lbs = [l_mat.astype(BF16) for l_mat in ls]
    for it in range(n_iter):
        if it < n_iter - 1:
            lxs = [_dot(lb, x.astype(BF16)) for lb, x in zip(lbs, xs)]
        else:
            lxs = [_dot3(l_mat, x) for l_mat, x in zip(ls, xs)]
        es = [eye - x - lx for x, lx in zip(xs, lxs)]
        xs = [x + _dot(x.astype(BF16), e.astype(BF16)) for x, e in zip(xs, es)]
    return xs


def _dn_body(q_ref, k_ref, v_ref, z_ref, ba_ref, alog_ref, dtb_ref, ng_ref, o_ref,
             state, beta_rep, gc_rep, grow_s, cdec_s, u_s, w_s, qd_s, at_s, kdt_s):
    s = pl.program_id(1)
    ts, W = q_ref.shape
    H = W // DN_HEAD_DIM
    C = DN_CHUNK
    n_chunk = ts // C

    @pl.when(s == 0)
    def _():
        state[...] = jnp.zeros(state.shape, F32)

    ba = ba_ref[...]
    beta = 1.0 / (1.0 + jnp.exp(-ba))
    xs = ba + dtb_ref[...]
    softplus = jnp.maximum(xs, 0.0) + jnp.log1p(jnp.exp(-jnp.abs(xs)))
    g = -jnp.exp(alog_ref[...]) * softplus
    row_in_chunk = lax.broadcasted_iota(jnp.int32, (ts, V7X_LANES), 0) % C
    gc = g
    sh = 1
    while sh < C:
        gc = gc + jnp.where(row_in_chunk >= sh, pltpu.roll(gc, sh, axis=0), 0.0)
        sh *= 2
    gct = gc.T
    lane = lax.broadcasted_iota(jnp.int32, (ts, V7X_LANES), 1)
    for h in range(H):
        beta_rep[h] = jnp.broadcast_to(
            jnp.sum(jnp.where(lane == h, beta, 0.0), axis=-1, keepdims=True), (ts, V7X_LANES))
        gc_rep[h] = jnp.broadcast_to(
            jnp.sum(jnp.where(lane == H + h, gc, 0.0), axis=-1, keepdims=True), (ts, V7X_LANES))
        grow_s[h] = gct[H + h:H + h + 1, :]

    ii = lax.broadcasted_iota(jnp.int32, (C, C), 0)
    jj = lax.broadcasted_iota(jnp.int32, (C, C), 1)
    eye = (ii == jj).astype(F32)

    def head_group_body(hg, carry):
        pairs = []
        for hh in range(DN_HEADS_PER_STEP):
            h = hg * DN_HEADS_PER_STEP + hh
            cols = pl.ds(pl.multiple_of(h * DN_HEAD_DIM, DN_HEAD_DIM), DN_HEAD_DIM)
            bcol, gcol, grow_all = beta_rep[h], gc_rep[h], grow_s[h]
            egc = jnp.exp(gcol)
            for c in range(n_chunk):
                r = slice(c * C, (c + 1) * C)
                pairs.append(dict(h=h, c=c, r=r, cols=cols, q=q_ref[r, cols], k=k_ref[r, cols], v=v_ref[r, cols],
                                  b=bcol[r], g=gcol[r], e=egc[r], grow=grow_all[:, r]))
        kbs = [p["k"].astype(BF16) for p in pairs]
        decs = [jnp.exp(jnp.minimum(p["g"] - p["grow"], 0.0)) for p in pairs]
        kks = [_dot_nt(kb, kb) for kb in kbs]
        qks = [_dot_nt(p["q"].astype(BF16), kb) for p, kb in zip(pairs, kbs)]
        ls = [jnp.where(ii > jj, kk * dec, 0.0) * p["b"] for kk, dec, p in zip(kks, decs, pairs)]
        for p, qk, dec in zip(pairs, qks, decs):
            r, cols = p["r"], p["cols"]
            at_s[r, cols] = jnp.where(ii >= jj, qk * dec, 0.0).astype(BF16)
            qd_s[r, cols] = (p["q"] * p["e"]).astype(BF16)
            kd = p["k"] * jnp.exp(p["g"][C - 1:C, :] - p["g"])
            kdt_s[r, cols] = kd.T.astype(BF16)
            cdec_s[p["c"], p["h"]] = p["e"][C - 1:C, :]
        t_invs = _unit_lower_inverses(ls, eye)
        rhs = [jnp.concatenate([p["v"] * p["b"], p["k"] * (p["b"] * p["e"])], axis=1) for p in pairs]
        sols = [_dot3(t_inv, b) for t_inv, b in zip(t_invs, rhs)]
        for p, sol in zip(pairs, sols):
            u_s[p["r"], p["cols"]] = sol[:, :DN_HEAD_DIM]
            w_s[p["r"], p["cols"]] = sol[:, DN_HEAD_DIM:].astype(BF16)
        return carry

    lax.fori_loop(0, H // DN_HEADS_PER_STEP, head_group_body, 0)

    def chunk_body(c, carry):
        r0 = pl.multiple_of(c * C, C)
        rows = pl.ds(r0, C)
        hs = range(H)
        cols = [slice(h * DN_HEAD_DIM, (h + 1) * DN_HEAD_DIM) for h in hs]
        sts = [state[h] for h in hs]
        sbs = [st.astype(BF16) for st in sts]
        wss = [_dot(w_s[rows, cols[h]], sbs[h]) for h in hs]
        qss = [_dot(qd_s[rows, cols[h]], sbs[h]) for h in hs]
        vbs = [(u_s[rows, cols[h]] - wss[h]).astype(BF16) for h in hs]
        avs = [_dot(at_s[rows, cols[h]], vbs[h]) for h in hs]
        kvs = [_dot(kdt_s[rows, cols[h]], vbs[h]) for h in hs]
        for h in hs:
            state[h] = sts[h] * cdec_s[c, h] + kvs[h]
            gated = _rms_rows(qss[h] + avs[h], ng_ref[...]) * _silu(z_ref[rows, cols[h]])
            o_ref[rows, cols[h]] = gated.astype(o_ref.dtype)
        return carry

    lax.fori_loop(0, n_chunk, chunk_body, 0)


def gated_delta_net(p, a_log, dt_bias, norm_g, l, B, S, ts, col_block0, H):
    W = H * DN_HEAD_DIM
    nt = S // ts
    NB = V7X_LANES
    gate_block = (col_block0 + 4) * (W // NB)
    C = DN_CHUNK
    vmem = (2 * 4 * ts * W * 4 + 2 * ts * NB * 4 + 2 * ts * W * 2
            + H * DN_HEAD_DIM * DN_HEAD_DIM * 4 + 2 * H * ts * V7X_LANES * 4 + H * V7X_SUBLANES * ts * 4
            + ts * W * 4 + 4 * ts * W * 2 + 48 * C * C * 4 * DN_HEADS_PER_STEP)

    def pblock(k):
        return pl.BlockSpec((ts, W), lambda b, s: (b * nt + s, col_block0 + k))

    return pl.pallas_call(
        _dn_body,
        grid=(B, nt),
        in_specs=[
            pblock(0), pblock(1), pblock(2), pblock(3),
            pl.BlockSpec((ts, NB), lambda b, s: (b * nt + s, gate_block)),
            pl.BlockSpec((None, 1, NB), lambda b, s: (l, 0, 0)),
            pl.BlockSpec((None, 1, NB), lambda b, s: (l, 0, 0)),
            pl.BlockSpec((None, 1, DN_HEAD_DIM), lambda b, s: (l, 0, 0)),
        ],
        out_specs=pl.BlockSpec((ts, W), lambda b, s: (b * nt + s, 0)),
        out_shape=jax.ShapeDtypeStruct((B * S, W), BF16),
        scratch_shapes=[
            pltpu.VMEM((H, DN_HEAD_DIM, DN_HEAD_DIM), F32),
            pltpu.VMEM((H, ts, V7X_LANES), F32),
            pltpu.VMEM((H, ts, V7X_LANES), F32),
            pltpu.VMEM((H, 1, ts), F32),
            pltpu.VMEM((ts // C, H, 1, V7X_LANES), F32),
            pltpu.VMEM((ts, W), F32),
            pltpu.VMEM((ts, W), BF16),
            pltpu.VMEM((ts, W), BF16),
            pltpu.VMEM((ts, W), BF16),
            pltpu.VMEM((ts, W), BF16),
        ],
        compiler_params=_compiler_params(2, vmem),
        name="gated_delta_net",
    )(p, p, p, p, p, a_log, dt_bias, norm_g)


def _mix_out_body(x_ref, yp_ref, yd_ref, w1_ref, w2_ref, o_ref):
    o_ref[...] = x_ref[...] + _dot(yp_ref[...], w1_ref[...]) + _dot(yd_ref[...], w2_ref[...])


def mix_out(x, y_pool, y_dn, w, l, tm, tn):
    R, D = x.shape
    K1, K2 = y_pool.shape[1], y_dn.shape[1]
    assert K1 == K2
    vmem = 2 * (2 * tm * tn * 4 + tm * (K1 + K2) * 2 + (K1 + K2) * tn * 2) + 2 * tm * tn * 4
    return pl.pallas_call(
        _mix_out_body,
        grid=(R // tm, D // tn),
        in_specs=[
            pl.BlockSpec((tm, tn), lambda i, j: (i, j)),
            pl.BlockSpec((tm, K1), lambda i, j: (i, 0)),
            pl.BlockSpec((tm, K2), lambda i, j: (i, 0)),
            pl.BlockSpec((None, K1, tn), lambda i, j: (l, 0, j)),
            pl.BlockSpec((None, K2, tn), lambda i, j: (l, 1, j)),
        ],
        out_specs=pl.BlockSpec((tm, tn), lambda i, j: (i, j)),
        out_shape=jax.ShapeDtypeStruct((R, D), F32),
        compiler_params=_compiler_params(2, vmem),
        name="mix_out",
    )(x, y_pool, y_dn, w, w)


def _xattn_body(x_ref, g_ref, wq_ref, k_ref, v_ref, wo_ref, o_ref, h_ref, a_ref):
    j = pl.program_id(1)
    dh = wq_ref.shape[1]
    tn = wo_ref.shape[1]

    @pl.when(j == 0)
    def _():
        _norm_rows_into(h_ref, x_ref, g_ref)

    @pl.when(j < XA_HEADS)
    def _():
        q = _dot(h_ref[...], wq_ref[...])
        sc = _dot_nt(q.astype(BF16), k_ref[...]) * (dh ** -0.5)
        e = jnp.exp(sc - jnp.max(sc, axis=-1, keepdims=True))
        pr = e / jnp.sum(e, axis=-1, keepdims=True)
        a_ref[:, pl.ds(pl.multiple_of(j * dh, dh), dh)] = _dot(pr.astype(BF16), v_ref[...]).astype(BF16)

    @pl.when(j >= XA_HEADS)
    def _():
        cols = pl.ds(pl.multiple_of((j - XA_HEADS) * tn, tn), tn)
        o_ref[...] = x_ref[:, cols] + _dot(a_ref[...], wo_ref[...])


def cross_attention(x, g, w_xq, kv, w_xo, l, B, S, M, tm, tn):
    R, D = x.shape
    dh = D // XA_HEADS
    tiles_per_batch = S // tm
    last = XA_HEADS - 1
    head = lambda j: jnp.minimum(j, last)
    otile = lambda j: jnp.maximum(j - XA_HEADS, 0)
    vmem = (2 * tm * D * 4 + 2 * tm * D * 2 + 4 * D * dh * 2 + 4 * M * dh * 2 + 4 * D * tn * 2 + 3 * tm * tn * 4
            + 3 * tm * dh * 4 + 3 * tm * M * 4)
    return pl.pallas_call(
        _xattn_body,
        grid=(R // tm, XA_HEADS + D // tn),
        in_specs=[
            pl.BlockSpec((tm, D), lambda i, j: (i, 0)),
            pl.BlockSpec((None, 1, D), lambda i, j: (l, 0, 0)),
            pl.BlockSpec((None, D, dh), lambda i, j: (l, 0, head(j))),
            pl.BlockSpec((None, M, dh), lambda i, j: (l, i // tiles_per_batch, head(j))),
            pl.BlockSpec((None, M, dh), lambda i, j: (l, i // tiles_per_batch, XA_HEADS + head(j))),
            pl.BlockSpec((None, D, tn), lambda i, j: (l, 0, otile(j))),
        ],
        out_specs=pl.BlockSpec((tm, tn), lambda i, j: (i, otile(j))),
        out_shape=jax.ShapeDtypeStruct((R, D), F32),
        scratch_shapes=[pltpu.VMEM((tm, D), BF16), pltpu.VMEM((tm, D), BF16)],
        compiler_params=_compiler_params(2, vmem),
        name="cross_attention",
    )(x, g, w_xq, kv, kv, w_xo)


def _ffn_body(x_ref, g_ref, wg_ref, wu_ref, cw_ref, cb_ref, wd_ref, o_ref, h_ref, gbuf, carry, *, tiles_per_seq):
    i = pl.program_id(0)
    j = pl.program_id(1)
    tm = x_ref.shape[0]
    K = cw_ref.shape[0]
    halo = V7X_SUBLANES

    @pl.when(j == 0)
    def _():
        _norm_rows_into(h_ref, x_ref, g_ref)
        o_ref[...] = x_ref[...]

    h = h_ref[...]
    gate = _dot(h, wg_ref[...])
    up = _dot(h, wu_ref[...])
    first = (i % tiles_per_seq) == 0
    gbuf[0:halo, :] = jnp.where(first, 0.0, carry[j])
    gbuf[halo:halo + tm, :] = gate
    carry[j] = gate[tm - halo:tm, :]
    conv = cb_ref[...] + gbuf[halo - K + 1:halo - K + 1 + tm, :] * cw_ref[0:1, :]
    for t in range(1, K):
        conv = conv + gbuf[halo - K + 1 + t:halo - K + 1 + t + tm, :] * cw_ref[t:t + 1, :]
    act = _silu(conv) * up
    o_ref[...] += _dot(act.astype(BF16), wd_ref[...])


def conv_glu_ffn(x, g, w_gate, w_up, conv_w, conv_b, w_down, l, S, tm, tf):
    R, D = x.shape
    F = w_gate.shape[2]
    K = conv_w.shape[1]
    nf = F // tf
    vmem = (3 * tm * D * 4 + tm * D * 2 + 2 * 3 * D * tf * 2 + (tm + V7X_SUBLANES) * tf * 4
            + nf * V7X_SUBLANES * tf * 4 + 4 * tm * tf * 4)
    return pl.pallas_call(
        functools.partial(_ffn_body, tiles_per_seq=S // tm),
        grid=(R // tm, nf),
        in_specs=[
            pl.BlockSpec((tm, D), lambda i, j: (i, 0), pipeline_mode=pl.Buffered(1)),
            pl.BlockSpec((None, 1, D), lambda i, j: (l, 0, 0)),
            pl.BlockSpec((None, D, tf), lambda i, j: (l, 0, j)),
            pl.BlockSpec((None, D, tf), lambda i, j: (l, 0, j)),
            pl.BlockSpec((None, K, tf), lambda i, j: (l, 0, j)),
            pl.BlockSpec((None, 1, tf), lambda i, j: (l, 0, j)),
            pl.BlockSpec((None, tf, D), lambda i, j: (l, j, 0)),
        ],
        out_specs=pl.BlockSpec((tm, D), lambda i, j: (i, 0)),
        out_shape=jax.ShapeDtypeStruct((R, D), F32),
        scratch_shapes=[
            pltpu.VMEM((tm, D), BF16),
            pltpu.VMEM((tm + V7X_SUBLANES, tf), F32),
            pltpu.VMEM((nf, V7X_SUBLANES, tf), F32),
        ],
        compiler_params=_compiler_params(2, vmem),
        name="conv_glu_ffn",
    )(x, g, w_gate, w_up, conv_w, conv_b, w_down)


def _final_norm_body(x_ref, g_ref, o_ref):
    _norm_rows_into(o_ref, x_ref, g_ref)


def final_norm(x, g, tm):
    R, D = x.shape
    return pl.pallas_call(
        _final_norm_body,
        grid=(R // tm,),
        in_specs=[pl.BlockSpec((tm, D), lambda i: (i, 0)), pl.BlockSpec((1, D), lambda i: (0, 0))],
        out_specs=pl.BlockSpec((tm, D), lambda i: (i, 0)),
        out_shape=jax.ShapeDtypeStruct((R, D), F32),
        compiler_params=_compiler_params(1, 4 * tm * D * 4),
        name="final_norm",
    )(x, g)


def _tiles(S):
    return dict(tm=min(1024, S), ts_pool=min(512, S), ts_dn=min(512, S))


def kernel(x, mem, mix_norm_g, w_in, w_pool, pool_scale, dn_conv_w, dn_a_log, dn_dt_bias, dn_norm_g, w_mix_out,
           xa_norm_g, mem_norm_g, w_xq, w_xkv, w_xo, ffn_norm_g, w_gate, w_up, ffn_conv_w, ffn_conv_b, w_down,
           final_norm_g):
    B, S, D = x.shape
    M = mem.shape[1]
    depth = w_in.shape[0]
    H = dn_a_log.shape[1]
    dn_w = H * DN_HEAD_DIM
    pool_w = w_pool.shape[1] * w_pool.shape[2]
    main = pool_w + 4 * dn_w
    assert pool_w == dn_w and w_in.shape[2] == main + 2 * H and 2 * H <= V7X_LANES
    t = _tiles(S)

    tn_in = 512
    n_in = -(-w_in.shape[2] // tn_in) * tn_in
    w_in_b = jnp.pad(w_in, ((0, 0), (0, 0), (0, n_in - w_in.shape[2]))).astype(BF16)
    pad_gate = lambda a: jnp.pad(a, ((0, 0), (H, V7X_LANES - 2 * H)))[:, None, :]
    a_log_p, dt_bias_p = pad_gate(dn_a_log), pad_gate(dn_dt_bias)
    w_pool_b = w_pool.astype(BF16)
    w_mix_b = w_mix_out.astype(BF16)
    w_xq_b, w_xkv_b, w_xo_b = w_xq.astype(BF16), w_xkv.astype(BF16), w_xo.astype(BF16)
    w_gate_b, w_up_b, w_down_b = w_gate.astype(BF16), w_up.astype(BF16), w_down.astype(BF16)
    row = lambda a: a[:, None, :]

    xr = x.reshape(B * S, D)
    kv = norm_matmul(mem.reshape(B * M, D), mem_norm_g[None, :], w_xkv_b, BF16, tm=min(1024, B * M), tn=512)
    for l in range(depth):
        p = in_proj(xr, mix_norm_g[l][None, :], w_in_b, dn_conv_w, l, S, tm=t["tm"], tn=tn_in, conv_col0=pool_w)
        y_pool = pool_mixer(p, w_pool_b, row(pool_scale), l, B, S, t["ts_pool"])
        y_dn = gated_delta_net(p, a_log_p, dt_bias_p, row(dn_norm_g), l, B, S, t["ts_dn"],
                               col_block0=pool_w // dn_w, H=H)
        xr = mix_out(xr, y_pool, y_dn, w_mix_b, l, tm=t["tm"], tn=1024)
        xr = cross_attention(xr, row(xa_norm_g), w_xq_b, kv, w_xo_b, l, B, S, M, tm=t["tm"], tn=512)
        xr = conv_glu_ffn(xr, row(ffn_norm_g), w_gate_b, w_up_b, ffn_conv_w, row(ffn_conv_b), w_down_b, l, S,
                          tm=t["tm"], tf=512)
    return final_norm(xr, final_norm_g[None, :], tm=t["tm"]).reshape(B, S, D)
```

```python
import functools
import math

import jax
import jax.numpy as jnp
from jax import lax
from jax.experimental import pallas as pl
from jax.experimental.pallas import tpu as pltpu

F32 = jnp.float32
BF16 = jnp.bfloat16

EPS = 1e-6
POOL_WINDOWS = (2, 4, 8, 16)
POOL_HALO = 16
DN_HEAD_DIM = 128
DN_CHUNK = 128
DN_HEADS_PER_STEP = 2
XA_HEADS = 4

V7X_VMEM_BYTES = 64 * 1024 * 1024
V7X_LANES = 128
V7X_SUBLANES = 8
NORM_ROWS = 128
XA_SUB_ROWS = 512
FFN_SUB_ROWS = 512
IN_PROJ_SUB_ROWS = 128


def _compiler_params(n_axes, vmem_bytes):
    limit = min(int(vmem_bytes * 1.25) + (4 << 20), V7X_VMEM_BYTES * 7 // 8)
    return pltpu.CompilerParams(dimension_semantics=("arbitrary",) * n_axes, vmem_limit_bytes=limit)


def _dot(a, b):
    return jnp.dot(a, b, preferred_element_type=F32)


def _dot_nt(a, b):
    return lax.dot_general(a, b, (((1,), (1,)), ((), ())), preferred_element_type=F32)


def _split_bf16(x):
    hi = x.astype(BF16)
    lo = (x - hi.astype(F32)).astype(BF16)
    return hi, lo


def _dot3(a, b):
    ah, al = _split_bf16(a)
    bh, bl = _split_bf16(b)
    return _dot(ah, bh) + _dot(ah, bl) + _dot(al, bh)


def _silu(x):
    return x * (1.0 / (1.0 + jnp.exp(-x)))


def _rms_rows(x, g):
    ms = jnp.mean(x * x, axis=-1, keepdims=True)
    return x * lax.rsqrt(ms + EPS) * g


def _norm_rows_into(h_ref, x_ref, g_ref):
    g = g_ref[...]

    def body(r, c):
        rows = pl.ds(pl.multiple_of(r * NORM_ROWS, NORM_ROWS), NORM_ROWS)
        h_ref[rows, :] = _rms_rows(x_ref[rows, :], g).astype(h_ref.dtype)
        return c

    lax.fori_loop(0, x_ref.shape[0] // NORM_ROWS, body, 0)


def _norm_matmul_body(x_ref, g_ref, w_ref, o_ref, h_ref):
    @pl.when((pl.program_id(1) == 0) & (pl.program_id(2) == 0))
    def _():
        _norm_rows_into(h_ref, x_ref, g_ref)

    o_ref[...] = _dot(h_ref[...], w_ref[...]).astype(o_ref.dtype)


def norm_matmul(x, g, w, out_dtype, tm, tn):
    R, K = x.shape
    L, _, N = w.shape
    osz = jnp.dtype(out_dtype).itemsize
    vmem = 2 * tm * K * 4 + tm * K * 2 + 2 * K * tn * 2 + 2 * tm * tn * osz + tm * tn * 4
    return pl.pallas_call(
        _norm_matmul_body,
        grid=(R // tm, L, N // tn),
        in_specs=[
            pl.BlockSpec((tm, K), lambda i, l, j: (i, 0)),
            pl.BlockSpec((1, K), lambda i, l, j: (0, 0)),
            pl.BlockSpec((None, K, tn), lambda i, l, j: (l, 0, j)),
        ],
        out_specs=pl.BlockSpec((None, tm, tn), lambda i, l, j: (l, i, j)),
        out_shape=jax.ShapeDtypeStruct((L, R, N), out_dtype),
        scratch_shapes=[pltpu.VMEM((tm, K), BF16)],
        compiler_params=_compiler_params(3, vmem),
        name="norm_matmul",
    )(x, g, w)


def _in_proj_body(x_ref, g_ref, w_ref, cw_ref, p_ref, h_ref, cbuf, carry, *, tiles_per_seq, conv0, tiles_per_part):
    i = pl.program_id(0)
    j = pl.program_id(1)
    tm, tn = p_ref.shape
    K = cw_ref.shape[0]
    halo = V7X_SUBLANES

    @pl.when(j == 0)
    def _():
        _norm_rows_into(h_ref, x_ref, g_ref)

    is_conv = (j >= conv0) & (j < conv0 + 3 * tiles_per_part)

    @pl.when(jnp.logical_not(is_conv))
    def _():
        p_ref[...] = _dot(h_ref[...], w_ref[...])

    @pl.when(is_conv)
    def _():
        cj = j - conv0
        part = cj // tiles_per_part
        q_scale = jnp.where(part == 0, DN_HEAD_DIM ** -0.5, 1.0)
        cbuf[0:halo, :] = jnp.where((i % tiles_per_seq) == 0, 0.0, carry[cj])
        sub = min(IN_PROJ_SUB_ROWS, tm)
        for r0 in range(0, tm, sub):
            acc = _dot(h_ref[r0:r0 + sub, :], w_ref[...])
            cbuf[halo + r0:halo + r0 + sub, :] = acc
            y = acc * cw_ref[K - 1:K, :]
            for t in range(K - 2, -1, -1):
                y = y + cbuf[halo - K + 1 + t + r0:halo - K + 1 + t + r0 + sub, :] * cw_ref[t:t + 1, :]
            y = _silu(y)
            for c in range(0, tn, DN_HEAD_DIM):
                yh = y[:, c:c + DN_HEAD_DIM]
                inv = lax.rsqrt(jnp.sum(yh * yh, axis=-1, keepdims=True) + EPS) * q_scale
                p_ref[r0:r0 + sub, c:c + DN_HEAD_DIM] = yh * jnp.where(part == 2, 1.0, inv)
        carry[cj] = cbuf[tm:tm + halo, :]


def in_proj(x, g, w, conv_w, l, S, tm, tn, conv_col0):
    R, D = x.shape
    N = w.shape[2]
    K, conv_cols = conv_w.shape[1:]
    assert conv_col0 % tn == 0 and (conv_cols // 3) % tn == 0 and tn % DN_HEAD_DIM == 0
    conv0, n_conv = conv_col0 // tn, conv_cols // tn
    vmem = (2 * tm * D * 4 + tm * D * 2 + 2 * D * tn * 2 + 3 * tm * tn * 4 + (tm + V7X_SUBLANES) * tn * 4
            + n_conv * V7X_SUBLANES * tn * 4 + 3 * tm * tn * 4)
    return pl.pallas_call(
        functools.partial(_in_proj_body, tiles_per_seq=S // tm, conv0=conv0, tiles_per_part=n_conv // 3),
        grid=(R // tm, N // tn),
        in_specs=[
            pl.BlockSpec((tm, D), lambda i, j: (i, 0)),
            pl.BlockSpec((1, D), lambda i, j: (0, 0)),
            pl.BlockSpec((None, D, tn), lambda i, j: (l, 0, j)),
            pl.BlockSpec((None, K, tn), lambda i, j: (l, 0, jnp.clip(j - conv0, 0, n_conv - 1))),
        ],
        out_specs=pl.BlockSpec((tm, tn), lambda i, j: (i, j)),
        out_shape=jax.ShapeDtypeStruct((R, N), F32),
        scratch_shapes=[
            pltpu.VMEM((tm, D), BF16),
            pltpu.VMEM((tm + V7X_SUBLANES, tn), F32),
            pltpu.VMEM((n_conv, V7X_SUBLANES, tn), F32),
        ],
        compiler_params=_compiler_params(2, vmem),
        name="in_proj",
    )(x, g, w, conv_w)


def _pool_body(u_ref, wp_ref, sc_ref, o_ref, buf):
    s = pl.program_id(1)
    ts = u_ref.shape[0]
    G = wp_ref.shape[1]

    @pl.when(s == 0)
    def _():
        buf[0:POOL_HALO, :] = jnp.zeros((POOL_HALO, buf.shape[1]), F32)

    buf[POOL_HALO:POOL_HALO + ts, :] = u_ref[...]
    pos = (s * ts + 1 + lax.broadcasted_iota(jnp.int32, (ts, 1), 0)).astype(F32)
    for i, w in enumerate(POOL_WINDOWS):
        cols = slice(i * G, (i + 1) * G)
        u = buf[POOL_HALO:POOL_HALO + ts, cols]
        acc = u
        for k in range(1, w):
            acc = acc + buf[POOL_HALO - k:POOL_HALO - k + ts, cols]
        mixed = acc / jnp.minimum(pos, float(w)) - u
        y = _dot(mixed.astype(BF16), wp_ref[i]) * sc_ref[:, cols]
        o_ref[:, cols] = y.astype(o_ref.dtype)
    buf[0:POOL_HALO, :] = buf[ts:ts + POOL_HALO, :]


def pool_mixer(p, w_pool, pool_scale, l, B, S, ts):
    n_win, G, _ = w_pool.shape[1:]
    W = n_win * G
    nt = S // ts
    vmem = 2 * ts * W * 4 + (ts + POOL_HALO) * W * 4 + 2 * n_win * G * G * 2 + 2 * ts * W * 2 + 4 * ts * G * 4
    return pl.pallas_call(
        _pool_body,
        grid=(B, nt),
        in_specs=[
            pl.BlockSpec((ts, W), lambda b, s: (b * nt + s, 0)),
            pl.BlockSpec((None, n_win, G, G), lambda b, s: (l, 0, 0, 0)),
            pl.BlockSpec((None, 1, W), lambda b, s: (l, 0, 0)),
        ],
        out_specs=pl.BlockSpec((ts, W), lambda b, s: (b * nt + s, 0)),
        out_shape=jax.ShapeDtypeStruct((B * S, W), BF16),
        scratch_shapes=[pltpu.VMEM((ts + POOL_HALO, W), F32)],
        compiler_params=_compiler_params(2, vmem),
        name="pool_mixer",
    )(p, w_pool, pool_scale)


def _unit_lower_inverses(ls, eye):
    C = ls[0].shape[0]
    n_iter = int(math.log2(C)) - 1
    xs = [eye - l_mat for l_mat in ls]
    lbs = [l_mat.astype(BF16) for l_mat in ls]
    for it in range(n_iter):
        if it < n_iter - 1:
            lxs = [_dot(lb, x.astype(BF16)) for lb, x in zip(lbs, xs)]
        else:
            lxs = [_dot3(l_mat, x) for l_mat, x in zip(ls, xs)]
        es = [eye - x - lx for x, lx in zip(xs, lxs)]
        xs = [x + _dot(x.astype(BF16), e.astype(BF16)) for x, e in zip(xs, es)]
    return xs


def _dn_body(q_ref, k_ref, v_ref, z_ref, ba_ref, alog_ref, dtb_ref, ng_ref, o_ref,
             state, beta_rep, gc_rep, grow_s, cdec_s, u_s, w_s, qd_s, at_s, kdt_s):
    s = pl.program_id(1)
    ts, W = q_ref.shape
    H = W // DN_HEAD_DIM
    C = DN_CHUNK
    n_chunk = ts // C

    @pl.when(s == 0)
    def _():
        state[...] = jnp.zeros(state.shape, F32)

    ba = ba_ref[...]
    beta = 1.0 / (1.0 + jnp.exp(-ba))
    xs = ba + dtb_ref[...]
    softplus = jnp.maximum(xs, 0.0) + jnp.log1p(jnp.exp(-jnp.abs(xs)))
    g = -jnp.exp(alog_ref[...]) * softplus
    row_in_chunk = lax.broadcasted_iota(jnp.int32, (ts, V7X_LANES), 0) % C
    gc = g
    sh = 1
    while sh < C:
        gc = gc + jnp.where(row_in_chunk >= sh, pltpu.roll(gc, sh, axis=0), 0.0)
        sh *= 2
    gct = gc.T
    lane = lax.broadcasted_iota(jnp.int32, (ts, V7X_LANES), 1)
    for h in range(H):
        beta_rep[h] = jnp.broadcast_to(
            jnp.sum(jnp.where(lane == h, beta, 0.0), axis=-1, keepdims=True), (ts, V7X_LANES))
        gc_rep[h] = jnp.broadcast_to(
            jnp.sum(jnp.where(lane == H + h, gc, 0.0), axis=-1, keepdims=True), (ts, V7X_LANES))
        grow_s[h] = gct[H + h:H + h + 1, :]

    ii = lax.broadcasted_iota(jnp.int32, (C, C), 0)
    jj = lax.broadcasted_iota(jnp.int32, (C, C), 1)
    eye = (ii == jj).astype(F32)

    def head_group_body(hg, carry):
        pairs = []
        for hh in range(DN_HEADS_PER_STEP):
            h = hg * DN_HEADS_PER_STEP + hh
            cols = pl.ds(pl.multiple_of(h * DN_HEAD_DIM, DN_HEAD_DIM), DN_HEAD_DIM)
            bcol, gcol, grow_all = beta_rep[h], gc_rep[h], grow_s[h]
            egc = jnp.exp(gcol)
            for c in range(n_chunk):
                r = slice(c * C, (c + 1) * C)
                pairs.append(dict(h=h, c=c, r=r, cols=cols, q=q_ref[r, cols], k=k_ref[r, cols], v=v_ref[r, cols],
                                  b=bcol[r], g=gcol[r], e=egc[r], grow=grow_all[:, r]))
        kbs = [p["k"].astype(BF16) for p in pairs]
        decs = [jnp.exp(jnp.minimum(p["g"] - p["grow"], 0.0)) for p in pairs]
        kks = [_dot_nt(kb, kb) for kb in kbs]
        qks = [_dot_nt(p["q"].astype(BF16), kb) for p, kb in zip(pairs, kbs)]
        ls = [jnp.where(ii > jj, kk * dec, 0.0) * p["b"] for kk, dec, p in zip(kks, decs, pairs)]
        for p, qk, dec in zip(pairs, qks, decs):
            r, cols = p["r"], p["cols"]
            at_s[r, cols] = jnp.where(ii >= jj, qk * dec, 0.0).astype(BF16)
            qd_s[r, cols] = (p["q"] * p["e"]).astype(BF16)
            kd = p["k"] * jnp.exp(p["g"][C - 1:C, :] - p["g"])
            kdt_s[r, cols] = kd.T.astype(BF16)
            cdec_s[p["c"], p["h"]] = p["e"][C - 1:C, :]
        t_invs = _unit_lower_inverses(ls, eye)
        rhs = [jnp.concatenate([p["v"] * p["b"], p["k"] * (p["b"] * p["e"])], axis=1) for p in pairs]
        sols = [_dot3(t_inv, b) for t_inv, b in zip(t_invs, rhs)]
        for p, sol in zip(pairs, sols):
            u_s[p["r"], p["cols"]] = sol[:, :DN_HEAD_DIM]
            w_s[p["r"], p["cols"]] = sol[:, DN_HEAD_DIM:].astype(BF16)
        return carry

    lax.fori_loop(0, H // DN_HEADS_PER_STEP, head_group_body, 0)

    def chunk_body(c, carry):
        r0 = pl.multiple_of(c * C, C)
        rows = pl.ds(r0, C)
        hs = range(H)
        cols = [slice(h * DN_HEAD_DIM, (h + 1) * DN_HEAD_DIM) for h in hs]
        sts = [state[h] for h in hs]
        sbs = [st.astype(BF16) for st in sts]
        wss = [_dot(w_s[rows, cols[h]], sbs[h]) for h in hs]
        qss = [_dot(qd_s[rows, cols[h]], sbs[h]) for h in hs]
        vbs = [(u_s[rows, cols[h]] - wss[h]).astype(BF16) for h in hs]
        avs = [_dot(at_s[rows, cols[h]], vbs[h]) for h in hs]
        kvs = [_dot(kdt_s[rows, cols[h]], vbs[h]) for h in hs]
        for h in hs:
            state[h] = sts[h] * cdec_s[c, h] + kvs[h]
            gated = _rms_rows(qss[h] + avs[h], ng_ref[...]) * _silu(z_ref[rows, cols[h]])
            o_ref[rows, cols[h]] = gated.astype(o_ref.dtype)
        return carry

    lax.fori_loop(0, n_chunk, chunk_body, 0)


def gated_delta_net(p, a_log, dt_bias, norm_g, l, B, S, ts, col_block0, H):
    W = H * DN_HEAD_DIM
    nt = S // ts
    NB = V7X_LANES
    gate_block = (col_block0 + 4) * (W // NB)
    C = DN_CHUNK
    vmem = (2 * 4 * ts * W * 4 + 2 * ts * NB * 4 + 2 * ts * W * 2
            + H * DN_HEAD_DIM * DN_HEAD_DIM * 4 + 2 * H * ts * V7X_LANES * 4 + H * V7X_SUBLANES * ts * 4
            + ts * W * 4 + 4 * ts * W * 2 + 48 * C * C * 4 * DN_HEADS_PER_STEP)

    def pblock(k):
        return pl.BlockSpec((ts, W), lambda b, s: (b * nt + s, col_block0 + k))

    return pl.pallas_call(
        _dn_body,
        grid=(B, nt),
        in_specs=[
            pblock(0), pblock(1), pblock(2), pblock(3),
            pl.BlockSpec((ts, NB), lambda b, s: (b * nt + s, gate_block)),
            pl.BlockSpec((None, 1, NB), lambda b, s: (l, 0, 0)),
            pl.BlockSpec((None, 1, NB), lambda b, s: (l, 0, 0)),
            pl.BlockSpec((None, 1, DN_HEAD_DIM), lambda b, s: (l, 0, 0)),
        ],
        out_specs=pl.BlockSpec((ts, W), lambda b, s: (b * nt + s, 0)),
        out_shape=jax.ShapeDtypeStruct((B * S, W), BF16),
        scratch_shapes=[
            pltpu.VMEM((H, DN_HEAD_DIM, DN_HEAD_DIM), F32),
            pltpu.VMEM((H, ts, V7X_LANES), F32),
            pltpu.VMEM((H, ts, V7X_LANES), F32),
            pltpu.VMEM((H, 1, ts), F32),
            pltpu.VMEM((ts // C, H, 1, V7X_LANES), F32),
            pltpu.VMEM((ts, W), F32),
            pltpu.VMEM((ts, W), BF16),
            pltpu.VMEM((ts, W), BF16),
            pltpu.VMEM((ts, W), BF16),
            pltpu.VMEM((ts, W), BF16),
        ],
        compiler_params=_compiler_params(2, vmem),
        name="gated_delta_net",
    )(p, p, p, p, p, a_log, dt_bias, norm_g)


def _mix_out_body(x_ref, yp_ref, yd_ref, w1_ref, w2_ref, o_ref):
    o_ref[...] = x_ref[...] + _dot(yp_ref[...], w1_ref[...]) + _dot(yd_ref[...], w2_ref[...])


def mix_out(x, y_pool, y_dn, w, l, tm, tn):
    R, D = x.shape
    K1, K2 = y_pool.shape[1], y_dn.shape[1]
    assert K1 == K2
    vmem = 2 * (2 * tm * tn * 4 + tm * (K1 + K2) * 2 + (K1 + K2) * tn * 2) + 2 * tm * tn * 4
    return pl.pallas_call(
        _mix_out_body,
        grid=(R // tm, D // tn),
        in_specs=[
            pl.BlockSpec((tm, tn), lambda i, j: (i, j)),
            pl.BlockSpec((tm, K1), lambda i, j: (i, 0)),
            pl.BlockSpec((tm, K2), lambda i, j: (i, 0)),
            pl.BlockSpec((None, K1, tn), lambda i, j: (l, 0, j)),
            pl.BlockSpec((None, K2, tn), lambda i, j: (l, 1, j)),
        ],
        out_specs=pl.BlockSpec((tm, tn), lambda i, j: (i, j)),
        out_shape=jax.ShapeDtypeStruct((R, D), F32),
        compiler_params=_compiler_params(2, vmem),
        name="mix_out",
    )(x, y_pool, y_dn, w, w)


def _xattn_body(x_ref, g_ref, wq_ref, k_ref, v_ref, wo_ref, o_ref, h_ref, a_ref):
    j = pl.program_id(1)
    dh = wq_ref.shape[1]
    tn = wo_ref.shape[1]

    @pl.when(j == 0)
    def _():
        _norm_rows_into(h_ref, x_ref, g_ref)

    @pl.when(j < XA_HEADS)
    def _():
        hcols = pl.ds(pl.multiple_of(j * dh, dh), dh)
        tm = h_ref.shape[0]
        sub = min(XA_SUB_ROWS, tm)

        def scores(r0):
            q = _dot(h_ref[r0:r0 + sub, :], wq_ref[...])
            return _dot_nt(q.astype(BF16), k_ref[...]) * (dh ** -0.5)

        sc = scores(0)
        for r0 in range(0, tm, sub):
            sc_next = scores(r0 + sub) if r0 + sub < tm else None
            e = jnp.exp(sc - jnp.max(sc, axis=-1, keepdims=True))
            pr = e / jnp.sum(e, axis=-1, keepdims=True)
            a_ref[r0:r0 + sub, hcols] = _dot(pr.astype(BF16), v_ref[...]).astype(BF16)
            sc = sc_next

    @pl.when(j >= XA_HEADS)
    def _():
        cols = pl.ds(pl.multiple_of((j - XA_HEADS) * tn, tn), tn)
        o_ref[...] = x_ref[:, cols] + _dot(a_ref[...], wo_ref[...])


def cross_attention(x, g, w_xq, kv, w_xo, l, B, S, M, tm, tn):
    R, D = x.shape
    dh = D // XA_HEADS
    tiles_per_batch = S // tm
    last = XA_HEADS - 1
    head = lambda j: jnp.minimum(j, last)
    otile = lambda j: jnp.maximum(j - XA_HEADS, 0)
    vmem = (2 * tm * D * 4 + 2 * tm * D * 2 + 4 * D * dh * 2 + 4 * M * dh * 2 + 4 * D * tn * 2 + 3 * tm * tn * 4
            + 3 * tm * dh * 4 + 3 * tm * M * 4)
    return pl.pallas_call(
        _xattn_body,
        grid=(R // tm, XA_HEADS + D // tn),
        in_specs=[
            pl.BlockSpec((tm, D), lambda i, j: (i, 0)),
            pl.BlockSpec((None, 1, D), lambda i, j: (l, 0, 0)),
            pl.BlockSpec((None, D, dh), lambda i, j: (l, 0, head(j))),
            pl.BlockSpec((None, M, dh), lambda i, j: (l, i // tiles_per_batch, head(j))),
            pl.BlockSpec((None, M, dh), lambda i, j: (l, i // tiles_per_batch, XA_HEADS + head(j))),
            pl.BlockSpec((None, D, tn), lambda i, j: (l, 0, otile(j))),
        ],
        out_specs=pl.BlockSpec((tm, tn), lambda i, j: (i, otile(j))),
        out_shape=jax.ShapeDtypeStruct((R, D), F32),
        scratch_shapes=[pltpu.VMEM((tm, D), BF16), pltpu.VMEM((tm, D), BF16)],
        compiler_params=_compiler_params(2, vmem),
        name="cross_attention",
    )(x, g, w_xq, kv, kv, w_xo)


def _ffn_body(x_ref, g_ref, wg_ref, wu_ref, cw_ref, cb_ref, wd_ref, o_ref, h_ref, gbuf, carry, *, tiles_per_seq):
    i = pl.program_id(0)
    j = pl.program_id(1)
    tm = x_ref.shape[0]
    K = cw_ref.shape[0]
    halo = V7X_SUBLANES

    @pl.when(j == 0)
    def _():
        _norm_rows_into(h_ref, x_ref, g_ref)
        o_ref[...] = x_ref[...]

    gbuf[0:halo, :] = jnp.where((i % tiles_per_seq) == 0, 0.0, carry[j])
    sub = min(FFN_SUB_ROWS, tm)

    def gate_up(r0):
        h = h_ref[r0:r0 + sub, :]
        return _dot(h, wg_ref[...]), _dot(h, wu_ref[...])

    gate, up = gate_up(0)
    for r0 in range(0, tm, sub):
        nxt = gate_up(r0 + sub) if r0 + sub < tm else None
        gbuf[halo + r0:halo + r0 + sub, :] = gate
        conv = gate * cw_ref[K - 1:K, :] + cb_ref[...]
        for t in range(K - 2, -1, -1):
            conv = conv + gbuf[halo - K + 1 + t + r0:halo - K + 1 + t + r0 + sub, :] * cw_ref[t:t + 1, :]
        act = _silu(conv) * up
        o_ref[r0:r0 + sub, :] += _dot(act.astype(BF16), wd_ref[...])
        if nxt is not None:
            gate, up = nxt
    carry[j] = gbuf[tm:tm + halo, :]


def conv_glu_ffn(x, g, w_gate, w_up, conv_w, conv_b, w_down, l, S, tm, tf):
    R, D = x.shape
    F = w_gate.shape[2]
    K = conv_w.shape[1]
    nf = F // tf
    vmem = (3 * tm * D * 4 + tm * D * 2 + 2 * 3 * D * tf * 2 + (tm + V7X_SUBLANES) * tf * 4
            + nf * V7X_SUBLANES * tf * 4 + 4 * tm * tf * 4)
    return pl.pallas_call(
        functools.partial(_ffn_body, tiles_per_seq=S // tm),
        grid=(R // tm, nf),
        in_specs=[
            pl.BlockSpec((tm, D), lambda i, j: (i, 0), pipeline_mode=pl.Buffered(1)),
            pl.BlockSpec((None, 1, D), lambda i, j: (l, 0, 0)),
            pl.BlockSpec((None, D, tf), lambda i, j: (l, 0, j)),
            pl.BlockSpec((None, D, tf), lambda i, j: (l, 0, j)),
            pl.BlockSpec((None, K, tf), lambda i, j: (l, 0, j)),
            pl.BlockSpec((None, 1, tf), lambda i, j: (l, 0, j)),
            pl.BlockSpec((None, tf, D), lambda i, j: (l, j, 0)),
        ],
        out_specs=pl.BlockSpec((tm, D), lambda i, j: (i, 0)),
        out_shape=jax.ShapeDtypeStruct((R, D), F32),
        scratch_shapes=[
            pltpu.VMEM((tm, D), BF16),
            pltpu.VMEM((tm + V7X_SUBLANES, tf), F32),
            pltpu.VMEM((nf, V7X_SUBLANES, tf), F32),
        ],
        compiler_params=_compiler_params(2, vmem),
        name="conv_glu_ffn",
    )(x, g, w_gate, w_up, conv_w, conv_b, w_down)


def _final_norm_body(x_ref, g_ref, o_ref):
    _norm_rows_into(o_ref, x_ref, g_ref)


def final_norm(x, g, tm):
    R, D = x.shape
    return pl.pallas_call(
        _final_norm_body,
        grid=(R // tm,),
        in_specs=[pl.BlockSpec((tm, D), lambda i: (i, 0)), pl.BlockSpec((1, D), lambda i: (0, 0))],
        out_specs=pl.BlockSpec((tm, D), lambda i: (i, 0)),
        out_shape=jax.ShapeDtypeStruct((R, D), F32),
        compiler_params=_compiler_params(1, 4 * tm * D * 4),
        name="final_norm",
    )(x, g)


def _tiles(S):
    return dict(tm=min(1024, S), ts_pool=min(512, S), ts_dn=min(512, S))


def kernel(x, mem, mix_norm_g, w_in, w_pool, pool_scale, dn_conv_w, dn_a_log, dn_dt_bias, dn_norm_g, w_mix_out,
           xa_norm_g, mem_norm_g, w_xq, w_xkv, w_xo, ffn_norm_g, w_gate, w_up, ffn_conv_w, ffn_conv_b, w_down,
           final_norm_g):
    B, S, D = x.shape
    M = mem.shape[1]
    depth = w_in.shape[0]
    H = dn_a_log.shape[1]
    dn_w = H * DN_HEAD_DIM
    pool_w = w_pool.shape[1] * w_pool.shape[2]
    main = pool_w + 4 * dn_w
    assert pool_w == dn_w and w_in.shape[2] == main + 2 * H and 2 * H <= V7X_LANES
    t = _tiles(S)

    tn_in = 512
    n_in = -(-w_in.shape[2] // tn_in) * tn_in
    w_in_b = jnp.pad(w_in, ((0, 0), (0, 0), (0, n_in - w_in.shape[2]))).astype(BF16)
    pad_gate = lambda a: jnp.pad(a, ((0, 0), (H, V7X_LANES - 2 * H)))[:, None, :]
    a_log_p, dt_bias_p = pad_gate(dn_a_log), pad_gate(dn_dt_bias)
    w_pool_b = w_pool.astype(BF16)
    w_mix_b = w_mix_out.astype(BF16)
    w_xq_b, w_xkv_b, w_xo_b = w_xq.astype(BF16), w_xkv.astype(BF16), w_xo.astype(BF16)
    w_gate_b, w_up_b, w_down_b = w_gate.astype(BF16), w_up.astype(BF16), w_down.astype(BF16)
    row = lambda a: a[:, None, :]

    xr = x.reshape(B * S, D)
    kv = norm_matmul(mem.reshape(B * M, D), mem_norm_g[None, :], w_xkv_b, BF16, tm=min(1024, B * M), tn=512)
    for l in range(depth):
        p = in_proj(xr, mix_norm_g[l][None, :], w_in_b, dn_conv_w, l, S, tm=t["tm"], tn=tn_in, conv_col0=pool_w)
        y_pool = pool_mixer(p, w_pool_b, row(pool_scale), l, B, S, t["ts_pool"])
        y_dn = gated_delta_net(p, a_log_p, dt_bias_p, row(dn_norm_g), l, B, S, t["ts_dn"],
                               col_block0=pool_w // dn_w, H=H)
        xr = mix_out(xr, y_pool, y_dn, w_mix_b, l, tm=t["tm"], tn=1024)
        xr = cross_attention(xr, row(xa_norm_g), w_xq_b, kv, w_xo_b, l, B, S, M, tm=t["tm"], tn=512)
        xr = conv_glu_ffn(xr, row(ffn_norm_g), w_gate_b, w_up_b, ffn_conv_w, row(ffn_conv_b), w_down_b, l, S,
                          tm=t["tm"], tf=512)
    return final_norm(xr, final_norm_g[None, :], tm=t["tm"]).reshape(B, S, D)
```

```python
import functools
import math

import jax
import jax.numpy as jnp
from jax import lax
from jax.experimental import pallas as pl
from jax.experimental.pallas import tpu as pltpu

F32 = jnp.float32
BF16 = jnp.bfloat16

EPS = 1e-6
POOL_WINDOWS = (2, 4, 8, 16)
POOL_HALO = 16
DN_HEAD_DIM = 128
DN_CHUNK = 128
DN_HEADS_PER_STEP = 2
XA_HEADS = 4

V7X_VMEM_BYTES = 64 * 1024 * 1024
V7X_LANES = 128
V7X_SUBLANES = 8
NORM_ROWS = 128
XA_SUB_ROWS = 512
FFN_SUB_ROWS = 512
IN_PROJ_SUB_ROWS = 128


def _compiler_params(n_axes, vmem_bytes):
    limit = min(int(vmem_bytes * 1.25) + (4 << 20), V7X_VMEM_BYTES * 7 // 8)
    return pltpu.CompilerParams(dimension_semantics=("arbitrary",) * n_axes, vmem_limit_bytes=limit)


def _dot(a, b):
    return jnp.dot(a, b, preferred_element_type=F32)


def _dot_nt(a, b):
    return lax.dot_general(a, b, (((1,), (1,)), ((), ())), preferred_element_type=F32)


def _silu(x):
    return x * (1.0 / (1.0 + jnp.exp(-x)))


def _rms_rows(x, g):
    ms = jnp.mean(x * x, axis=-1, keepdims=True)
    return x * lax.rsqrt(ms + EPS) * g


def _norm_rows_into(h_ref, x_ref, g_ref):
    g = g_ref[...]

    def body(r, c):
        rows = pl.ds(pl.multiple_of(r * NORM_ROWS, NORM_ROWS), NORM_ROWS)
        h_ref[rows, :] = _rms_rows(x_ref[rows, :], g).astype(h_ref.dtype)
        return c

    lax.fori_loop(0, x_ref.shape[0] // NORM_ROWS, body, 0)


def _norm_matmul_body(x_ref, g_ref, w_ref, o_ref, h_ref):
    @pl.when((pl.program_id(1) == 0) & (pl.program_id(2) == 0))
    def _():
        _norm_rows_into(h_ref, x_ref, g_ref)

    o_ref[...] = _dot(h_ref[...], w_ref[...]).astype(o_ref.dtype)


def norm_matmul(x, g, w, out_dtype, tm, tn):
    R, K = x.shape
    L, _, N = w.shape
    osz = jnp.dtype(out_dtype).itemsize
    vmem = 2 * tm * K * 4 + tm * K * 2 + 2 * K * tn * 2 + 2 * tm * tn * osz + tm * tn * 4
    return pl.pallas_call(
        _norm_matmul_body,
        grid=(R // tm, L, N // tn),
        in_specs=[
            pl.BlockSpec((tm, K), lambda i, l, j: (i, 0)),
            pl.BlockSpec((1, K), lambda i, l, j: (0, 0)),
            pl.BlockSpec((None, K, tn), lambda i, l, j: (l, 0, j)),
        ],
        out_specs=pl.BlockSpec((None, tm, tn), lambda i, l, j: (l, i, j)),
        out_shape=jax.ShapeDtypeStruct((L, R, N), out_dtype),
        scratch_shapes=[pltpu.VMEM((tm, K), BF16)],
        compiler_params=_compiler_params(3, vmem),
        name="norm_matmul",
    )(x, g, w)


def _in_proj_body(x_ref, g_ref, w_ref, cw_ref, p_ref, h_ref, cbuf, carry, *, tiles_per_seq, conv0, tiles_per_part):
    i = pl.program_id(0)
    j = pl.program_id(1)
    tm, tn = p_ref.shape
    K = cw_ref.shape[0]
    halo = V7X_SUBLANES

    @pl.when(j == 0)
    def _():
        _norm_rows_into(h_ref, x_ref, g_ref)

    is_conv = (j >= conv0) & (j < conv0 + 3 * tiles_per_part)

    @pl.when(jnp.logical_not(is_conv))
    def _():
        p_ref[...] = _dot(h_ref[...], w_ref[...])

    @pl.when(is_conv)
    def _():
        cj = j - conv0
        part = cj // tiles_per_part
        q_scale = jnp.where(part == 0, DN_HEAD_DIM ** -0.5, 1.0)
        cbuf[0:halo, :] = jnp.where((i % tiles_per_seq) == 0, 0.0, carry[cj])
        sub = min(IN_PROJ_SUB_ROWS, tm)
        for r0 in range(0, tm, sub):
            acc = _dot(h_ref[r0:r0 + sub, :], w_ref[...])
            cbuf[halo + r0:halo + r0 + sub, :] = acc
            y = acc * cw_ref[K - 1:K, :]
            for t in range(K - 2, -1, -1):
                y = y + cbuf[halo - K + 1 + t + r0:halo - K + 1 + t + r0 + sub, :] * cw_ref[t:t + 1, :]
            y = _silu(y)
            for c in range(0, tn, DN_HEAD_DIM):
                yh = y[:, c:c + DN_HEAD_DIM]
                inv = lax.rsqrt(jnp.sum(yh * yh, axis=-1, keepdims=True) + EPS) * q_scale
                p_ref[r0:r0 + sub, c:c + DN_HEAD_DIM] = yh * jnp.where(part == 2, 1.0, inv)
        carry[cj] = cbuf[tm:tm + halo, :]


def in_proj(x, g, w, conv_w, l, S, tm, tn, conv_col0):
    R, D = x.shape
    N = w.shape[2]
    K, conv_cols = conv_w.shape[1:]
    assert conv_col0 % tn == 0 and (conv_cols // 3) % tn == 0 and tn % DN_HEAD_DIM == 0
    conv0, n_conv = conv_col0 // tn, conv_cols // tn
    vmem = (2 * tm * D * 4 + tm * D * 2 + 2 * D * tn * 2 + 3 * tm * tn * 4 + (tm + V7X_SUBLANES) * tn * 4
            + n_conv * V7X_SUBLANES * tn * 4 + 3 * tm * tn * 4)
    return pl.pallas_call(
        functools.partial(_in_proj_body, tiles_per_seq=S // tm, conv0=conv0, tiles_per_part=n_conv // 3),
        grid=(R // tm, N // tn),
        in_specs=[
            pl.BlockSpec((tm, D), lambda i, j: (i, 0)),
            pl.BlockSpec((1, D), lambda i, j: (0, 0)),
            pl.BlockSpec((None, D, tn), lambda i, j: (l, 0, j)),
            pl.BlockSpec((None, K, tn), lambda i, j: (l, 0, jnp.clip(j - conv0, 0, n_conv - 1))),
        ],
        out_specs=pl.BlockSpec((tm, tn), lambda i, j: (i, j)),
        out_shape=jax.ShapeDtypeStruct((R, N), F32),
        scratch_shapes=[
            pltpu.VMEM((tm, D), BF16),
            pltpu.VMEM((tm + V7X_SUBLANES, tn), F32),
            pltpu.VMEM((n_conv, V7X_SUBLANES, tn), F32),
        ],
        compiler_params=_compiler_params(2, vmem),
        name="in_proj",
    )(x, g, w, conv_w)


def _pool_body(u_ref, wp_ref, sc_ref, o_ref, buf):
    s = pl.program_id(1)
    ts = u_ref.shape[0]
    G = wp_ref.shape[1]

    @pl.when(s == 0)
    def _():
        buf[0:POOL_HALO, :] = jnp.zeros((POOL_HALO, buf.shape[1]), F32)

    buf[POOL_HALO:POOL_HALO + ts, :] = u_ref[...]
    pos = (s * ts + 1 + lax.broadcasted_iota(jnp.int32, (ts, 1), 0)).astype(F32)
    for i, w in enumerate(POOL_WINDOWS):
        cols = slice(i * G, (i + 1) * G)
        u = buf[POOL_HALO:POOL_HALO + ts, cols]
        acc = u
        for k in range(1, w):
            acc = acc + buf[POOL_HALO - k:POOL_HALO - k + ts, cols]
        mixed = acc / jnp.minimum(pos, float(w)) - u
        y = _dot(mixed.astype(BF16), wp_ref[i]) * sc_ref[:, cols]
        o_ref[:, cols] = y.astype(o_ref.dtype)
    buf[0:POOL_HALO, :] = buf[ts:ts + POOL_HALO, :]


def pool_mixer(p, w_pool, pool_scale, l, B, S, ts):
    n_win, G, _ = w_pool.shape[1:]
    W = n_win * G
    nt = S // ts
    vmem = 2 * ts * W * 4 + (ts + POOL_HALO) * W * 4 + 2 * n_win * G * G * 2 + 2 * ts * W * 2 + 4 * ts * G * 4
    return pl.pallas_call(
        _pool_body,
        grid=(B, nt),
        in_specs=[
            pl.BlockSpec((ts, W), lambda b, s: (b * nt + s, 0)),
            pl.BlockSpec((None, n_win, G, G), lambda b, s: (l, 0, 0, 0)),
            pl.BlockSpec((None, 1, W), lambda b, s: (l, 0, 0)),
        ],
        out_specs=pl.BlockSpec((ts, W), lambda b, s: (b * nt + s, 0)),
        out_shape=jax.ShapeDtypeStruct((B * S, W), BF16),
        scratch_shapes=[pltpu.VMEM((ts + POOL_HALO, W), F32)],
        compiler_params=_compiler_params(2, vmem),
        name="pool_mixer",
    )(p, w_pool, pool_scale)


def _unit_lower_inverses(ls, eye):
    C = ls[0].shape[0]
    n_iter = int(math.log2(C)) - 1
    xs = [eye - l_mat for l_mat in ls]
    lbs = [l_mat.astype(BF16) for l_mat in ls]
    for _ in range(n_iter):
        xbs = [x.astype(BF16) for x in xs]
        lxs = [_dot(lb, xb) for lb, xb in zip(lbs, xbs)]
        es = [eye - x - lx for x, lx in zip(xs, lxs)]
        xs = [x + _dot(xb, e.astype(BF16)) for x, xb, e in zip(xs, xbs, es)]
    return xs


def _dn_body(q_ref, k_ref, v_ref, z_ref, ba_ref, alog_ref, dtb_ref, ng_ref, o_ref,
             state, beta_rep, gc_rep, grow_s, cdec_s, u_s, w_s, qd_s, at_s, kdt_s):
    s = pl.program_id(1)
    ts, W = q_ref.shape
    H = W // DN_HEAD_DIM
    C = DN_CHUNK
    n_chunk = ts // C

    @pl.when(s == 0)
    def _():
        state[...] = jnp.zeros(state.shape, F32)

    ba = ba_ref[...]
    beta = 1.0 / (1.0 + jnp.exp(-ba))
    xs = ba + dtb_ref[...]
    softplus = jnp.maximum(xs, 0.0) + jnp.log1p(jnp.exp(-jnp.abs(xs)))
    g = -jnp.exp(alog_ref[...]) * softplus
    row_in_chunk = lax.broadcasted_iota(jnp.int32, (ts, V7X_LANES), 0) % C
    gc = g
    sh = 1
    while sh < C:
        gc = gc + jnp.where(row_in_chunk >= sh, pltpu.roll(gc, sh, axis=0), 0.0)
        sh *= 2
    gct = gc.T
    lane = lax.broadcasted_iota(jnp.int32, (ts, V7X_LANES), 1)
    for h in range(H):
        beta_rep[h] = jnp.broadcast_to(
            jnp.sum(jnp.where(lane == h, beta, 0.0), axis=-1, keepdims=True), (ts, V7X_LANES))
        gc_rep[h] = jnp.broadcast_to(
            jnp.sum(jnp.where(lane == H + h, gc, 0.0), axis=-1, keepdims=True), (ts, V7X_LANES))
        grow_s[h] = gct[H + h:H + h + 1, :]

    ii = lax.broadcasted_iota(jnp.int32, (C, C), 0)
    jj = lax.broadcasted_iota(jnp.int32, (C, C), 1)
    eye = (ii == jj).astype(F32)

    def head_group_body(hg, carry):
        pairs = []
        for hh in range(DN_HEADS_PER_STEP):
            h = hg * DN_HEADS_PER_STEP + hh
            cols = pl.ds(pl.multiple_of(h * DN_HEAD_DIM, DN_HEAD_DIM), DN_HEAD_DIM)
            bcol, gcol, grow_all = beta_rep[h], gc_rep[h], grow_s[h]
            egc = jnp.exp(gcol)
            for c in range(n_chunk):
                r = slice(c * C, (c + 1) * C)
                pairs.append(dict(h=h, c=c, r=r, cols=cols, q=q_ref[r, cols], k=k_ref[r, cols], v=v_ref[r, cols],
                                  b=bcol[r], g=gcol[r], e=egc[r], grow=grow_all[:, r]))
        kbs = [p["k"].astype(BF16) for p in pairs]
        decs = [jnp.exp(jnp.minimum(p["g"] - p["grow"], 0.0)) for p in pairs]
        kks = [_dot_nt(kb, kb) for kb in kbs]
        qks = [_dot_nt(p["q"].astype(BF16), kb) for p, kb in zip(pairs, kbs)]
        ls = [jnp.where(ii > jj, kk * dec, 0.0) * p["b"] for kk, dec, p in zip(kks, decs, pairs)]
        for p, qk, dec in zip(pairs, qks, decs):
            r, cols = p["r"], p["cols"]
            at_s[r, cols] = jnp.where(ii >= jj, qk * dec, 0.0).astype(BF16)
            qd_s[r, cols] = (p["q"] * p["e"]).astype(BF16)
            kd = p["k"] * jnp.exp(p["g"][C - 1:C, :] - p["g"])
            kdt_s[r, cols] = kd.T.astype(BF16)
            cdec_s[p["c"], p["h"]] = p["e"][C - 1:C, :]
        t_invs = _unit_lower_inverses(ls, eye)
        rhs = [jnp.concatenate([p["v"] * p["b"], p["k"] * (p["b"] * p["e"])], axis=1) for p in pairs]
        sols = [_dot(t_inv.astype(BF16), b.astype(BF16)) for t_inv, b in zip(t_invs, rhs)]
        for p, sol in zip(pairs, sols):
            u_s[p["r"], p["cols"]] = sol[:, :DN_HEAD_DIM]
            w_s[p["r"], p["cols"]] = sol[:, DN_HEAD_DIM:].astype(BF16)
        return carry

    lax.fori_loop(0, H // DN_HEADS_PER_STEP, head_group_body, 0)

    def chunk_body(c, carry):
        r0 = pl.multiple_of(c * C, C)
        rows = pl.ds(r0, C)
        hs = range(H)
        cols = [slice(h * DN_HEAD_DIM, (h + 1) * DN_HEAD_DIM) for h in hs]
        sts = [state[h] for h in hs]
        sbs = [st.astype(BF16) for st in sts]
        wss = [_dot(w_s[rows, cols[h]], sbs[h]) for h in hs]
        qss = [_dot(qd_s[rows, cols[h]], sbs[h]) for h in hs]
        vbs = [(u_s[rows, cols[h]] - wss[h]).astype(BF16) for h in hs]
        avs = [_dot(at_s[rows, cols[h]], vbs[h]) for h in hs]
        kvs = [_dot(kdt_s[rows, cols[h]], vbs[h]) for h in hs]
        for h in hs:
            state[h] = sts[h] * cdec_s[c, h] + kvs[h]
            gated = _rms_rows(qss[h] + avs[h], ng_ref[...]) * _silu(z_ref[rows, cols[h]])
            o_ref[rows, cols[h]] = gated.astype(o_ref.dtype)
        return carry

    lax.fori_loop(0, n_chunk, chunk_body, 0)


def gated_delta_net(p, a_log, dt_bias, norm_g, l, B, S, ts, col_block0, H):
    W = H * DN_HEAD_DIM
    nt = S // ts
    NB = V7X_LANES
    gate_block = (col_block0 + 4) * (W // NB)
    C = DN_CHUNK
    vmem = (2 * 4 * ts * W * 4 + 2 * ts * NB * 4 + 2 * ts * W * 2
            + H * DN_HEAD_DIM * DN_HEAD_DIM * 4 + 2 * H * ts * V7X_LANES * 4 + H * V7X_SUBLANES * ts * 4
            + ts * W * 4 + 4 * ts * W * 2 + 48 * C * C * 4 * DN_HEADS_PER_STEP)

    def pblock(k):
        return pl.BlockSpec((ts, W), lambda b, s: (b * nt + s, col_block0 + k))

    return pl.pallas_call(
        _dn_body,
        grid=(B, nt),
        in_specs=[
            pblock(0), pblock(1), pblock(2), pblock(3),
            pl.BlockSpec((ts, NB), lambda b, s: (b * nt + s, gate_block)),
            pl.BlockSpec((None, 1, NB), lambda b, s: (l, 0, 0)),
            pl.BlockSpec((None, 1, NB), lambda b, s: (l, 0, 0)),
            pl.BlockSpec((None, 1, DN_HEAD_DIM), lambda b, s: (l, 0, 0)),
        ],
        out_specs=pl.BlockSpec((ts, W), lambda b, s: (b * nt + s, 0)),
        out_shape=jax.ShapeDtypeStruct((B * S, W), BF16),
        scratch_shapes=[
            pltpu.VMEM((H, DN_HEAD_DIM, DN_HEAD_DIM), F32),
            pltpu.VMEM((H, ts, V7X_LANES), F32),
            pltpu.VMEM((H, ts, V7X_LANES), F32),
            pltpu.VMEM((H, 1, ts), F32),
            pltpu.VMEM((ts // C, H, 1, V7X_LANES), F32),
            pltpu.VMEM((ts, W), F32),
            pltpu.VMEM((ts, W), BF16),
            pltpu.VMEM((ts, W), BF16),
            pltpu.VMEM((ts, W), BF16),
            pltpu.VMEM((ts, W), BF16),
        ],
        compiler_params=_compiler_params(2, vmem),
        name="gated_delta_net",
    )(p, p, p, p, p, a_log, dt_bias, norm_g)


def _mix_out_body(x_ref, yp_ref, yd_ref, w1_ref, w2_ref, o_ref):
    o_ref[...] = x_ref[...] + _dot(yp_ref[...], w1_ref[...]) + _dot(yd_ref[...], w2_ref[...])


def mix_out(x, y_pool, y_dn, w, l, tm, tn):
    R, D = x.shape
    K1, K2 = y_pool.shape[1], y_dn.shape[1]
    assert K1 == K2
    vmem = 2 * (2 * tm * tn * 4 + tm * (K1 + K2) * 2 + (K1 + K2) * tn * 2) + 2 * tm * tn * 4
    return pl.pallas_call(
        _mix_out_body,
        grid=(R // tm, D // tn),
        in_specs=[
            pl.BlockSpec((tm, tn), lambda i, j: (i, j)),
            pl.BlockSpec((tm, K1), lambda i, j: (i, 0)),
            pl.BlockSpec((tm, K2), lambda i, j: (i, 0)),
            pl.BlockSpec((None, K1, tn), lambda i, j: (l, 0, j)),
            pl.BlockSpec((None, K2, tn), lambda i, j: (l, 1, j)),
        ],
        out_specs=pl.BlockSpec((tm, tn), lambda i, j: (i, j)),
        out_shape=jax.ShapeDtypeStruct((R, D), F32),
        compiler_params=_compiler_params(2, vmem),
        name="mix_out",
    )(x, y_pool, y_dn, w, w)


def _xattn_body(x_ref, g_ref, wq_ref, k_ref, v_ref, wo_ref, o_ref, h_ref, a_ref):
    j = pl.program_id(1)
    dh = wq_ref.shape[1]
    tn = wo_ref.shape[1]

    @pl.when(j == 0)
    def _():
        _norm_rows_into(h_ref, x_ref, g_ref)

    @pl.when(j < XA_HEADS)
    def _():
        hcols = pl.ds(pl.multiple_of(j * dh, dh), dh)
        tm = h_ref.shape[0]
        sub = min(XA_SUB_ROWS, tm)

        def scores(r0):
            q = _dot(h_ref[r0:r0 + sub, :], wq_ref[...])
            return _dot_nt(q.astype(BF16), k_ref[...]) * (dh ** -0.5)

        sc = scores(0)
        for r0 in range(0, tm, sub):
            sc_next = scores(r0 + sub) if r0 + sub < tm else None
            e = jnp.exp(sc - jnp.max(sc, axis=-1, keepdims=True))
            pr = e / jnp.sum(e, axis=-1, keepdims=True)
            a_ref[r0:r0 + sub, hcols] = _dot(pr.astype(BF16), v_ref[...]).astype(BF16)
            sc = sc_next

    @pl.when(j >= XA_HEADS)
    def _():
        cols = pl.ds(pl.multiple_of((j - XA_HEADS) * tn, tn), tn)
        o_ref[...] = x_ref[:, cols] + _dot(a_ref[...], wo_ref[...])


def cross_attention(x, g, w_xq, kv, w_xo, l, B, S, M, tm, tn):
    R, D = x.shape
    dh = D // XA_HEADS
    tiles_per_batch = S // tm
    last = XA_HEADS - 1
    head = lambda j: jnp.minimum(j, last)
    otile = lambda j: jnp.maximum(j - XA_HEADS, 0)
    vmem = (2 * tm * D * 4 + 2 * tm * D * 2 + 4 * D * dh * 2 + 4 * M * dh * 2 + 4 * D * tn * 2 + 3 * tm * tn * 4
            + 3 * tm * dh * 4 + 3 * tm * M * 4)
    return pl.pallas_call(
        _xattn_body,
        grid=(R // tm, XA_HEADS + D // tn),
        in_specs=[
            pl.BlockSpec((tm, D), lambda i, j: (i, 0)),
            pl.BlockSpec((None, 1, D), lambda i, j: (l, 0, 0)),
            pl.BlockSpec((None, D, dh), lambda i, j: (l, 0, head(j))),
            pl.BlockSpec((None, M, dh), lambda i, j: (l, i // tiles_per_batch, head(j))),
            pl.BlockSpec((None, M, dh), lambda i, j: (l, i // tiles_per_batch, XA_HEADS + head(j))),
            pl.BlockSpec((None, D, tn), lambda i, j: (l, 0, otile(j))),
        ],
        out_specs=pl.BlockSpec((tm, tn), lambda i, j: (i, otile(j))),
        out_shape=jax.ShapeDtypeStruct((R, D), F32),
        scratch_shapes=[pltpu.VMEM((tm, D), BF16), pltpu.VMEM((tm, D), BF16)],
        compiler_params=_compiler_params(2, vmem),
        name="cross_attention",
    )(x, g, w_xq, kv, kv, w_xo)


def _ffn_body(x_ref, g_ref, wg_ref, wu_ref, cw_ref, cb_ref, wd_ref, o_ref, h_ref, gbuf, carry, *, tiles_per_seq):
    i = pl.program_id(0)
    j = pl.program_id(1)
    tm = x_ref.shape[0]
    K = cw_ref.shape[0]
    halo = V7X_SUBLANES

    @pl.when(j == 0)
    def _():
        _norm_rows_into(h_ref, x_ref, g_ref)
        o_ref[...] = x_ref[...]

    gbuf[0:halo, :] = jnp.where((i % tiles_per_seq) == 0, 0.0, carry[j])
    sub = min(FFN_SUB_ROWS, tm)

    def gate_up(r0):
        h = h_ref[r0:r0 + sub, :]
        return _dot(h, wg_ref[...]), _dot(h, wu_ref[...])

    gate, up = gate_up(0)
    for r0 in range(0, tm, sub):
        nxt = gate_up(r0 + sub) if r0 + sub < tm else None
        gbuf[halo + r0:halo + r0 + sub, :] = gate
        conv = gate * cw_ref[K - 1:K, :] + cb_ref[...]
        for t in range(K - 2, -1, -1):
            conv = conv + gbuf[halo - K + 1 + t + r0:halo - K + 1 + t + r0 + sub, :] * cw_ref[t:t + 1, :]
        act = _silu(conv) * up
        o_ref[r0:r0 + sub, :] += _dot(act.astype(BF16), wd_ref[...])
        if nxt is not None:
            gate, up = nxt
    carry[j] = gbuf[tm:tm + halo, :]


def conv_glu_ffn(x, g, w_gate, w_up, conv_w, conv_b, w_down, l, S, tm, tf):
    R, D = x.shape
    F = w_gate.shape[2]
    K = conv_w.shape[1]
    nf = F // tf
    vmem = (3 * tm * D * 4 + tm * D * 2 + 2 * 3 * D * tf * 2 + (tm + V7X_SUBLANES) * tf * 4
            + nf * V7X_SUBLANES * tf * 4 + 4 * tm * tf * 4)
    return pl.pallas_call(
        functools.partial(_ffn_body, tiles_per_seq=S // tm),
        grid=(R // tm, nf),
        in_specs=[
            pl.BlockSpec((tm, D), lambda i, j: (i, 0), pipeline_mode=pl.Buffered(1)),
            pl.BlockSpec((None, 1, D), lambda i, j: (l, 0, 0)),
            pl.BlockSpec((None, D, tf), lambda i, j: (l, 0, j)),
            pl.BlockSpec((None, D, tf), lambda i, j: (l, 0, j)),
            pl.BlockSpec((None, K, tf), lambda i, j: (l, 0, j)),
            pl.BlockSpec((None, 1, tf), lambda i, j: (l, 0, j)),
            pl.BlockSpec((None, tf, D), lambda i, j: (l, j, 0)),
        ],
        out_specs=pl.BlockSpec((tm, D), lambda i, j: (i, 0)),
        out_shape=jax.ShapeDtypeStruct((R, D), F32),
        scratch_shapes=[
            pltpu.VMEM((tm, D), BF16),
            pltpu.VMEM((tm + V7X_SUBLANES, tf), F32),
            pltpu.VMEM((nf, V7X_SUBLANES, tf), F32),
        ],
        compiler_params=_compiler_params(2, vmem),
        name="conv_glu_ffn",
    )(x, g, w_gate, w_up, conv_w, conv_b, w_down)


def _final_norm_body(x_ref, g_ref, o_ref):
    _norm_rows_into(o_ref, x_ref, g_ref)


def final_norm(x, g, tm):
    R, D = x.shape
    return pl.pallas_call(
        _final_norm_body,
        grid=(R // tm,),
        in_specs=[pl.BlockSpec((tm, D), lambda i: (i, 0)), pl.BlockSpec((1, D), lambda i: (0, 0))],
        out_specs=pl.BlockSpec((tm, D), lambda i: (i, 0)),
        out_shape=jax.ShapeDtypeStruct((R, D), F32),
        compiler_params=_compiler_params(1, 4 * tm * D * 4),
        name="final_norm",
    )(x, g)


def _tiles(S):
    return dict(tm=min(1024, S), ts_pool=min(512, S), ts_dn=min(512, S))


def kernel(x, mem, mix_norm_g, w_in, w_pool, pool_scale, dn_conv_w, dn_a_log, dn_dt_bias, dn_norm_g, w_mix_out,
           xa_norm_g, mem_norm_g, w_xq, w_xkv, w_xo, ffn_norm_g, w_gate, w_up, ffn_conv_w, ffn_conv_b, w_down,
           final_norm_g):
    B, S, D = x.shape
    M = mem.shape[1]
    depth = w_in.shape[0]
    H = dn_a_log.shape[1]
    dn_w = H * DN_HEAD_DIM
    pool_w = w_pool.shape[1] * w_pool.shape[2]
    main = pool_w + 4 * dn_w
    assert pool_w == dn_w and w_in.shape[2] == main + 2 * H and 2 * H <= V7X_LANES
    t = _tiles(S)

    tn_in = 512
    n_in = -(-w_in.shape[2] // tn_in) * tn_in
    w_in_b = jnp.pad(w_in, ((0, 0), (0, 0), (0, n_in - w_in.shape[2]))).astype(BF16)
    pad_gate = lambda a: jnp.pad(a, ((0, 0), (H, V7X_LANES - 2 * H)))[:, None, :]
    a_log_p, dt_bias_p = pad_gate(dn_a_log), pad_gate(dn_dt_bias)
    w_pool_b = w_pool.astype(BF16)
    w_mix_b = w_mix_out.astype(BF16)
    w_xq_b, w_xkv_b, w_xo_b = w_xq.astype(BF16), w_xkv.astype(BF16), w_xo.astype(BF16)
    w_gate_b, w_up_b, w_down_b = w_gate.astype(BF16), w_up.astype(BF16), w_down.astype(BF16)
    row = lambda a: a[:, None, :]

    xr = x.reshape(B * S, D)
    kv = norm_matmul(mem.reshape(B * M, D), mem_norm_g[None, :], w_xkv_b, BF16, tm=min(1024, B * M), tn=512)
    for l in range(depth):
        p = in_proj(xr, mix_norm_g[l][None, :], w_in_b, dn_conv_w, l, S, tm=t["tm"], tn=tn_in, conv_col0=pool_w)
        y_pool = pool_mixer(p, w_pool_b, row(pool_scale), l, B, S, t["ts_pool"])
        y_dn = gated_delta_net(p, a_log_p, dt_bias_p, row(dn_norm_g), l, B, S, t["ts_dn"],
                               col_block0=pool_w // dn_w, H=H)
        xr = mix_out(xr, y_pool, y_dn, w_mix_b, l, tm=t["tm"], tn=1024)
        xr = cross_attention(xr, row(xa_norm_g), w_xq_b, kv, w_xo_b, l, B, S, M, tm=t["tm"], tn=512)
        xr = conv_glu_ffn(xr, row(ffn_norm_g), w_gate_b, w_up_b, ffn_conv_w, row(ffn_conv_b), w_down_b, l, S,
                          tm=t["tm"], tf=512)
    return final_norm(xr, final_norm_g[None, :], tm=t["tm"]).reshape(B, S, D)
```

```python
import functools
import math

import jax
import jax.numpy as jnp
from jax import lax
from jax.experimental import pallas as pl
from jax.experimental.pallas import tpu as pltpu

F32 = jnp.float32
BF16 = jnp.bfloat16

EPS = 1e-6
POOL_WINDOWS = (2, 4, 8, 16)
POOL_HALO = 16
DN_HEAD_DIM = 128
DN_CHUNK = 128
DN_HEADS_PER_STEP = 2
XA_HEADS = 4

V7X_VMEM_BYTES = 64 * 1024 * 1024
V7X_LANES = 128
V7X_SUBLANES = 8
NORM_ROWS = 128
XA_SUB_ROWS = 512
FFN_SUB_ROWS = 512
IN_PROJ_SUB_ROWS = 128


def _compiler_params(n_axes, vmem_bytes):
    limit = min(int(vmem_bytes * 1.25) + (4 << 20), V7X_VMEM_BYTES * 7 // 8)
    return pltpu.CompilerParams(dimension_semantics=("arbitrary",) * n_axes, vmem_limit_bytes=limit)


def _dot(a, b):
    return jnp.dot(a, b, preferred_element_type=F32)


def _dot_nt(a, b):
    return lax.dot_general(a, b, (((1,), (1,)), ((), ())), preferred_element_type=F32)


def _silu(x):
    return x * (1.0 / (1.0 + jnp.exp(-x)))


def _rms_rows(x, g):
    ms = jnp.mean(x * x, axis=-1, keepdims=True)
    return x * lax.rsqrt(ms + EPS) * g


def _rows_shifted_down(a, prev, s):
    n = prev.shape[0]
    rows = lax.broadcasted_iota(jnp.int32, prev.shape, 0)
    rolled = [pltpu.roll(piece, s, axis=0) for piece in [prev] + [a[r:r + n, :] for r in range(0, a.shape[0], n)]]
    return jnp.concatenate([jnp.where(rows < s, lo, hi) for lo, hi in zip(rolled[:-1], rolled[1:])], axis=0)


def _norm_rows_into(h_ref, x_ref, g_ref):
    g = g_ref[...]

    def body(r, c):
        rows = pl.ds(pl.multiple_of(r * NORM_ROWS, NORM_ROWS), NORM_ROWS)
        h_ref[rows, :] = _rms_rows(x_ref[rows, :], g).astype(h_ref.dtype)
        return c

    lax.fori_loop(0, x_ref.shape[0] // NORM_ROWS, body, 0)


def _norm_matmul_body(x_ref, g_ref, w_ref, o_ref, h_ref):
    @pl.when((pl.program_id(1) == 0) & (pl.program_id(2) == 0))
    def _():
        _norm_rows_into(h_ref, x_ref, g_ref)

    o_ref[...] = _dot(h_ref[...], w_ref[...]).astype(o_ref.dtype)


def norm_matmul(x, g, w, out_dtype, tm, tn):
    R, K = x.shape
    L, _, N = w.shape
    osz = jnp.dtype(out_dtype).itemsize
    vmem = 2 * tm * K * 4 + tm * K * 2 + 2 * K * tn * 2 + 2 * tm * tn * osz + tm * tn * 4
    return pl.pallas_call(
        _norm_matmul_body,
        grid=(R // tm, L, N // tn),
        in_specs=[
            pl.BlockSpec((tm, K), lambda i, l, j: (i, 0)),
            pl.BlockSpec((1, K), lambda i, l, j: (0, 0)),
            pl.BlockSpec((None, K, tn), lambda i, l, j: (l, 0, j)),
        ],
        out_specs=pl.BlockSpec((None, tm, tn), lambda i, l, j: (l, i, j)),
        out_shape=jax.ShapeDtypeStruct((L, R, N), out_dtype),
        scratch_shapes=[pltpu.VMEM((tm, K), BF16)],
        compiler_params=_compiler_params(3, vmem),
        name="norm_matmul",
    )(x, g, w)


def _in_proj_body(x_ref, g_ref, w_ref, cw_ref, p_ref, h_ref, carry, *, tiles_per_seq, conv0, tiles_per_part):
    i = pl.program_id(0)
    j = pl.program_id(1)
    tm, tn = p_ref.shape
    K = cw_ref.shape[0]
    halo = V7X_SUBLANES

    @pl.when(j == 0)
    def _():
        _norm_rows_into(h_ref, x_ref, g_ref)

    is_conv = (j >= conv0) & (j < conv0 + 3 * tiles_per_part)

    @pl.when(jnp.logical_not(is_conv))
    def _():
        p_ref[...] = _dot(h_ref[...], w_ref[...])

    @pl.when(is_conv)
    def _():
        cj = j - conv0
        part = cj // tiles_per_part
        q_scale = jnp.where(part == 0, DN_HEAD_DIM ** -0.5, 1.0)
        prev = jnp.where((i % tiles_per_seq) == 0, 0.0, carry[cj])
        sub = min(IN_PROJ_SUB_ROWS, tm)
        for r0 in range(0, tm, sub):
            acc = _dot(h_ref[r0:r0 + sub, :], w_ref[...])
            y = acc * cw_ref[K - 1:K, :]
            for s in range(1, K):
                y = y + _rows_shifted_down(acc, prev, s) * cw_ref[K - 1 - s:K - s, :]
            prev = acc[sub - halo:sub, :]
            y = _silu(y)
            for c in range(0, tn, DN_HEAD_DIM):
                yh = y[:, c:c + DN_HEAD_DIM]
                inv = lax.rsqrt(jnp.sum(yh * yh, axis=-1, keepdims=True) + EPS) * q_scale
                p_ref[r0:r0 + sub, c:c + DN_HEAD_DIM] = yh * jnp.where(part == 2, 1.0, inv)
        carry[cj] = prev


def in_proj(x, g, w, conv_w, l, S, tm, tn, conv_col0):
    R, D = x.shape
    N = w.shape[2]
    K, conv_cols = conv_w.shape[1:]
    assert conv_col0 % tn == 0 and (conv_cols // 3) % tn == 0 and tn % DN_HEAD_DIM == 0
    conv0, n_conv = conv_col0 // tn, conv_cols // tn
    vmem = (2 * tm * D * 4 + tm * D * 2 + 2 * D * tn * 2 + 3 * tm * tn * 4 + (tm + V7X_SUBLANES) * tn * 4
            + n_conv * V7X_SUBLANES * tn * 4 + 3 * tm * tn * 4)
    return pl.pallas_call(
        functools.partial(_in_proj_body, tiles_per_seq=S // tm, conv0=conv0, tiles_per_part=n_conv // 3),
        grid=(R // tm, N // tn),
        in_specs=[
            pl.BlockSpec((tm, D), lambda i, j: (i, 0)),
            pl.BlockSpec((1, D), lambda i, j: (0, 0)),
            pl.BlockSpec((None, D, tn), lambda i, j: (l, 0, j)),
            pl.BlockSpec((None, K, tn), lambda i, j: (l, 0, jnp.clip(j - conv0, 0, n_conv - 1))),
        ],
        out_specs=pl.BlockSpec((tm, tn), lambda i, j: (i, j)),
        out_shape=jax.ShapeDtypeStruct((R, N), F32),
        scratch_shapes=[
            pltpu.VMEM((tm, D), BF16),
            pltpu.VMEM((n_conv, V7X_SUBLANES, tn), F32),
        ],
        compiler_params=_compiler_params(2, vmem),
        name="in_proj",
    )(x, g, w, conv_w)


def _pool_body(u_ref, wp_ref, sc_ref, o_ref, buf):
    s = pl.program_id(1)
    ts = u_ref.shape[0]
    G = wp_ref.shape[1]

    @pl.when(s == 0)
    def _():
        buf[0:POOL_HALO, :] = jnp.zeros((POOL_HALO, buf.shape[1]), F32)

    buf[POOL_HALO:POOL_HALO + ts, :] = u_ref[...]
    pos = (s * ts + 1 + lax.broadcasted_iota(jnp.int32, (ts, 1), 0)).astype(F32)
    for i, w in enumerate(POOL_WINDOWS):
        cols = slice(i * G, (i + 1) * G)
        u = buf[POOL_HALO:POOL_HALO + ts, cols]
        acc = u
        for k in range(1, w):
            acc = acc + buf[POOL_HALO - k:POOL_HALO - k + ts, cols]
        mixed = acc / jnp.minimum(pos, float(w)) - u
        y = _dot(mixed.astype(BF16), wp_ref[i]) * sc_ref[:, cols]
        o_ref[:, cols] = y.astype(o_ref.dtype)
    buf[0:POOL_HALO, :] = buf[ts:ts + POOL_HALO, :]


def pool_mixer(p, w_pool, pool_scale, l, B, S, ts):
    n_win, G, _ = w_pool.shape[1:]
    W = n_win * G
    nt = S // ts
    vmem = 2 * ts * W * 4 + (ts + POOL_HALO) * W * 4 + 2 * n_win * G * G * 2 + 2 * ts * W * 2 + 4 * ts * G * 4
    return pl.pallas_call(
        _pool_body,
        grid=(B, nt),
        in_specs=[
            pl.BlockSpec((ts, W), lambda b, s: (b * nt + s, 0)),
            pl.BlockSpec((None, n_win, G, G), lambda b, s: (l, 0, 0, 0)),
            pl.BlockSpec((None, 1, W), lambda b, s: (l, 0, 0)),
        ],
        out_specs=pl.BlockSpec((ts, W), lambda b, s: (b * nt + s, 0)),
        out_shape=jax.ShapeDtypeStruct((B * S, W), BF16),
        scratch_shapes=[pltpu.VMEM((ts + POOL_HALO, W), F32)],
        compiler_params=_compiler_params(2, vmem),
        name="pool_mixer",
    )(p, w_pool, pool_scale)


def _unit_lower_inverses(ls, eye):
    C = ls[0].shape[0]
    n_iter = int(math.log2(C)) - 1
    xs = [eye - l_mat for l_mat in ls]
    lbs = [l_mat.astype(BF16) for l_mat in ls]
    for _ in range(n_iter):
        xbs = [x.astype(BF16) for x in xs]
        lxs = [_dot(lb, xb) for lb, xb in zip(lbs, xbs)]
        es = [eye - x - lx for x, lx in zip(xs, lxs)]
        xs = [x + _dot(xb, e.astype(BF16)) for x, xb, e in zip(xs, xbs, es)]
    return xs


def _dn_body(q_ref, k_ref, v_ref, z_ref, ba_ref, alog_ref, dtb_ref, ng_ref, o_ref,
             state, beta_rep, gc_rep, grow_s, cdec_s, u_s, w_s, qd_s, at_s, kdt_s):
    s = pl.program_id(1)
    ts, W = q_ref.shape
    H = W // DN_HEAD_DIM
    C = DN_CHUNK
    n_chunk = ts // C

    @pl.when(s == 0)
    def _():
        state[...] = jnp.zeros(state.shape, F32)

    ba = ba_ref[...]
    beta = 1.0 / (1.0 + jnp.exp(-ba))
    xs = ba + dtb_ref[...]
    softplus = jnp.maximum(xs, 0.0) + jnp.log1p(jnp.exp(-jnp.abs(xs)))
    g = -jnp.exp(alog_ref[...]) * softplus
    row_in_chunk = lax.broadcasted_iota(jnp.int32, (ts, V7X_LANES), 0) % C
    gc = g
    sh = 1
    while sh < C:
        gc = gc + jnp.where(row_in_chunk >= sh, pltpu.roll(gc, sh, axis=0), 0.0)
        sh *= 2
    gct = gc.T
    lane = lax.broadcasted_iota(jnp.int32, (ts, V7X_LANES), 1)
    for h in range(H):
        beta_rep[h] = jnp.broadcast_to(
            jnp.sum(jnp.where(lane == h, beta, 0.0), axis=-1, keepdims=True), (ts, V7X_LANES))
        gc_rep[h] = jnp.broadcast_to(
            jnp.sum(jnp.where(lane == H + h, gc, 0.0), axis=-1, keepdims=True), (ts, V7X_LANES))
        grow_s[h] = gct[H + h:H + h + 1, :]

    ii = lax.broadcasted_iota(jnp.int32, (C, C), 0)
    jj = lax.broadcasted_iota(jnp.int32, (C, C), 1)
    eye = (ii == jj).astype(F32)

    def head_group_body(hg, carry):
        pairs = []
        for hh in range(DN_HEADS_PER_STEP):
            h = hg * DN_HEADS_PER_STEP + hh
            cols = pl.ds(pl.multiple_of(h * DN_HEAD_DIM, DN_HEAD_DIM), DN_HEAD_DIM)
            bcol, gcol, grow_all = beta_rep[h], gc_rep[h], grow_s[h]
            egc = jnp.exp(gcol)
            for c in range(n_chunk):
                r = slice(c * C, (c + 1) * C)
                pairs.append(dict(h=h, c=c, r=r, cols=cols, q=q_ref[r, cols], k=k_ref[r, cols], v=v_ref[r, cols],
                                  b=bcol[r], g=gcol[r], e=egc[r], grow=grow_all[:, r]))
        kbs = [p["k"].astype(BF16) for p in pairs]
        decs = [jnp.exp(jnp.minimum(p["g"] - p["grow"], 0.0)) for p in pairs]
        kks = [_dot_nt(kb, kb) for kb in kbs]
        qks = [_dot_nt(p["q"].astype(BF16), kb) for p, kb in zip(pairs, kbs)]
        ls = [jnp.where(ii > jj, kk * dec, 0.0) * p["b"] for kk, dec, p in zip(kks, decs, pairs)]
        for p, qk, dec in zip(pairs, qks, decs):
            r, cols = p["r"], p["cols"]
            at_s[r, cols] = jnp.where(ii >= jj, qk * dec, 0.0).astype(BF16)
            qd_s[r, cols] = (p["q"] * p["e"]).astype(BF16)
            kd = p["k"] * jnp.exp(p["g"][C - 1:C, :] - p["g"])
            kdt_s[r, cols] = kd.T.astype(BF16)
            cdec_s[p["c"], p["h"]] = p["e"][C - 1:C, :]
        t_invs = _unit_lower_inverses(ls, eye)
        rhs = [jnp.concatenate([p["v"] * p["b"], p["k"] * (p["b"] * p["e"])], axis=1) for p in pairs]
        sols = [_dot(t_inv.astype(BF16), b.astype(BF16)) for t_inv, b in zip(t_invs, rhs)]
        for p, sol in zip(pairs, sols):
            u_s[p["r"], p["cols"]] = sol[:, :DN_HEAD_DIM]
            w_s[p["r"], p["cols"]] = sol[:, DN_HEAD_DIM:].astype(BF16)
        return carry

    lax.fori_loop(0, H // DN_HEADS_PER_STEP, head_group_body, 0)

    def chunk_body(c, carry):
        r0 = pl.multiple_of(c * C, C)
        rows = pl.ds(r0, C)
        hs = range(H)
        cols = [slice(h * DN_HEAD_DIM, (h + 1) * DN_HEAD_DIM) for h in hs]
        sts = [state[h] for h in hs]
        sbs = [st.astype(BF16) for st in sts]
        wss = [_dot(w_s[rows, cols[h]], sbs[h]) for h in hs]
        qss = [_dot(qd_s[rows, cols[h]], sbs[h]) for h in hs]
        vbs = [(u_s[rows, cols[h]] - wss[h]).astype(BF16) for h in hs]
        avs = [_dot(at_s[rows, cols[h]], vbs[h]) for h in hs]
        kvs = [_dot(kdt_s[rows, cols[h]], vbs[h]) for h in hs]
        for h in hs:
            state[h] = sts[h] * cdec_s[c, h] + kvs[h]
            gated = _rms_rows(qss[h] + avs[h], ng_ref[...]) * _silu(z_ref[rows, cols[h]])
            o_ref[rows, cols[h]] = gated.astype(o_ref.dtype)
        return carry

    lax.fori_loop(0, n_chunk, chunk_body, 0)


def gated_delta_net(p, a_log, dt_bias, norm_g, l, B, S, ts, col_block0, H):
    W = H * DN_HEAD_DIM
    nt = S // ts
    NB = V7X_LANES
    gate_block = (col_block0 + 4) * (W // NB)
    C = DN_CHUNK
    vmem = (2 * 4 * ts * W * 4 + 2 * ts * NB * 4 + 2 * ts * W * 2
            + H * DN_HEAD_DIM * DN_HEAD_DIM * 4 + 2 * H * ts * V7X_LANES * 4 + H * V7X_SUBLANES * ts * 4
            + ts * W * 4 + 4 * ts * W * 2 + 48 * C * C * 4 * DN_HEADS_PER_STEP)

    def pblock(k):
        return pl.BlockSpec((ts, W), lambda b, s: (b * nt + s, col_block0 + k))

    return pl.pallas_call(
        _dn_body,
        grid=(B, nt),
        in_specs=[
            pblock(0), pblock(1), pblock(2), pblock(3),
            pl.BlockSpec((ts, NB), lambda b, s: (b * nt + s, gate_block)),
            pl.BlockSpec((None, 1, NB), lambda b, s: (l, 0, 0)),
            pl.BlockSpec((None, 1, NB), lambda b, s: (l, 0, 0)),
            pl.BlockSpec((None, 1, DN_HEAD_DIM), lambda b, s: (l, 0, 0)),
        ],
        out_specs=pl.BlockSpec((ts, W), lambda b, s: (b * nt + s, 0)),
        out_shape=jax.ShapeDtypeStruct((B * S, W), BF16),
        scratch_shapes=[
            pltpu.VMEM((H, DN_HEAD_DIM, DN_HEAD_DIM), F32),
            pltpu.VMEM((H, ts, V7X_LANES), F32),
            pltpu.VMEM((H, ts, V7X_LANES), F32),
            pltpu.VMEM((H, 1, ts), F32),
            pltpu.VMEM((ts // C, H, 1, V7X_LANES), F32),
            pltpu.VMEM((ts, W), F32),
            pltpu.VMEM((ts, W), BF16),
            pltpu.VMEM((ts, W), BF16),
            pltpu.VMEM((ts, W), BF16),
            pltpu.VMEM((ts, W), BF16),
        ],
        compiler_params=_compiler_params(2, vmem),
        name="gated_delta_net",
    )(p, p, p, p, p, a_log, dt_bias, norm_g)


def _mix_out_body(x_ref, yp_ref, yd_ref, w1_ref, w2_ref, o_ref):
    o_ref[...] = x_ref[...] + _dot(yp_ref[...], w1_ref[...]) + _dot(yd_ref[...], w2_ref[...])


def mix_out(x, y_pool, y_dn, w, l, tm, tn):
    R, D = x.shape
    K1, K2 = y_pool.shape[1], y_dn.shape[1]
    assert K1 == K2
    vmem = 2 * (2 * tm * tn * 4 + tm * (K1 + K2) * 2 + (K1 + K2) * tn * 2) + 2 * tm * tn * 4
    return pl.pallas_call(
        _mix_out_body,
        grid=(R // tm, D // tn),
        in_specs=[
            pl.BlockSpec((tm, tn), lambda i, j: (i, j)),
            pl.BlockSpec((tm, K1), lambda i, j: (i, 0)),
            pl.BlockSpec((tm, K2), lambda i, j: (i, 0)),
            pl.BlockSpec((None, K1, tn), lambda i, j: (l, 0, j)),
            pl.BlockSpec((None, K2, tn), lambda i, j: (l, 1, j)),
        ],
        out_specs=pl.BlockSpec((tm, tn), lambda i, j: (i, j)),
        out_shape=jax.ShapeDtypeStruct((R, D), F32),
        compiler_params=_compiler_params(2, vmem),
        name="mix_out",
    )(x, y_pool, y_dn, w, w)


def _xattn_body(x_ref, g_ref, wq_ref, k_ref, v_ref, wo_ref, o_ref, h_ref, a_ref):
    j = pl.program_id(1)
    dh = wq_ref.shape[1]
    tn = wo_ref.shape[1]

    @pl.when(j == 0)
    def _():
        _norm_rows_into(h_ref, x_ref, g_ref)

    @pl.when(j < XA_HEADS)
    def _():
        hcols = pl.ds(pl.multiple_of(j * dh, dh), dh)
        tm = h_ref.shape[0]
        sub = min(XA_SUB_ROWS, tm)

        def scores(r0):
            q = _dot(h_ref[r0:r0 + sub, :], wq_ref[...])
            return _dot_nt(q.astype(BF16), k_ref[...]) * (dh ** -0.5)

        sc = scores(0)
        for r0 in range(0, tm, sub):
            sc_next = scores(r0 + sub) if r0 + sub < tm else None
            e = jnp.exp(sc - jnp.max(sc, axis=-1, keepdims=True))
            pr = e / jnp.sum(e, axis=-1, keepdims=True)
            a_ref[r0:r0 + sub, hcols] = _dot(pr.astype(BF16), v_ref[...]).astype(BF16)
            sc = sc_next

    @pl.when(j >= XA_HEADS)
    def _():
        cols = pl.ds(pl.multiple_of((j - XA_HEADS) * tn, tn), tn)
        o_ref[...] = x_ref[:, cols] + _dot(a_ref[...], wo_ref[...])


def cross_attention(x, g, w_xq, kv, w_xo, l, B, S, M, tm, tn):
    R, D = x.shape
    dh = D // XA_HEADS
    tiles_per_batch = S // tm
    last = XA_HEADS - 1
    head = lambda j: jnp.minimum(j, last)
    otile = lambda j: jnp.maximum(j - XA_HEADS, 0)
    vmem = (2 * tm * D * 4 + 2 * tm * D * 2 + 4 * D * dh * 2 + 4 * M * dh * 2 + 4 * D * tn * 2 + 3 * tm * tn * 4
            + 3 * tm * dh * 4 + 3 * tm * M * 4)
    return pl.pallas_call(
        _xattn_body,
        grid=(R // tm, XA_HEADS + D // tn),
        in_specs=[
            pl.BlockSpec((tm, D), lambda i, j: (i, 0)),
            pl.BlockSpec((None, 1, D), lambda i, j: (l, 0, 0)),
            pl.BlockSpec((None, D, dh), lambda i, j: (l, 0, head(j))),
            pl.BlockSpec((None, M, dh), lambda i, j: (l, i // tiles_per_batch, head(j))),
            pl.BlockSpec((None, M, dh), lambda i, j: (l, i // tiles_per_batch, XA_HEADS + head(j))),
            pl.BlockSpec((None, D, tn), lambda i, j: (l, 0, otile(j))),
        ],
        out_specs=pl.BlockSpec((tm, tn), lambda i, j: (i, otile(j))),
        out_shape=jax.ShapeDtypeStruct((R, D), F32),
        scratch_shapes=[pltpu.VMEM((tm, D), BF16), pltpu.VMEM((tm, D), BF16)],
        compiler_params=_compiler_params(2, vmem),
        name="cross_attention",
    )(x, g, w_xq, kv, kv, w_xo)


def _ffn_body(x_ref, g_ref, wg_ref, wu_ref, cw_ref, cb_ref, wd_ref, og_ref, o_ref, h_ref, gbuf, carry, *,
              tiles_per_seq, norm_output):
    i = pl.program_id(0)
    j = pl.program_id(1)
    tm = x_ref.shape[0]
    K = cw_ref.shape[0]
    halo = V7X_SUBLANES

    @pl.when(j == 0)
    def _():
        _norm_rows_into(h_ref, x_ref, g_ref)
        o_ref[...] = x_ref[...]

    gbuf[0:halo, :] = jnp.where((i % tiles_per_seq) == 0, 0.0, carry[j])
    sub = min(FFN_SUB_ROWS, tm)

    def gate_up(r0):
        h = h_ref[r0:r0 + sub, :]
        return _dot(h, wg_ref[...]), _dot(h, wu_ref[...])

    gate, up = gate_up(0)
    for r0 in range(0, tm, sub):
        nxt = gate_up(r0 + sub) if r0 + sub < tm else None
        gbuf[halo + r0:halo + r0 + sub, :] = gate
        conv = gate * cw_ref[K - 1:K, :] + cb_ref[...]
        for t in range(K - 2, -1, -1):
            conv = conv + gbuf[halo - K + 1 + t + r0:halo - K + 1 + t + r0 + sub, :] * cw_ref[t:t + 1, :]
        act = _silu(conv) * up
        o_ref[r0:r0 + sub, :] += _dot(act.astype(BF16), wd_ref[...])
        if nxt is not None:
            gate, up = nxt
    carry[j] = gbuf[tm:tm + halo, :]

    if norm_output:
        @pl.when(j == pl.num_programs(1) - 1)
        def _():
            _norm_rows_into(o_ref, o_ref, og_ref)


def conv_glu_ffn(x, g, w_gate, w_up, conv_w, conv_b, w_down, out_g, l, S, tm, tf, norm_output):
    R, D = x.shape
    F = w_gate.shape[2]
    K = conv_w.shape[1]
    nf = F // tf
    vmem = (3 * tm * D * 4 + tm * D * 2 + 2 * 3 * D * tf * 2 + (tm + V7X_SUBLANES) * tf * 4
            + nf * V7X_SUBLANES * tf * 4 + 4 * tm * tf * 4)
    return pl.pallas_call(
        functools.partial(_ffn_body, tiles_per_seq=S // tm, norm_output=norm_output),
        grid=(R // tm, nf),
        in_specs=[
            pl.BlockSpec((tm, D), lambda i, j: (i, 0), pipeline_mode=pl.Buffered(1)),
            pl.BlockSpec((None, 1, D), lambda i, j: (l, 0, 0)),
            pl.BlockSpec((None, D, tf), lambda i, j: (l, 0, j)),
            pl.BlockSpec((None, D, tf), lambda i, j: (l, 0, j)),
            pl.BlockSpec((None, K, tf), lambda i, j: (l, 0, j)),
            pl.BlockSpec((None, 1, tf), lambda i, j: (l, 0, j)),
            pl.BlockSpec((None, tf, D), lambda i, j: (l, j, 0)),
            pl.BlockSpec((1, D), lambda i, j: (0, 0)),
        ],
        out_specs=pl.BlockSpec((tm, D), lambda i, j: (i, 0)),
        out_shape=jax.ShapeDtypeStruct((R, D), F32),
        scratch_shapes=[
            pltpu.VMEM((tm, D), BF16),
            pltpu.VMEM((tm + V7X_SUBLANES, tf), F32),
            pltpu.VMEM((nf, V7X_SUBLANES, tf), F32),
        ],
        compiler_params=_compiler_params(2, vmem),
        name="conv_glu_ffn",
    )(x, g, w_gate, w_up, conv_w, conv_b, w_down, out_g)


def _tiles(S):
    return dict(tm=min(1024, S), ts_pool=min(512, S), ts_dn=min(512, S))


def kernel(x, mem, mix_norm_g, w_in, w_pool, pool_scale, dn_conv_w, dn_a_log, dn_dt_bias, dn_norm_g, w_mix_out,
           xa_norm_g, mem_norm_g, w_xq, w_xkv, w_xo, ffn_norm_g, w_gate, w_up, ffn_conv_w, ffn_conv_b, w_down,
           final_norm_g):
    B, S, D = x.shape
    M = mem.shape[1]
    depth = w_in.shape[0]
    H = dn_a_log.shape[1]
    dn_w = H * DN_HEAD_DIM
    pool_w = w_pool.shape[1] * w_pool.shape[2]
    main = pool_w + 4 * dn_w
    assert pool_w == dn_w and w_in.shape[2] == main + 2 * H and 2 * H <= V7X_LANES
    t = _tiles(S)

    tn_in = 512
    n_in = -(-w_in.shape[2] // tn_in) * tn_in
    w_in_b = jnp.pad(w_in, ((0, 0), (0, 0), (0, n_in - w_in.shape[2]))).astype(BF16)
    pad_gate = lambda a: jnp.pad(a, ((0, 0), (H, V7X_LANES - 2 * H)))[:, None, :]
    a_log_p, dt_bias_p = pad_gate(dn_a_log), pad_gate(dn_dt_bias)
    w_pool_b = w_pool.astype(BF16)
    w_mix_b = w_mix_out.astype(BF16)
    w_xq_b, w_xkv_b, w_xo_b = w_xq.astype(BF16), w_xkv.astype(BF16), w_xo.astype(BF16)
    w_gate_b, w_up_b, w_down_b = w_gate.astype(BF16), w_up.astype(BF16), w_down.astype(BF16)
    row = lambda a: a[:, None, :]

    xr = x.reshape(B * S, D)
    kv = norm_matmul(mem.reshape(B * M, D), mem_norm_g[None, :], w_xkv_b, BF16, tm=min(1024, B * M), tn=512)
    for l in range(depth):
        p = in_proj(xr, mix_norm_g[l][None, :], w_in_b, dn_conv_w, l, S, tm=t["tm"], tn=tn_in, conv_col0=pool_w)
        y_pool = pool_mixer(p, w_pool_b, row(pool_scale), l, B, S, t["ts_pool"])
        y_dn = gated_delta_net(p, a_log_p, dt_bias_p, row(dn_norm_g), l, B, S, t["ts_dn"],
                               col_block0=pool_w // dn_w, H=H)
        xr = mix_out(xr, y_pool, y_dn, w_mix_b, l, tm=t["tm"], tn=1024)
        xr = cross_attention(xr, row(xa_norm_g), w_xq_b, kv, w_xo_b, l, B, S, M, tm=t["tm"], tn=512)
        xr = conv_glu_ffn(xr, row(ffn_norm_g), w_gate_b, w_up_b, ffn_conv_w, row(ffn_conv_b), w_down_b,
                          final_norm_g[None, :], l, S, tm=t["tm"], tf=512, norm_output=(l == depth - 1))
    return xr.reshape(B, S, D)
```

```python
import functools
import math

import jax
import jax.numpy as jnp
from jax import lax
from jax.experimental import pallas as pl
from jax.experimental.pallas import tpu as pltpu

F32 = jnp.float32
BF16 = jnp.bfloat16

EPS = 1e-6
POOL_WINDOWS = (2, 4, 8, 16)
POOL_HALO = 16
DN_HEAD_DIM = 128
DN_CHUNK = 128
DN_HEADS_PER_STEP = 2
XA_HEADS = 4

V7X_VMEM_BYTES = 64 * 1024 * 1024
V7X_LANES = 128
V7X_SUBLANES = 8
NORM_ROWS = 128
XA_SUB_ROWS = 512
FFN_SUB_ROWS = 512
IN_PROJ_SUB_ROWS = 128


def _compiler_params(n_axes, vmem_bytes):
    limit = min(int(vmem_bytes * 1.25) + (4 << 20), V7X_VMEM_BYTES * 7 // 8)
    return pltpu.CompilerParams(dimension_semantics=("arbitrary",) * n_axes, vmem_limit_bytes=limit)


def _dot(a, b):
    return jnp.dot(a, b, preferred_element_type=F32)


def _dot_nt(a, b):
    return lax.dot_general(a, b, (((1,), (1,)), ((), ())), preferred_element_type=F32)


def _silu(x):
    return x * (1.0 / (1.0 + jnp.exp(-x)))


def _rms_rows(x, g):
    ms = jnp.mean(x * x, axis=-1, keepdims=True)
    return x * lax.rsqrt(ms + EPS) * g


def _rows_shifted_down(a, prev, s):
    n = prev.shape[0]
    rows = lax.broadcasted_iota(jnp.int32, prev.shape, 0)
    rolled = [pltpu.roll(piece, s, axis=0) for piece in [prev] + [a[r:r + n, :] for r in range(0, a.shape[0], n)]]
    return jnp.concatenate([jnp.where(rows < s, lo, hi) for lo, hi in zip(rolled[:-1], rolled[1:])], axis=0)


def _norm_rows_into(h_ref, x_ref, g_ref):
    g = g_ref[...]

    def body(r, c):
        rows = pl.ds(pl.multiple_of(r * NORM_ROWS, NORM_ROWS), NORM_ROWS)
        h_ref[rows, :] = _rms_rows(x_ref[rows, :], g).astype(h_ref.dtype)
        return c

    lax.fori_loop(0, x_ref.shape[0] // NORM_ROWS, body, 0)


def _norm_matmul_body(x_ref, g_ref, w_ref, o_ref, h_ref):
    @pl.when((pl.program_id(1) == 0) & (pl.program_id(2) == 0))
    def _():
        _norm_rows_into(h_ref, x_ref, g_ref)

    o_ref[...] = _dot(h_ref[...], w_ref[...]).astype(o_ref.dtype)


def _column_tiles(w, tn):
    L, K, N = w.shape
    return w.astype(BF16).reshape(L, K, N // tn, tn).transpose(0, 2, 1, 3)


def norm_matmul(x, g, w, out_dtype, tm):
    R, K = x.shape
    L, nt, _, tn = w.shape
    N = nt * tn
    osz = jnp.dtype(out_dtype).itemsize
    vmem = 2 * tm * K * 4 + tm * K * 2 + 2 * K * tn * 2 + 2 * tm * tn * osz + tm * tn * 4
    return pl.pallas_call(
        _norm_matmul_body,
        grid=(R // tm, L, N // tn),
        in_specs=[
            pl.BlockSpec((tm, K), lambda i, l, j: (i, 0)),
            pl.BlockSpec((1, K), lambda i, l, j: (0, 0)),
            pl.BlockSpec((None, None, K, tn), lambda i, l, j: (l, j, 0, 0)),
        ],
        out_specs=pl.BlockSpec((None, tm, tn), lambda i, l, j: (l, i, j)),
        out_shape=jax.ShapeDtypeStruct((L, R, N), out_dtype),
        scratch_shapes=[pltpu.VMEM((tm, K), BF16)],
        compiler_params=_compiler_params(3, vmem),
        name="norm_matmul",
    )(x, g, w)


def _in_proj_body(x_ref, g_ref, w_ref, cw_ref, p_ref, h_ref, carry, *, tiles_per_seq, conv0, tiles_per_part):
    i = pl.program_id(0)
    j = pl.program_id(1)
    tm, tn = p_ref.shape
    K = cw_ref.shape[0]
    halo = V7X_SUBLANES

    @pl.when(j == 0)
    def _():
        _norm_rows_into(h_ref, x_ref, g_ref)

    is_conv = (j >= conv0) & (j < conv0 + 3 * tiles_per_part)

    @pl.when(jnp.logical_not(is_conv))
    def _():
        p_ref[...] = _dot(h_ref[...], w_ref[...])

    @pl.when(is_conv)
    def _():
        cj = j - conv0
        part = cj // tiles_per_part
        q_scale = jnp.where(part == 0, DN_HEAD_DIM ** -0.5, 1.0)
        prev = jnp.where((i % tiles_per_seq) == 0, 0.0, carry[cj])
        sub = min(IN_PROJ_SUB_ROWS, tm)
        for r0 in range(0, tm, sub):
            acc = _dot(h_ref[r0:r0 + sub, :], w_ref[...])
            y = acc * cw_ref[K - 1:K, :]
            for s in range(1, K):
                y = y + _rows_shifted_down(acc, prev, s) * cw_ref[K - 1 - s:K - s, :]
            prev = acc[sub - halo:sub, :]
            y = _silu(y)
            for c in range(0, tn, DN_HEAD_DIM):
                yh = y[:, c:c + DN_HEAD_DIM]
                inv = lax.rsqrt(jnp.sum(yh * yh, axis=-1, keepdims=True) + EPS) * q_scale
                p_ref[r0:r0 + sub, c:c + DN_HEAD_DIM] = yh * jnp.where(part == 2, 1.0, inv)
        carry[cj] = prev


def in_proj(x, g, w, conv_w, l, S, tm, conv_col0):
    R, D = x.shape
    tn = w.shape[3]
    N = w.shape[1] * tn
    K, conv_cols = conv_w.shape[1:]
    assert conv_col0 % tn == 0 and (conv_cols // 3) % tn == 0 and tn % DN_HEAD_DIM == 0
    conv0, n_conv = conv_col0 // tn, conv_cols // tn
    vmem = (2 * tm * D * 4 + tm * D * 2 + 2 * D * tn * 2 + 3 * tm * tn * 4 + (tm + V7X_SUBLANES) * tn * 4
            + n_conv * V7X_SUBLANES * tn * 4 + 3 * tm * tn * 4)
    return pl.pallas_call(
        functools.partial(_in_proj_body, tiles_per_seq=S // tm, conv0=conv0, tiles_per_part=n_conv // 3),
        grid=(R // tm, N // tn),
        in_specs=[
            pl.BlockSpec((tm, D), lambda i, j: (i, 0)),
            pl.BlockSpec((1, D), lambda i, j: (0, 0)),
            pl.BlockSpec((None, None, D, tn), lambda i, j: (l, j, 0, 0)),
            pl.BlockSpec((None, K, tn), lambda i, j: (l, 0, jnp.clip(j - conv0, 0, n_conv - 1))),
        ],
        out_specs=pl.BlockSpec((tm, tn), lambda i, j: (i, j)),
        out_shape=jax.ShapeDtypeStruct((R, N), F32),
        scratch_shapes=[
            pltpu.VMEM((tm, D), BF16),
            pltpu.VMEM((n_conv, V7X_SUBLANES, tn), F32),
        ],
        compiler_params=_compiler_params(2, vmem),
        name="in_proj",
    )(x, g, w, conv_w)


def _pool_body(u_ref, wp_ref, sc_ref, o_ref, buf):
    s = pl.program_id(1)
    ts = u_ref.shape[0]
    G = wp_ref.shape[1]

    @pl.when(s == 0)
    def _():
        buf[0:POOL_HALO, :] = jnp.zeros((POOL_HALO, buf.shape[1]), F32)

    buf[POOL_HALO:POOL_HALO + ts, :] = u_ref[...]
    pos = (s * ts + 1 + lax.broadcasted_iota(jnp.int32, (ts, 1), 0)).astype(F32)
    for i, w in enumerate(POOL_WINDOWS):
        cols = slice(i * G, (i + 1) * G)
        u = buf[POOL_HALO:POOL_HALO + ts, cols]
        acc = u
        for k in range(1, w):
            acc = acc + buf[POOL_HALO - k:POOL_HALO - k + ts, cols]
        mixed = acc / jnp.minimum(pos, float(w)) - u
        y = _dot(mixed.astype(BF16), wp_ref[i]) * sc_ref[:, cols]
        o_ref[:, cols] = y.astype(o_ref.dtype)
    buf[0:POOL_HALO, :] = buf[ts:ts + POOL_HALO, :]


def pool_mixer(p, w_pool, pool_scale, l, B, S, ts):
    n_win, G, _ = w_pool.shape[1:]
    W = n_win * G
    nt = S // ts
    vmem = 2 * ts * W * 4 + (ts + POOL_HALO) * W * 4 + 2 * n_win * G * G * 2 + 2 * ts * W * 2 + 4 * ts * G * 4
    return pl.pallas_call(
        _pool_body,
        grid=(B, nt),
        in_specs=[
            pl.BlockSpec((ts, W), lambda b, s: (b * nt + s, 0)),
            pl.BlockSpec((None, n_win, G, G), lambda b, s: (l, 0, 0, 0)),
            pl.BlockSpec((None, 1, W), lambda b, s: (l, 0, 0)),
        ],
        out_specs=pl.BlockSpec((ts, W), lambda b, s: (b * nt + s, 0)),
        out_shape=jax.ShapeDtypeStruct((B * S, W), BF16),
        scratch_shapes=[pltpu.VMEM((ts + POOL_HALO, W), F32)],
        compiler_params=_compiler_params(2, vmem),
        name="pool_mixer",
    )(p, w_pool, pool_scale)


def _unit_lower_inverses(ls, eye):
    C = ls[0].shape[0]
    n_iter = int(math.log2(C)) - 1
    xs = [eye - l_mat for l_mat in ls]
    lbs = [l_mat.astype(BF16) for l_mat in ls]
    for _ in range(n_iter):
        xbs = [x.astype(BF16) for x in xs]
        lxs = [_dot(lb, xb) for lb, xb in zip(lbs, xbs)]
        es = [eye - x - lx for x, lx in zip(xs, lxs)]
        xs = [x + _dot(xb, e.astype(BF16)) for x, xb, e in zip(xs, xbs, es)]
    return xs


def _dn_body(q_ref, k_ref, v_ref, z_ref, ba_ref, alog_ref, dtb_ref, ng_ref, o_ref,
             state, beta_rep, gc_rep, grow_s, cdec_s, u_s, w_s, qd_s, at_s, kdt_s):
    s = pl.program_id(1)
    ts, W = q_ref.shape
    H = W // DN_HEAD_DIM
    C = DN_CHUNK
    n_chunk = ts // C

    @pl.when(s == 0)
    def _():
        state[...] = jnp.zeros(state.shape, F32)

    ba = ba_ref[...]
    beta = 1.0 / (1.0 + jnp.exp(-ba))
    xs = ba + dtb_ref[...]
    softplus = jnp.maximum(xs, 0.0) + jnp.log1p(jnp.exp(-jnp.abs(xs)))
    g = -jnp.exp(alog_ref[...]) * softplus
    row_in_chunk = lax.broadcasted_iota(jnp.int32, (ts, V7X_LANES), 0) % C
    gc = g
    sh = 1
    while sh < C:
        gc = gc + jnp.where(row_in_chunk >= sh, pltpu.roll(gc, sh, axis=0), 0.0)
        sh *= 2
    gct = gc.T
    lane = lax.broadcasted_iota(jnp.int32, (ts, V7X_LANES), 1)
    for h in range(H):
        beta_rep[h] = jnp.broadcast_to(
            jnp.sum(jnp.where(lane == h, beta, 0.0), axis=-1, keepdims=True), (ts, V7X_LANES))
        gc_rep[h] = jnp.broadcast_to(
            jnp.sum(jnp.where(lane == H + h, gc, 0.0), axis=-1, keepdims=True), (ts, V7X_LANES))
        grow_s[h] = gct[H + h:H + h + 1, :]

    ii = lax.broadcasted_iota(jnp.int32, (C, C), 0)
    jj = lax.broadcasted_iota(jnp.int32, (C, C), 1)
    eye = (ii == jj).astype(F32)

    def head_group_body(hg, carry):
        pairs = []
        for hh in range(DN_HEADS_PER_STEP):
            h = hg * DN_HEADS_PER_STEP + hh
            cols = pl.ds(pl.multiple_of(h * DN_HEAD_DIM, DN_HEAD_DIM), DN_HEAD_DIM)
            bcol, gcol, grow_all = beta_rep[h], gc_rep[h], grow_s[h]
            egc = jnp.exp(gcol)
            for c in range(n_chunk):
                r = slice(c * C, (c + 1) * C)
                pairs.append(dict(h=h, c=c, r=r, cols=cols, q=q_ref[r, cols], k=k_ref[r, cols], v=v_ref[r, cols],
                                  b=bcol[r], g=gcol[r], e=egc[r], grow=grow_all[:, r]))
        kbs = [p["k"].astype(BF16) for p in pairs]
        decs = [jnp.exp(jnp.minimum(p["g"] - p["grow"], 0.0)) for p in pairs]
        kks = [_dot_nt(kb, kb) for kb in kbs]
        qks = [_dot_nt(p["q"].astype(BF16), kb) for p, kb in zip(pairs, kbs)]
        ls = [jnp.where(ii > jj, kk * dec, 0.0) * p["b"] for kk, dec, p in zip(kks, decs, pairs)]
        for p, qk, dec in zip(pairs, qks, decs):
            r, cols = p["r"], p["cols"]
            at_s[r, cols] = jnp.where(ii >= jj, qk * dec, 0.0).astype(BF16)
            qd_s[r, cols] = (p["q"] * p["e"]).astype(BF16)
            kd = p["k"] * jnp.exp(p["g"][C - 1:C, :] - p["g"])
            kdt_s[r, cols] = kd.T.astype(BF16)
            cdec_s[p["c"], p["h"]] = p["e"][C - 1:C, :]
        t_invs = _unit_lower_inverses(ls, eye)
        rhs = [jnp.concatenate([p["v"] * p["b"], p["k"] * (p["b"] * p["e"])], axis=1) for p in pairs]
        sols = [_dot(t_inv.astype(BF16), b.astype(BF16)) for t_inv, b in zip(t_invs, rhs)]
        for p, sol in zip(pairs, sols):
            u_s[p["r"], p["cols"]] = sol[:, :DN_HEAD_DIM]
            w_s[p["r"], p["cols"]] = sol[:, DN_HEAD_DIM:].astype(BF16)
        return carry

    lax.fori_loop(0, H // DN_HEADS_PER_STEP, head_group_body, 0)

    def chunk_body(c, carry):
        r0 = pl.multiple_of(c * C, C)
        rows = pl.ds(r0, C)
        hs = range(H)
        cols = [slice(h * DN_HEAD_DIM, (h + 1) * DN_HEAD_DIM) for h in hs]
        sts = [state[h] for h in hs]
        sbs = [st.astype(BF16) for st in sts]
        wss = [_dot(w_s[rows, cols[h]], sbs[h]) for h in hs]
        qss = [_dot(qd_s[rows, cols[h]], sbs[h]) for h in hs]
        vbs = [(u_s[rows, cols[h]] - wss[h]).astype(BF16) for h in hs]
        avs = [_dot(at_s[rows, cols[h]], vbs[h]) for h in hs]
        kvs = [_dot(kdt_s[rows, cols[h]], vbs[h]) for h in hs]
        for h in hs:
            state[h] = sts[h] * cdec_s[c, h] + kvs[h]
            gated = _rms_rows(qss[h] + avs[h], ng_ref[...]) * _silu(z_ref[rows, cols[h]])
            o_ref[rows, cols[h]] = gated.astype(o_ref.dtype)
        return carry

    lax.fori_loop(0, n_chunk, chunk_body, 0)


def gated_delta_net(p, a_log, dt_bias, norm_g, l, B, S, ts, col_block0, H):
    W = H * DN_HEAD_DIM
    nt = S // ts
    NB = V7X_LANES
    gate_block = (col_block0 + 4) * (W // NB)
    C = DN_CHUNK
    vmem = (2 * 4 * ts * W * 4 + 2 * ts * NB * 4 + 2 * ts * W * 2
            + H * DN_HEAD_DIM * DN_HEAD_DIM * 4 + 2 * H * ts * V7X_LANES * 4 + H * V7X_SUBLANES * ts * 4
            + ts * W * 4 + 4 * ts * W * 2 + 48 * C * C * 4 * DN_HEADS_PER_STEP)

    def pblock(k):
        return pl.BlockSpec((ts, W), lambda b, s: (b * nt + s, col_block0 + k))

    return pl.pallas_call(
        _dn_body,
        grid=(B, nt),
        in_specs=[
            pblock(0), pblock(1), pblock(2), pblock(3),
            pl.BlockSpec((ts, NB), lambda b, s: (b * nt + s, gate_block)),
            pl.BlockSpec((None, 1, NB), lambda b, s: (l, 0, 0)),
            pl.BlockSpec((None, 1, NB), lambda b, s: (l, 0, 0)),
            pl.BlockSpec((None, 1, DN_HEAD_DIM), lambda b, s: (l, 0, 0)),
        ],
        out_specs=pl.BlockSpec((ts, W), lambda b, s: (b * nt + s, 0)),
        out_shape=jax.ShapeDtypeStruct((B * S, W), BF16),
        scratch_shapes=[
            pltpu.VMEM((H, DN_HEAD_DIM, DN_HEAD_DIM), F32),
            pltpu.VMEM((H, ts, V7X_LANES), F32),
            pltpu.VMEM((H, ts, V7X_LANES), F32),
            pltpu.VMEM((H, 1, ts), F32),
            pltpu.VMEM((ts // C, H, 1, V7X_LANES), F32),
            pltpu.VMEM((ts, W), F32),
            pltpu.VMEM((ts, W), BF16),
            pltpu.VMEM((ts, W), BF16),
            pltpu.VMEM((ts, W), BF16),
            pltpu.VMEM((ts, W), BF16),
        ],
        compiler_params=_compiler_params(2, vmem),
        name="gated_delta_net",
    )(p, p, p, p, p, a_log, dt_bias, norm_g)


def _mix_out_body(x_ref, yp_ref, yd_ref, w1_ref, w2_ref, o_ref):
    o_ref[...] = x_ref[...] + _dot(yp_ref[...], w1_ref[...]) + _dot(yd_ref[...], w2_ref[...])


def mix_out(x, y_pool, y_dn, w, l, tm, tn):
    R, D = x.shape
    K1, K2 = y_pool.shape[1], y_dn.shape[1]
    assert K1 == K2
    vmem = 2 * (2 * tm * tn * 4 + tm * (K1 + K2) * 2 + (K1 + K2) * tn * 2) + 2 * tm * tn * 4
    return pl.pallas_call(
        _mix_out_body,
        grid=(R // tm, D // tn),
        in_specs=[
            pl.BlockSpec((tm, tn), lambda i, j: (i, j)),
            pl.BlockSpec((tm, K1), lambda i, j: (i, 0)),
            pl.BlockSpec((tm, K2), lambda i, j: (i, 0)),
            pl.BlockSpec((None, K1, tn), lambda i, j: (l, 0, j)),
            pl.BlockSpec((None, K2, tn), lambda i, j: (l, 1, j)),
        ],
        out_specs=pl.BlockSpec((tm, tn), lambda i, j: (i, j)),
        out_shape=jax.ShapeDtypeStruct((R, D), F32),
        compiler_params=_compiler_params(2, vmem),
        name="mix_out",
    )(x, y_pool, y_dn, w, w)


def _xattn_body(x_ref, g_ref, wq_ref, k_ref, v_ref, wo_ref, o_ref, h_ref, a_ref):
    j = pl.program_id(1)
    dh = wq_ref.shape[1]
    tn = wo_ref.shape[1]

    @pl.when(j == 0)
    def _():
        _norm_rows_into(h_ref, x_ref, g_ref)

    @pl.when(j < XA_HEADS)
    def _():
        hcols = pl.ds(pl.multiple_of(j * dh, dh), dh)
        tm = h_ref.shape[0]
        sub = min(XA_SUB_ROWS, tm)

        def scores(r0):
            q = _dot(h_ref[r0:r0 + sub, :], wq_ref[...])
            return _dot_nt(q.astype(BF16), k_ref[...]) * (dh ** -0.5)

        sc = scores(0)
        for r0 in range(0, tm, sub):
            sc_next = scores(r0 + sub) if r0 + sub < tm else None
            e = jnp.exp(sc - jnp.max(sc, axis=-1, keepdims=True))
            pr = e / jnp.sum(e, axis=-1, keepdims=True)
            a_ref[r0:r0 + sub, hcols] = _dot(pr.astype(BF16), v_ref[...]).astype(BF16)
            sc = sc_next

    @pl.when(j >= XA_HEADS)
    def _():
        cols = pl.ds(pl.multiple_of((j - XA_HEADS) * tn, tn), tn)
        o_ref[...] = x_ref[:, cols] + _dot(a_ref[...], wo_ref[...])


def cross_attention(x, g, w_xq, kv, w_xo, l, B, S, M, tm):
    R, D = x.shape
    dh = D // XA_HEADS
    tn = w_xo.shape[3]
    assert w_xq.shape[1:] == (XA_HEADS, D, dh)
    tiles_per_batch = S // tm
    last = XA_HEADS - 1
    head = lambda j: jnp.minimum(j, last)
    otile = lambda j: jnp.maximum(j - XA_HEADS, 0)
    vmem = (2 * tm * D * 4 + 2 * tm * D * 2 + 4 * D * dh * 2 + 4 * M * dh * 2 + 4 * D * tn * 2 + 3 * tm * tn * 4
            + 3 * tm * dh * 4 + 3 * tm * M * 4)
    return pl.pallas_call(
        _xattn_body,
        grid=(R // tm, XA_HEADS + D // tn),
        in_specs=[
            pl.BlockSpec((tm, D), lambda i, j: (i, 0)),
            pl.BlockSpec((None, 1, D), lambda i, j: (l, 0, 0)),
            pl.BlockSpec((None, None, D, dh), lambda i, j: (l, head(j), 0, 0)),
            pl.BlockSpec((None, M, dh), lambda i, j: (l, i // tiles_per_batch, head(j))),
            pl.BlockSpec((None, M, dh), lambda i, j: (l, i // tiles_per_batch, XA_HEADS + head(j))),
            pl.BlockSpec((None, None, D, tn), lambda i, j: (l, otile(j), 0, 0)),
        ],
        out_specs=pl.BlockSpec((tm, tn), lambda i, j: (i, otile(j))),
        out_shape=jax.ShapeDtypeStruct((R, D), F32),
        scratch_shapes=[pltpu.VMEM((tm, D), BF16), pltpu.VMEM((tm, D), BF16)],
        compiler_params=_compiler_params(2, vmem),
        name="cross_attention",
    )(x, g, w_xq, kv, kv, w_xo)


def _ffn_body(x_ref, g_ref, wg_ref, wu_ref, cw_ref, cb_ref, wd_ref, og_ref, o_ref, h_ref, gbuf, carry, *,
              tiles_per_seq, norm_output):
    i = pl.program_id(0)
    j = pl.program_id(1)
    tm = x_ref.shape[0]
    K = cw_ref.shape[0]
    halo = V7X_SUBLANES

    @pl.when(j == 0)
    def _():
        _norm_rows_into(h_ref, x_ref, g_ref)
        o_ref[...] = x_ref[...]

    gbuf[0:halo, :] = jnp.where((i % tiles_per_seq) == 0, 0.0, carry[j])
    sub = min(FFN_SUB_ROWS, tm)

    def gate_up(r0):
        h = h_ref[r0:r0 + sub, :]
        return _dot(h, wg_ref[...]), _dot(h, wu_ref[...])

    gate, up = gate_up(0)
    for r0 in range(0, tm, sub):
        nxt = gate_up(r0 + sub) if r0 + sub < tm else None
        gbuf[halo + r0:halo + r0 + sub, :] = gate
        conv = gate * cw_ref[K - 1:K, :] + cb_ref[...]
        for t in range(K - 2, -1, -1):
            conv = conv + gbuf[halo - K + 1 + t + r0:halo - K + 1 + t + r0 + sub, :] * cw_ref[t:t + 1, :]
        act = _silu(conv) * up
        o_ref[r0:r0 + sub, :] += _dot(act.astype(BF16), wd_ref[...])
        if nxt is not None:
            gate, up = nxt
    carry[j] = gbuf[tm:tm + halo, :]

    if norm_output:
        @pl.when(j == pl.num_programs(1) - 1)
        def _():
            _norm_rows_into(o_ref, o_ref, og_ref)


def conv_glu_ffn(x, g, w_gate, w_up, conv_w, conv_b, w_down, out_g, l, S, tm, norm_output):
    R, D = x.shape
    nf, tf = w_gate.shape[1], w_gate.shape[3]
    K = conv_w.shape[1]
    vmem = (4 * tm * D * 4 + tm * D * 2 + 2 * 3 * D * tf * 2 + (tm + V7X_SUBLANES) * tf * 4
            + nf * V7X_SUBLANES * tf * 4 + 4 * tm * tf * 4)
    return pl.pallas_call(
        functools.partial(_ffn_body, tiles_per_seq=S // tm, norm_output=norm_output),
        grid=(R // tm, nf),
        in_specs=[
            pl.BlockSpec((tm, D), lambda i, j: (i, 0)),
            pl.BlockSpec((None, 1, D), lambda i, j: (l, 0, 0)),
            pl.BlockSpec((None, None, D, tf), lambda i, j: (l, j, 0, 0)),
            pl.BlockSpec((None, None, D, tf), lambda i, j: (l, j, 0, 0)),
            pl.BlockSpec((None, K, tf), lambda i, j: (l, 0, j)),
            pl.BlockSpec((None, 1, tf), lambda i, j: (l, 0, j)),
            pl.BlockSpec((None, tf, D), lambda i, j: (l, j, 0)),
            pl.BlockSpec((1, D), lambda i, j: (0, 0)),
        ],
        out_specs=pl.BlockSpec((tm, D), lambda i, j: (i, 0)),
        out_shape=jax.ShapeDtypeStruct((R, D), F32),
        scratch_shapes=[
            pltpu.VMEM((tm, D), BF16),
            pltpu.VMEM((tm + V7X_SUBLANES, tf), F32),
            pltpu.VMEM((nf, V7X_SUBLANES, tf), F32),
        ],
        compiler_params=_compiler_params(2, vmem),
        name="conv_glu_ffn",
    )(x, g, w_gate, w_up, conv_w, conv_b, w_down, out_g)


def _tiles(S):
    return dict(tm=min(1024, S), ts_pool=min(512, S), ts_dn=min(512, S))


def kernel(x, mem, mix_norm_g, w_in, w_pool, pool_scale, dn_conv_w, dn_a_log, dn_dt_bias, dn_norm_g, w_mix_out,
           xa_norm_g, mem_norm_g, w_xq, w_xkv, w_xo, ffn_norm_g, w_gate, w_up, ffn_conv_w, ffn_conv_b, w_down,
           final_norm_g):
    B, S, D = x.shape
    M = mem.shape[1]
    depth = w_in.shape[0]
    H = dn_a_log.shape[1]
    dn_w = H * DN_HEAD_DIM
    pool_w = w_pool.shape[1] * w_pool.shape[2]
    main = pool_w + 4 * dn_w
    assert pool_w == dn_w and w_in.shape[2] == main + 2 * H and 2 * H <= V7X_LANES
    t = _tiles(S)

    tn = 512
    n_in = -(-w_in.shape[2] // tn) * tn
    w_in_t = _column_tiles(jnp.pad(w_in.astype(BF16), ((0, 0), (0, 0), (0, n_in - w_in.shape[2]))), tn)
    pad_gate = lambda a: jnp.pad(a, ((0, 0), (H, V7X_LANES - 2 * H)))[:, None, :]
    a_log_p, dt_bias_p = pad_gate(dn_a_log), pad_gate(dn_dt_bias)
    w_pool_b = w_pool.astype(BF16)
    w_mix_b = w_mix_out.astype(BF16)
    w_xq_t, w_xkv_t, w_xo_t = _column_tiles(w_xq, D // XA_HEADS), _column_tiles(w_xkv, tn), _column_tiles(w_xo, tn)
    w_gate_t, w_up_t, w_down_b = _column_tiles(w_gate, tn), _column_tiles(w_up, tn), w_down.astype(BF16)
    row = lambda a: a[:, None, :]

    xr = x.reshape(B * S, D)
    kv = norm_matmul(mem.reshape(B * M, D), mem_norm_g[None, :], w_xkv_t, BF16, tm=min(1024, B * M))
    for l in range(depth):
        p = in_proj(xr, mix_norm_g[l][None, :], w_in_t, dn_conv_w, l, S, tm=t["tm"], conv_col0=pool_w)
        y_pool = pool_mixer(p, w_pool_b, row(pool_scale), l, B, S, t["ts_pool"])
        y_dn = gated_delta_net(p, a_log_p, dt_bias_p, row(dn_norm_g), l, B, S, t["ts_dn"],
                               col_block0=pool_w // dn_w, H=H)
        xr = mix_out(xr, y_pool, y_dn, w_mix_b, l, tm=t["tm"], tn=1024)
        xr = cross_attention(xr, row(xa_norm_g), w_xq_t, kv, w_xo_t, l, B, S, M, tm=t["tm"])
        xr = conv_glu_ffn(xr, row(ffn_norm_g), w_gate_t, w_up_t, ffn_conv_w, row(ffn_conv_b), w_down_b,
                          final_norm_g[None, :], l, S, tm=t["tm"], norm_output=(l == depth - 1))
    return xr.reshape(B, S, D)
```

```python
import functools
import math

import jax
import jax.numpy as jnp
from jax import lax
from jax.experimental import pallas as pl
from jax.experimental.pallas import tpu as pltpu

F32 = jnp.float32
BF16 = jnp.bfloat16

EPS = 1e-6
POOL_WINDOWS = (2, 4, 8, 16)
POOL_HALO = 16
DN_HEAD_DIM = 128
DN_CHUNK = 128
DN_HEADS_PER_STEP = 2
XA_HEADS = 4

V7X_VMEM_BYTES = 64 * 1024 * 1024
V7X_LANES = 128
V7X_SUBLANES = 8
NORM_ROWS = 128
XA_SUB_ROWS = 512
FFN_SUB_ROWS = 512
IN_PROJ_SUB_ROWS = 128


def _compiler_params(n_axes, vmem_bytes):
    limit = min(int(vmem_bytes * 1.25) + (4 << 20), V7X_VMEM_BYTES * 7 // 8)
    return pltpu.CompilerParams(dimension_semantics=("arbitrary",) * n_axes, vmem_limit_bytes=limit)


def _dot(a, b):
    return jnp.dot(a, b, preferred_element_type=F32)


def _dot_nt(a, b):
    return lax.dot_general(a, b, (((1,), (1,)), ((), ())), preferred_element_type=F32)


def _silu(x):
    return x * (1.0 / (1.0 + jnp.exp(-x)))


def _rms_rows(x, g):
    ms = jnp.mean(x * x, axis=-1, keepdims=True)
    return x * lax.rsqrt(ms + EPS) * g


def _rows_shifted_down(a, prev, s):
    n = prev.shape[0]
    rows = lax.broadcasted_iota(jnp.int32, prev.shape, 0)
    rolled = [pltpu.roll(piece, s, axis=0) for piece in [prev] + [a[r:r + n, :] for r in range(0, a.shape[0], n)]]
    return jnp.concatenate([jnp.where(rows < s, lo, hi) for lo, hi in zip(rolled[:-1], rolled[1:])], axis=0)


def _norm_rows_into(h_ref, x_ref, g_ref):
    g = g_ref[...]

    def body(r, c):
        rows = pl.ds(pl.multiple_of(r * NORM_ROWS, NORM_ROWS), NORM_ROWS)
        h_ref[rows, :] = _rms_rows(x_ref[rows, :], g).astype(h_ref.dtype)
        return c

    lax.fori_loop(0, x_ref.shape[0] // NORM_ROWS, body, 0)


def _norm_matmul_body(x_ref, g_ref, w_ref, o_ref, h_ref):
    @pl.when((pl.program_id(1) == 0) & (pl.program_id(2) == 0))
    def _():
        _norm_rows_into(h_ref, x_ref, g_ref)

    o_ref[...] = _dot(h_ref[...], w_ref[...]).astype(o_ref.dtype)


def norm_matmul(x, g, w, out_dtype, tm, tn):
    R, K = x.shape
    L, _, N = w.shape
    osz = jnp.dtype(out_dtype).itemsize
    vmem = 2 * tm * K * 4 + tm * K * 2 + 2 * K * tn * 2 + 2 * tm * tn * osz + tm * tn * 4
    return pl.pallas_call(
        _norm_matmul_body,
        grid=(R // tm, L, N // tn),
        in_specs=[
            pl.BlockSpec((tm, K), lambda i, l, j: (i, 0)),
            pl.BlockSpec((1, K), lambda i, l, j: (0, 0)),
            pl.BlockSpec((None, K, tn), lambda i, l, j: (l, 0, j)),
        ],
        out_specs=pl.BlockSpec((None, tm, tn), lambda i, l, j: (l, i, j)),
        out_shape=jax.ShapeDtypeStruct((L, R, N), out_dtype),
        scratch_shapes=[pltpu.VMEM((tm, K), BF16)],
        compiler_params=_compiler_params(3, vmem),
        name="norm_matmul",
    )(x, g, w)


def _in_proj_body(x_ref, g_ref, w_ref, cw_ref, p_ref, h_ref, carry, *, tiles_per_seq, conv0, tiles_per_part):
    i = pl.program_id(0)
    j = pl.program_id(1)
    tm, tn = p_ref.shape
    K = cw_ref.shape[0]
    halo = V7X_SUBLANES

    @pl.when(j == 0)
    def _():
        _norm_rows_into(h_ref, x_ref, g_ref)

    is_conv = (j >= conv0) & (j < conv0 + 3 * tiles_per_part)

    @pl.when(jnp.logical_not(is_conv))
    def _():
        p_ref[...] = _dot(h_ref[...], w_ref[...])

    @pl.when(is_conv)
    def _():
        cj = j - conv0
        part = cj // tiles_per_part
        q_scale = jnp.where(part == 0, DN_HEAD_DIM ** -0.5, 1.0)
        prev = jnp.where((i % tiles_per_seq) == 0, 0.0, carry[cj])
        sub = min(IN_PROJ_SUB_ROWS, tm)
        for r0 in range(0, tm, sub):
            acc = _dot(h_ref[r0:r0 + sub, :], w_ref[...])
            y = acc * cw_ref[K - 1:K, :]
            for s in range(1, K):
                y = y + _rows_shifted_down(acc, prev, s) * cw_ref[K - 1 - s:K - s, :]
            prev = acc[sub - halo:sub, :]
            y = _silu(y)
            for c in range(0, tn, DN_HEAD_DIM):
                yh = y[:, c:c + DN_HEAD_DIM]
                inv = lax.rsqrt(jnp.sum(yh * yh, axis=-1, keepdims=True) + EPS) * q_scale
                p_ref[r0:r0 + sub, c:c + DN_HEAD_DIM] = yh * jnp.where(part == 2, 1.0, inv)
        carry[cj] = prev


def in_proj(x, g, w, conv_w, l, S, tm, tn, conv_col0):
    R, D = x.shape
    N = w.shape[2]
    K, conv_cols = conv_w.shape[1:]
    assert conv_col0 % tn == 0 and (conv_cols // 3) % tn == 0 and tn % DN_HEAD_DIM == 0
    conv0, n_conv = conv_col0 // tn, conv_cols // tn
    vmem = (2 * tm * D * 4 + tm * D * 2 + 2 * D * tn * 2 + 3 * tm * tn * 4 + (tm + V7X_SUBLANES) * tn * 4
            + n_conv * V7X_SUBLANES * tn * 4 + 3 * tm * tn * 4)
    return pl.pallas_call(
        functools.partial(_in_proj_body, tiles_per_seq=S // tm, conv0=conv0, tiles_per_part=n_conv // 3),
        grid=(R // tm, N // tn),
        in_specs=[
            pl.BlockSpec((tm, D), lambda i, j: (i, 0)),
            pl.BlockSpec((1, D), lambda i, j: (0, 0)),
            pl.BlockSpec((None, D, tn), lambda i, j: (l, 0, j)),
            pl.BlockSpec((None, K, tn), lambda i, j: (l, 0, jnp.clip(j - conv0, 0, n_conv - 1))),
        ],
        out_specs=pl.BlockSpec((tm, tn), lambda i, j: (i, j)),
        out_shape=jax.ShapeDtypeStruct((R, N), F32),
        scratch_shapes=[
            pltpu.VMEM((tm, D), BF16),
            pltpu.VMEM((n_conv, V7X_SUBLANES, tn), F32),
        ],
        compiler_params=_compiler_params(2, vmem),
        name="in_proj",
    )(x, g, w, conv_w)


def _pool_body(u_ref, wp_ref, sc_ref, o_ref, buf):
    s = pl.program_id(1)
    ts = u_ref.shape[0]
    G = wp_ref.shape[1]

    @pl.when(s == 0)
    def _():
        buf[0:POOL_HALO, :] = jnp.zeros((POOL_HALO, buf.shape[1]), F32)

    buf[POOL_HALO:POOL_HALO + ts, :] = u_ref[...]
    pos = (s * ts + 1 + lax.broadcasted_iota(jnp.int32, (ts, 1), 0)).astype(F32)
    for i, w in enumerate(POOL_WINDOWS):
        cols = slice(i * G, (i + 1) * G)
        u = buf[POOL_HALO:POOL_HALO + ts, cols]
        acc = u
        for k in range(1, w):
            acc = acc + buf[POOL_HALO - k:POOL_HALO - k + ts, cols]
        mixed = acc / jnp.minimum(pos, float(w)) - u
        y = _dot(mixed.astype(BF16), wp_ref[i]) * sc_ref[:, cols]
        o_ref[:, cols] = y.astype(o_ref.dtype)
    buf[0:POOL_HALO, :] = buf[ts:ts + POOL_HALO, :]


def pool_mixer(p, w_pool, pool_scale, l, B, S, ts):
    n_win, G, _ = w_pool.shape[1:]
    W = n_win * G
    nt = S // ts
    vmem = 2 * ts * W * 4 + (ts + POOL_HALO) * W * 4 + 2 * n_win * G * G * 2 + 2 * ts * W * 2 + 4 * ts * G * 4
    return pl.pallas_call(
        _pool_body,
        grid=(B, nt),
        in_specs=[
            pl.BlockSpec((ts, W), lambda b, s: (b * nt + s, 0)),
            pl.BlockSpec((None, n_win, G, G), lambda b, s: (l, 0, 0, 0)),
            pl.BlockSpec((None, 1, W), lambda b, s: (l, 0, 0)),
        ],
        out_specs=pl.BlockSpec((ts, W), lambda b, s: (b * nt + s, 0)),
        out_shape=jax.ShapeDtypeStruct((B * S, W), BF16),
        scratch_shapes=[pltpu.VMEM((ts + POOL_HALO, W), F32)],
        compiler_params=_compiler_params(2, vmem),
        name="pool_mixer",
    )(p, w_pool, pool_scale)


def _unit_lower_inverses(ls, eye):
    C = ls[0].shape[0]
    n_iter = int(math.log2(C)) - 1
    xs = [eye - l_mat for l_mat in ls]
    lbs = [l_mat.astype(BF16) for l_mat in ls]
    for _ in range(n_iter):
        xbs = [x.astype(BF16) for x in xs]
        lxs = [_dot(lb, xb) for lb, xb in zip(lbs, xbs)]
        es = [eye - x - lx for x, lx in zip(xs, lxs)]
        xs = [x + _dot(xb, e.astype(BF16)) for x, xb, e in zip(xs, xbs, es)]
    return xs


def _dn_body(q_ref, k_ref, v_ref, z_ref, ba_ref, alog_ref, dtb_ref, ng_ref, o_ref,
             state, beta_rep, gc_rep, grow_s, cdec_s, u_s, w_s, qd_s, at_s, kdt_s):
    s = pl.program_id(1)
    ts, W = q_ref.shape
    H = W // DN_HEAD_DIM
    C = DN_CHUNK
    n_chunk = ts // C

    @pl.when(s == 0)
    def _():
        state[...] = jnp.zeros(state.shape, F32)

    ba = ba_ref[...]
    beta = 1.0 / (1.0 + jnp.exp(-ba))
    xs = ba + dtb_ref[...]
    softplus = jnp.maximum(xs, 0.0) + jnp.log1p(jnp.exp(-jnp.abs(xs)))
    g = -jnp.exp(alog_ref[...]) * softplus
    row_in_chunk = lax.broadcasted_iota(jnp.int32, (ts, V7X_LANES), 0) % C
    gc = g
    sh = 1
    while sh < C:
        gc = gc + jnp.where(row_in_chunk >= sh, pltpu.roll(gc, sh, axis=0), 0.0)
        sh *= 2
    gct = gc.T
    lane = lax.broadcasted_iota(jnp.int32, (ts, V7X_LANES), 1)
    for h in range(H):
        beta_rep[h] = jnp.broadcast_to(
            jnp.sum(jnp.where(lane == h, beta, 0.0), axis=-1, keepdims=True), (ts, V7X_LANES))
        gc_rep[h] = jnp.broadcast_to(
            jnp.sum(jnp.where(lane == H + h, gc, 0.0), axis=-1, keepdims=True), (ts, V7X_LANES))
        grow_s[h] = gct[H + h:H + h + 1, :]

    ii = lax.broadcasted_iota(jnp.int32, (C, C), 0)
    jj = lax.broadcasted_iota(jnp.int32, (C, C), 1)
    eye = (ii == jj).astype(F32)

    def head_group_body(hg, carry):
        pairs = []
        for hh in range(DN_HEADS_PER_STEP):
            h = hg * DN_HEADS_PER_STEP + hh
            cols = pl.ds(pl.multiple_of(h * DN_HEAD_DIM, DN_HEAD_DIM), DN_HEAD_DIM)
            bcol, gcol, grow_all = beta_rep[h], gc_rep[h], grow_s[h]
            egc = jnp.exp(gcol)
            for c in range(n_chunk):
                r = slice(c * C, (c + 1) * C)
                pairs.append(dict(h=h, c=c, r=r, cols=cols, q=q_ref[r, cols], k=k_ref[r, cols], v=v_ref[r, cols],
                                  b=bcol[r], g=gcol[r], e=egc[r], grow=grow_all[:, r]))
        kbs = [p["k"].astype(BF16) for p in pairs]
        decs = [jnp.exp(jnp.minimum(p["g"] - p["grow"], 0.0)) for p in pairs]
        kks = [_dot_nt(kb, kb) for kb in kbs]
        qks = [_dot_nt(p["q"].astype(BF16), kb) for p, kb in zip(pairs, kbs)]
        ls = [jnp.where(ii > jj, kk * dec, 0.0) * p["b"] for kk, dec, p in zip(kks, decs, pairs)]
        for p, qk, dec in zip(pairs, qks, decs):
            r, cols = p["r"], p["cols"]
            at_s[r, cols] = jnp.where(ii >= jj, qk * dec, 0.0).astype(BF16)
            qd_s[r, cols] = (p["q"] * p["e"]).astype(BF16)
            kd = p["k"] * jnp.exp(p["g"][C - 1:C, :] - p["g"])
            kdt_s[r, cols] = kd.T.astype(BF16)
            cdec_s[p["c"], p["h"]] = p["e"][C - 1:C, :]
        t_invs = _unit_lower_inverses(ls, eye)
        rhs = [jnp.concatenate([p["v"] * p["b"], p["k"] * (p["b"] * p["e"])], axis=1) for p in pairs]
        sols = [_dot(t_inv.astype(BF16), b.astype(BF16)) for t_inv, b in zip(t_invs, rhs)]
        for p, sol in zip(pairs, sols):
            u_s[p["r"], p["cols"]] = sol[:, :DN_HEAD_DIM]
            w_s[p["r"], p["cols"]] = sol[:, DN_HEAD_DIM:].astype(BF16)
        return carry

    lax.fori_loop(0, H // DN_HEADS_PER_STEP, head_group_body, 0)

    def chunk_body(c, carry):
        r0 = pl.multiple_of(c * C, C)
        rows = pl.ds(r0, C)
        hs = range(H)
        cols = [slice(h * DN_HEAD_DIM, (h + 1) * DN_HEAD_DIM) for h in hs]
        sts = [state[h] for h in hs]
        sbs = [st.astype(BF16) for st in sts]
        wss = [_dot(w_s[rows, cols[h]], sbs[h]) for h in hs]
        qss = [_dot(qd_s[rows, cols[h]], sbs[h]) for h in hs]
        vbs = [(u_s[rows, cols[h]] - wss[h]).astype(BF16) for h in hs]
        avs = [_dot(at_s[rows, cols[h]], vbs[h]) for h in hs]
        kvs = [_dot(kdt_s[rows, cols[h]], vbs[h]) for h in hs]
        for h in hs:
            state[h] = sts[h] * cdec_s[c, h] + kvs[h]
            gated = _rms_rows(qss[h] + avs[h], ng_ref[...]) * _silu(z_ref[rows, cols[h]])
            o_ref[rows, cols[h]] = gated.astype(o_ref.dtype)
        return carry

    lax.fori_loop(0, n_chunk, chunk_body, 0)


def gated_delta_net(p, a_log, dt_bias, norm_g, l, B, S, ts, col_block0, H):
    W = H * DN_HEAD_DIM
    nt = S // ts
    NB = V7X_LANES
    gate_block = (col_block0 + 4) * (W // NB)
    C = DN_CHUNK
    vmem = (2 * 4 * ts * W * 4 + 2 * ts * NB * 4 + 2 * ts * W * 2
            + H * DN_HEAD_DIM * DN_HEAD_DIM * 4 + 2 * H * ts * V7X_LANES * 4 + H * V7X_SUBLANES * ts * 4
            + ts * W * 4 + 4 * ts * W * 2 + 48 * C * C * 4 * DN_HEADS_PER_STEP)

    def pblock(k):
        return pl.BlockSpec((ts, W), lambda b, s: (b * nt + s, col_block0 + k))

    return pl.pallas_call(
        _dn_body,
        grid=(B, nt),
        in_specs=[
            pblock(0), pblock(1), pblock(2), pblock(3),
            pl.BlockSpec((ts, NB), lambda b, s: (b * nt + s, gate_block)),
            pl.BlockSpec((None, 1, NB), lambda b, s: (l, 0, 0)),
            pl.BlockSpec((None, 1, NB), lambda b, s: (l, 0, 0)),
            pl.BlockSpec((None, 1, DN_HEAD_DIM), lambda b, s: (l, 0, 0)),
        ],
        out_specs=pl.BlockSpec((ts, W), lambda b, s: (b * nt + s, 0)),
        out_shape=jax.ShapeDtypeStruct((B * S, W), BF16),
        scratch_shapes=[
            pltpu.VMEM((H, DN_HEAD_DIM, DN_HEAD_DIM), F32),
            pltpu.VMEM((H, ts, V7X_LANES), F32),
            pltpu.VMEM((H, ts, V7X_LANES), F32),
            pltpu.VMEM((H, 1, ts), F32),
            pltpu.VMEM((ts // C, H, 1, V7X_LANES), F32),
            pltpu.VMEM((ts, W), F32),
            pltpu.VMEM((ts, W), BF16),
            pltpu.VMEM((ts, W), BF16),
            pltpu.VMEM((ts, W), BF16),
            pltpu.VMEM((ts, W), BF16),
        ],
        compiler_params=_compiler_params(2, vmem),
        name="gated_delta_net",
    )(p, p, p, p, p, a_log, dt_bias, norm_g)


def _mix_out_body(x_ref, yp_ref, yd_ref, w1_ref, w2_ref, o_ref):
    o_ref[...] = x_ref[...] + _dot(yp_ref[...], w1_ref[...]) + _dot(yd_ref[...], w2_ref[...])


def mix_out(x, y_pool, y_dn, w, l, tm, tn):
    R, D = x.shape
    K1, K2 = y_pool.shape[1], y_dn.shape[1]
    assert K1 == K2
    vmem = 2 * (2 * tm * tn * 4 + tm * (K1 + K2) * 2 + (K1 + K2) * tn * 2) + 2 * tm * tn * 4
    return pl.pallas_call(
        _mix_out_body,
        grid=(R // tm, D // tn),
        in_specs=[
            pl.BlockSpec((tm, tn), lambda i, j: (i, j)),
            pl.BlockSpec((tm, K1), lambda i, j: (i, 0)),
            pl.BlockSpec((tm, K2), lambda i, j: (i, 0)),
            pl.BlockSpec((None, K1, tn), lambda i, j: (l, 0, j)),
            pl.BlockSpec((None, K2, tn), lambda i, j: (l, 1, j)),
        ],
        out_specs=pl.BlockSpec((tm, tn), lambda i, j: (i, j)),
        out_shape=jax.ShapeDtypeStruct((R, D), F32),
        compiler_params=_compiler_params(2, vmem),
        name="mix_out",
    )(x, y_pool, y_dn, w, w)


def _xattn_body(x_ref, g_ref, wq_ref, k_ref, v_ref, wo_ref, o_ref, h_ref, a_ref):
    j = pl.program_id(1)
    dh = wq_ref.shape[1]
    tn = wo_ref.shape[1]

    @pl.when(j == 0)
    def _():
        _norm_rows_into(h_ref, x_ref, g_ref)

    @pl.when(j < XA_HEADS)
    def _():
        hcols = pl.ds(pl.multiple_of(j * dh, dh), dh)
        tm = h_ref.shape[0]
        sub = min(XA_SUB_ROWS, tm)

        def scores(r0):
            q = _dot(h_ref[r0:r0 + sub, :], wq_ref[...])
            return _dot_nt(q.astype(BF16), k_ref[...]) * (dh ** -0.5)

        sc = scores(0)
        for r0 in range(0, tm, sub):
            sc_next = scores(r0 + sub) if r0 + sub < tm else None
            e = jnp.exp(sc - jnp.max(sc, axis=-1, keepdims=True))
            pr = e / jnp.sum(e, axis=-1, keepdims=True)
            a_ref[r0:r0 + sub, hcols] = _dot(pr.astype(BF16), v_ref[...]).astype(BF16)
            sc = sc_next

    @pl.when(j >= XA_HEADS)
    def _():
        cols = pl.ds(pl.multiple_of((j - XA_HEADS) * tn, tn), tn)
        o_ref[...] = x_ref[:, cols] + _dot(a_ref[...], wo_ref[...])


def cross_attention(x, g, w_xq, kv, w_xo, l, B, S, M, tm, tn):
    R, D = x.shape
    dh = D // XA_HEADS
    tiles_per_batch = S // tm
    last = XA_HEADS - 1
    head = lambda j: jnp.minimum(j, last)
    otile = lambda j: jnp.maximum(j - XA_HEADS, 0)
    vmem = (2 * tm * D * 4 + 2 * tm * D * 2 + 4 * D * dh * 2 + 4 * M * dh * 2 + 4 * D * tn * 2 + 3 * tm * tn * 4
            + 3 * tm * dh * 4 + 3 * tm * M * 4)
    return pl.pallas_call(
        _xattn_body,
        grid=(R // tm, XA_HEADS + D // tn),
        in_specs=[
            pl.BlockSpec((tm, D), lambda i, j: (i, 0)),
            pl.BlockSpec((None, 1, D), lambda i, j: (l, 0, 0)),
            pl.BlockSpec((None, D, dh), lambda i, j: (l, 0, head(j))),
            pl.BlockSpec((None, M, dh), lambda i, j: (l, i // tiles_per_batch, head(j))),
            pl.BlockSpec((None, M, dh), lambda i, j: (l, i // tiles_per_batch, XA_HEADS + head(j))),
            pl.BlockSpec((None, D, tn), lambda i, j: (l, 0, otile(j))),
        ],
        out_specs=pl.BlockSpec((tm, tn), lambda i, j: (i, otile(j))),
        out_shape=jax.ShapeDtypeStruct((R, D), F32),
        scratch_shapes=[pltpu.VMEM((tm, D), BF16), pltpu.VMEM((tm, D), BF16)],
        compiler_params=_compiler_params(2, vmem),
        name="cross_attention",
    )(x, g, w_xq, kv, kv, w_xo)


def _ffn_body(x_ref, g_ref, wg_ref, wu_ref, cw_ref, cb_ref, wd_ref, og_ref, o_ref, h_ref, gbuf, carry, *,
              tiles_per_seq, norm_output):
    i = pl.program_id(0)
    j = pl.program_id(1)
    tm = x_ref.shape[0]
    K = cw_ref.shape[0]
    halo = V7X_SUBLANES

    @pl.when(j == 0)
    def _():
        _norm_rows_into(h_ref, x_ref, g_ref)
        o_ref[...] = x_ref[...]

    gbuf[0:halo, :] = jnp.where((i % tiles_per_seq) == 0, 0.0, carry[j])
    sub = min(FFN_SUB_ROWS, tm)

    def gate_up(r0):
        h = h_ref[r0:r0 + sub, :]
        return _dot(h, wg_ref[...]), _dot(h, wu_ref[...])

    gate, up = gate_up(0)
    for r0 in range(0, tm, sub):
        nxt = gate_up(r0 + sub) if r0 + sub < tm else None
        gbuf[halo + r0:halo + r0 + sub, :] = gate
        conv = gate * cw_ref[K - 1:K, :] + cb_ref[...]
        for t in range(K - 2, -1, -1):
            conv = conv + gbuf[halo - K + 1 + t + r0:halo - K + 1 + t + r0 + sub, :] * cw_ref[t:t + 1, :]
        act = _silu(conv) * up
        o_ref[r0:r0 + sub, :] += _dot(act.astype(BF16), wd_ref[...])
        if nxt is not None:
            gate, up = nxt
    carry[j] = gbuf[tm:tm + halo, :]

    if norm_output:
        @pl.when(j == pl.num_programs(1) - 1)
        def _():
            _norm_rows_into(o_ref, o_ref, og_ref)


def conv_glu_ffn(x, g, w_gate, w_up, conv_w, conv_b, w_down, out_g, l, S, tm, tf, norm_output):
    R, D = x.shape
    F = w_gate.shape[2]
    K = conv_w.shape[1]
    nf = F // tf
    vmem = (4 * tm * D * 4 + tm * D * 2 + 2 * 3 * D * tf * 2 + (tm + V7X_SUBLANES) * tf * 4
            + nf * V7X_SUBLANES * tf * 4 + 4 * tm * tf * 4)
    return pl.pallas_call(
        functools.partial(_ffn_body, tiles_per_seq=S // tm, norm_output=norm_output),
        grid=(R // tm, nf),
        in_specs=[
            pl.BlockSpec((tm, D), lambda i, j: (i, 0)),
            pl.BlockSpec((None, 1, D), lambda i, j: (l, 0, 0)),
            pl.BlockSpec((None, D, tf), lambda i, j: (l, 0, j)),
            pl.BlockSpec((None, D, tf), lambda i, j: (l, 0, j)),
            pl.BlockSpec((None, K, tf), lambda i, j: (l, 0, j)),
            pl.BlockSpec((None, 1, tf), lambda i, j: (l, 0, j)),
            pl.BlockSpec((None, tf, D), lambda i, j: (l, j, 0)),
            pl.BlockSpec((1, D), lambda i, j: (0, 0)),
        ],
        out_specs=pl.BlockSpec((tm, D), lambda i, j: (i, 0)),
        out_shape=jax.ShapeDtypeStruct((R, D), F32),
        scratch_shapes=[
            pltpu.VMEM((tm, D), BF16),
            pltpu.VMEM((tm + V7X_SUBLANES, tf), F32),
            pltpu.VMEM((nf, V7X_SUBLANES, tf), F32),
        ],
        compiler_params=_compiler_params(2, vmem),
        name="conv_glu_ffn",
    )(x, g, w_gate, w_up, conv_w, conv_b, w_down, out_g)


def _tiles(S):
    return dict(tm=min(1024, S), ts_pool=min(512, S), ts_dn=min(512, S))


def kernel(x, mem, mix_norm_g, w_in, w_pool, pool_scale, dn_conv_w, dn_a_log, dn_dt_bias, dn_norm_g, w_mix_out,
           xa_norm_g, mem_norm_g, w_xq, w_xkv, w_xo, ffn_norm_g, w_gate, w_up, ffn_conv_w, ffn_conv_b, w_down,
           final_norm_g):
    B, S, D = x.shape
    M = mem.shape[1]
    depth = w_in.shape[0]
    H = dn_a_log.shape[1]
    dn_w = H * DN_HEAD_DIM
    pool_w = w_pool.shape[1] * w_pool.shape[2]
    main = pool_w + 4 * dn_w
    assert pool_w == dn_w and w_in.shape[2] == main + 2 * H and 2 * H <= V7X_LANES
    t = _tiles(S)

    tn = 512
    n_in = -(-w_in.shape[2] // tn) * tn
    w_in_b = jnp.pad(w_in.astype(BF16), ((0, 0), (0, 0), (0, n_in - w_in.shape[2])))
    pad_gate = lambda a: jnp.pad(a, ((0, 0), (H, V7X_LANES - 2 * H)))[:, None, :]
    a_log_p, dt_bias_p = pad_gate(dn_a_log), pad_gate(dn_dt_bias)
    w_pool_b = w_pool.astype(BF16)
    w_mix_b = w_mix_out.astype(BF16)
    w_xq_b, w_xkv_b, w_xo_b = w_xq.astype(BF16), w_xkv.astype(BF16), w_xo.astype(BF16)
    w_gate_b, w_up_b, w_down_b = w_gate.astype(BF16), w_up.astype(BF16), w_down.astype(BF16)
    row = lambda a: a[:, None, :]

    xr = x.reshape(B * S, D)
    kv = norm_matmul(mem.reshape(B * M, D), mem_norm_g[None, :], w_xkv_b, BF16, tm=min(1024, B * M), tn=tn)
    for l in range(depth):
        p = in_proj(xr, mix_norm_g[l][None, :], w_in_b, dn_conv_w, l, S, tm=t["tm"], tn=tn, conv_col0=pool_w)
        y_pool = pool_mixer(p, w_pool_b, row(pool_scale), l, B, S, t["ts_pool"])
        y_dn = gated_delta_net(p, a_log_p, dt_bias_p, row(dn_norm_g), l, B, S, t["ts_dn"],
                               col_block0=pool_w // dn_w, H=H)
        xr = mix_out(xr, y_pool, y_dn, w_mix_b, l, tm=t["tm"], tn=1024)
        xr = cross_attention(xr, row(xa_norm_g), w_xq_b, kv, w_xo_b, l, B, S, M, tm=t["tm"], tn=tn)
        xr = conv_glu_ffn(xr, row(ffn_norm_g), w_gate_b, w_up_b, ffn_conv_w, row(ffn_conv_b), w_down_b,
                          final_norm_g[None, :], l, S, tm=t["tm"], tf=tn, norm_output=(l == depth - 1))
    return xr.reshape(B, S, D)
```

```python
import functools

import jax
import jax.numpy as jnp
from jax import lax
from jax.experimental import pallas as pl
from jax.experimental.pallas import tpu as pltpu

F32 = jnp.float32
BF16 = jnp.bfloat16

EPS = 1e-6
POOL_WINDOWS = (2, 4, 8, 16)
POOL_HALO = 16
DN_HEAD_DIM = 128
DN_CHUNK = 128
DN_HEADS_PER_STEP = 2
XA_HEADS = 4

V7X_VMEM_BYTES = 64 * 1024 * 1024
V7X_LANES = 128
V7X_SUBLANES = 8
NORM_ROWS = 128
XA_SUB_ROWS = 512
FFN_SUB_ROWS = 512
IN_PROJ_SUB_ROWS = 128


def _compiler_params(n_axes, vmem_bytes):
    limit = min(int(vmem_bytes * 1.25) + (4 << 20), V7X_VMEM_BYTES * 7 // 8)
    return pltpu.CompilerParams(dimension_semantics=("arbitrary",) * n_axes, vmem_limit_bytes=limit)


def _dot(a, b):
    return jnp.dot(a, b, preferred_element_type=F32)


def _dot_nt(a, b):
    return lax.dot_general(a, b, (((1,), (1,)), ((), ())), preferred_element_type=F32)


def _silu(x):
    return x * (1.0 / (1.0 + jnp.exp(-x)))


def _rms_rows(x, g):
    ms = jnp.mean(x * x, axis=-1, keepdims=True)
    return x * lax.rsqrt(ms + EPS) * g


def _rows_shifted_down(a, prev, s):
    n = prev.shape[0]
    rows = lax.broadcasted_iota(jnp.int32, prev.shape, 0)
    rolled = [pltpu.roll(piece, s, axis=0) for piece in [prev] + [a[r:r + n, :] for r in range(0, a.shape[0], n)]]
    return jnp.concatenate([jnp.where(rows < s, lo, hi) for lo, hi in zip(rolled[:-1], rolled[1:])], axis=0)


def _norm_rows_into(h_ref, x_ref, g_ref):
    g = g_ref[...]

    def body(r, c):
        rows = pl.ds(pl.multiple_of(r * NORM_ROWS, NORM_ROWS), NORM_ROWS)
        h_ref[rows, :] = _rms_rows(x_ref[rows, :], g).astype(h_ref.dtype)
        return c

    lax.fori_loop(0, x_ref.shape[0] // NORM_ROWS, body, 0)


def _norm_matmul_body(x_ref, g_ref, w_ref, o_ref, h_ref):
    @pl.when((pl.program_id(1) == 0) & (pl.program_id(2) == 0))
    def _():
        _norm_rows_into(h_ref, x_ref, g_ref)

    o_ref[...] = _dot(h_ref[...], w_ref[...]).astype(o_ref.dtype)


def norm_matmul(x, g, w, out_dtype, tm, tn):
    R, K = x.shape
    L, _, N = w.shape
    osz = jnp.dtype(out_dtype).itemsize
    vmem = 2 * tm * K * 4 + tm * K * 2 + 2 * K * tn * 2 + 2 * tm * tn * osz + tm * tn * 4
    return pl.pallas_call(
        _norm_matmul_body,
        grid=(R // tm, L, N // tn),
        in_specs=[
            pl.BlockSpec((tm, K), lambda i, l, j: (i, 0)),
            pl.BlockSpec((1, K), lambda i, l, j: (0, 0)),
            pl.BlockSpec((None, K, tn), lambda i, l, j: (l, 0, j)),
        ],
        out_specs=pl.BlockSpec((None, tm, tn), lambda i, l, j: (l, i, j)),
        out_shape=jax.ShapeDtypeStruct((L, R, N), out_dtype),
        scratch_shapes=[pltpu.VMEM((tm, K), BF16)],
        compiler_params=_compiler_params(3, vmem),
        name="norm_matmul",
    )(x, g, w)


def _in_proj_body(x_ref, g_ref, w_ref, cw_ref, p_ref, h_ref, carry, *, tiles_per_seq, conv0, tiles_per_part):
    i = pl.program_id(0)
    j = pl.program_id(1)
    tm, tn = p_ref.shape
    K = cw_ref.shape[0]
    halo = V7X_SUBLANES

    @pl.when(j == 0)
    def _():
        _norm_rows_into(h_ref, x_ref, g_ref)

    is_conv = (j >= conv0) & (j < conv0 + 3 * tiles_per_part)

    @pl.when(jnp.logical_not(is_conv))
    def _():
        p_ref[...] = _dot(h_ref[...], w_ref[...])

    @pl.when(is_conv)
    def _():
        cj = j - conv0
        part = cj // tiles_per_part
        q_scale = jnp.where(part == 0, DN_HEAD_DIM ** -0.5, 1.0)
        prev = jnp.where((i % tiles_per_seq) == 0, 0.0, carry[cj])
        sub = min(IN_PROJ_SUB_ROWS, tm)
        for r0 in range(0, tm, sub):
            acc = _dot(h_ref[r0:r0 + sub, :], w_ref[...])
            y = acc * cw_ref[K - 1:K, :]
            for s in range(1, K):
                y = y + _rows_shifted_down(acc, prev, s) * cw_ref[K - 1 - s:K - s, :]
            prev = acc[sub - halo:sub, :]
            y = _silu(y)
            for c in range(0, tn, DN_HEAD_DIM):
                yh = y[:, c:c + DN_HEAD_DIM]
                inv = lax.rsqrt(jnp.sum(yh * yh, axis=-1, keepdims=True) + EPS) * q_scale
                p_ref[r0:r0 + sub, c:c + DN_HEAD_DIM] = yh * jnp.where(part == 2, 1.0, inv)
        carry[cj] = prev


def in_proj(x, g, w, conv_w, l, S, tm, tn, conv_col0):
    R, D = x.shape
    N = w.shape[2]
    K, conv_cols = conv_w.shape[1:]
    assert conv_col0 % tn == 0 and (conv_cols // 3) % tn == 0 and tn % DN_HEAD_DIM == 0
    conv0, n_conv = conv_col0 // tn, conv_cols // tn
    vmem = (2 * tm * D * 4 + tm * D * 2 + 2 * D * tn * 2 + 3 * tm * tn * 4 + (tm + V7X_SUBLANES) * tn * 4
            + n_conv * V7X_SUBLANES * tn * 4 + 3 * tm * tn * 4)
    return pl.pallas_call(
        functools.partial(_in_proj_body, tiles_per_seq=S // tm, conv0=conv0, tiles_per_part=n_conv // 3),
        grid=(R // tm, N // tn),
        in_specs=[
            pl.BlockSpec((tm, D), lambda i, j: (i, 0)),
            pl.BlockSpec((1, D), lambda i, j: (0, 0)),
            pl.BlockSpec((None, D, tn), lambda i, j: (l, 0, j)),
            pl.BlockSpec((None, K, tn), lambda i, j: (l, 0, jnp.clip(j - conv0, 0, n_conv - 1))),
        ],
        out_specs=pl.BlockSpec((tm, tn), lambda i, j: (i, j)),
        out_shape=jax.ShapeDtypeStruct((R, N), F32),
        scratch_shapes=[
            pltpu.VMEM((tm, D), BF16),
            pltpu.VMEM((n_conv, V7X_SUBLANES, tn), F32),
        ],
        compiler_params=_compiler_params(2, vmem),
        name="in_proj",
    )(x, g, w, conv_w)


def _pool_body(u_ref, wp_ref, sc_ref, o_ref, buf):
    s = pl.program_id(1)
    ts = u_ref.shape[0]
    G = wp_ref.shape[1]

    @pl.when(s == 0)
    def _():
        buf[0:POOL_HALO, :] = jnp.zeros((POOL_HALO, buf.shape[1]), F32)

    buf[POOL_HALO:POOL_HALO + ts, :] = u_ref[...]
    pos = (s * ts + 1 + lax.broadcasted_iota(jnp.int32, (ts, 1), 0)).astype(F32)
    for i, w in enumerate(POOL_WINDOWS):
        cols = slice(i * G, (i + 1) * G)
        u = buf[POOL_HALO:POOL_HALO + ts, cols]
        acc = u
        for k in range(1, w):
            acc = acc + buf[POOL_HALO - k:POOL_HALO - k + ts, cols]
        mixed = acc / jnp.minimum(pos, float(w)) - u
        y = _dot(mixed.astype(BF16), wp_ref[i]) * sc_ref[:, cols]
        o_ref[:, cols] = y.astype(o_ref.dtype)
    buf[0:POOL_HALO, :] = buf[ts:ts + POOL_HALO, :]


def pool_mixer(p, w_pool, pool_scale, l, B, S, ts):
    n_win, G, _ = w_pool.shape[1:]
    W = n_win * G
    nt = S // ts
    vmem = 2 * ts * W * 4 + (ts + POOL_HALO) * W * 4 + 2 * n_win * G * G * 2 + 2 * ts * W * 2 + 4 * ts * G * 4
    return pl.pallas_call(
        _pool_body,
        grid=(B, nt),
        in_specs=[
            pl.BlockSpec((ts, W), lambda b, s: (b * nt + s, 0)),
            pl.BlockSpec((None, n_win, G, G), lambda b, s: (l, 0, 0, 0)),
            pl.BlockSpec((None, 1, W), lambda b, s: (l, 0, 0)),
        ],
        out_specs=pl.BlockSpec((ts, W), lambda b, s: (b * nt + s, 0)),
        out_shape=jax.ShapeDtypeStruct((B * S, W), BF16),
        scratch_shapes=[pltpu.VMEM((ts + POOL_HALO, W), F32)],
        compiler_params=_compiler_params(2, vmem),
        name="pool_mixer",
    )(p, w_pool, pool_scale)


def _unit_lower_inverses(ls, eye, ii, jj):
    C = ls[0].shape[0]
    size = 2
    same = (ii // size) == (jj // size)
    xs = [eye - jnp.where(same, l_mat, 0.0) for l_mat in ls]
    while size < C:
        inner, same = same, (ii // (2 * size)) == (jj // (2 * size))
        offs = [jnp.where(same & jnp.logical_not(inner), l_mat, 0.0).astype(BF16) for l_mat in ls]
        xbs = [x.astype(BF16) for x in xs]
        ys = [_dot(off, xb) for off, xb in zip(offs, xbs)]
        xs = [x - _dot(xb, y.astype(BF16)) for x, xb, y in zip(xs, xbs, ys)]
        size *= 2
    return xs


def _dn_body(q_ref, k_ref, v_ref, z_ref, ba_ref, alog_ref, dtb_ref, ng_ref, o_ref,
             state, beta_rep, gc_rep, grow_s, cdec_s, u_s, w_s, qd_s, at_s, kdt_s):
    s = pl.program_id(1)
    ts, W = q_ref.shape
    H = W // DN_HEAD_DIM
    C = DN_CHUNK
    n_chunk = ts // C

    @pl.when(s == 0)
    def _():
        state[...] = jnp.zeros(state.shape, F32)

    ba = ba_ref[...]
    beta = 1.0 / (1.0 + jnp.exp(-ba))
    xs = ba + dtb_ref[...]
    softplus = jnp.maximum(xs, 0.0) + jnp.log1p(jnp.exp(-jnp.abs(xs)))
    g = -jnp.exp(alog_ref[...]) * softplus
    row_in_chunk = lax.broadcasted_iota(jnp.int32, (ts, V7X_LANES), 0) % C
    gc = g
    sh = 1
    while sh < C:
        gc = gc + jnp.where(row_in_chunk >= sh, pltpu.roll(gc, sh, axis=0), 0.0)
        sh *= 2
    gct = gc.T
    lane = lax.broadcasted_iota(jnp.int32, (ts, V7X_LANES), 1)
    for h in range(H):
        beta_rep[h] = jnp.broadcast_to(
            jnp.sum(jnp.where(lane == h, beta, 0.0), axis=-1, keepdims=True), (ts, V7X_LANES))
        gc_rep[h] = jnp.broadcast_to(
            jnp.sum(jnp.where(lane == H + h, gc, 0.0), axis=-1, keepdims=True), (ts, V7X_LANES))
        grow_s[h] = gct[H + h:H + h + 1, :]

    ii = lax.broadcasted_iota(jnp.int32, (C, C), 0)
    jj = lax.broadcasted_iota(jnp.int32, (C, C), 1)
    eye = (ii == jj).astype(F32)

    def head_group_body(hg, carry):
        pairs = []
        for hh in range(DN_HEADS_PER_STEP):
            h = hg * DN_HEADS_PER_STEP + hh
            cols = pl.ds(pl.multiple_of(h * DN_HEAD_DIM, DN_HEAD_DIM), DN_HEAD_DIM)
            bcol, gcol, grow_all = beta_rep[h], gc_rep[h], grow_s[h]
            egc = jnp.exp(gcol)
            for c in range(n_chunk):
                r = slice(c * C, (c + 1) * C)
                pairs.append(dict(h=h, c=c, r=r, cols=cols, q=q_ref[r, cols], k=k_ref[r, cols], v=v_ref[r, cols],
                                  b=bcol[r], g=gcol[r], e=egc[r], grow=grow_all[:, r]))
        kbs = [p["k"].astype(BF16) for p in pairs]
        decs = [jnp.exp(jnp.minimum(p["g"] - p["grow"], 0.0)) for p in pairs]
        kks = [_dot_nt(kb, kb) for kb in kbs]
        qks = [_dot_nt(p["q"].astype(BF16), kb) for p, kb in zip(pairs, kbs)]
        ls = [jnp.where(ii > jj, kk * dec, 0.0) * p["b"] for kk, dec, p in zip(kks, decs, pairs)]
        for p, qk, dec in zip(pairs, qks, decs):
            r, cols = p["r"], p["cols"]
            at_s[r, cols] = jnp.where(ii >= jj, qk * dec, 0.0).astype(BF16)
            qd_s[r, cols] = (p["q"] * p["e"]).astype(BF16)
            kd = p["k"] * jnp.exp(p["g"][C - 1:C, :] - p["g"])
            kdt_s[r, cols] = kd.T.astype(BF16)
            cdec_s[p["c"], p["h"]] = p["e"][C - 1:C, :]
        t_invs = _unit_lower_inverses(ls, eye, ii, jj)
        rhs = [jnp.concatenate([p["v"] * p["b"], p["k"] * (p["b"] * p["e"])], axis=1) for p in pairs]
        sols = [_dot(t_inv.astype(BF16), b.astype(BF16)) for t_inv, b in zip(t_invs, rhs)]
        for p, sol in zip(pairs, sols):
            u_s[p["r"], p["cols"]] = sol[:, :DN_HEAD_DIM]
            w_s[p["r"], p["cols"]] = sol[:, DN_HEAD_DIM:].astype(BF16)
        return carry

    lax.fori_loop(0, H // DN_HEADS_PER_STEP, head_group_body, 0)

    def chunk_body(c, carry):
        r0 = pl.multiple_of(c * C, C)
        rows = pl.ds(r0, C)
        hs = range(H)
        cols = [slice(h * DN_HEAD_DIM, (h + 1) * DN_HEAD_DIM) for h in hs]
        sts = [state[h] for h in hs]
        sbs = [st.astype(BF16) for st in sts]
        wss = [_dot(w_s[rows, cols[h]], sbs[h]) for h in hs]
        qss = [_dot(qd_s[rows, cols[h]], sbs[h]) for h in hs]
        vbs = [(u_s[rows, cols[h]] - wss[h]).astype(BF16) for h in hs]
        avs = [_dot(at_s[rows, cols[h]], vbs[h]) for h in hs]
        kvs = [_dot(kdt_s[rows, cols[h]], vbs[h]) for h in hs]
        for h in hs:
            state[h] = sts[h] * cdec_s[c, h] + kvs[h]
            gated = _rms_rows(qss[h] + avs[h], ng_ref[...]) * _silu(z_ref[rows, cols[h]])
            o_ref[rows, cols[h]] = gated.astype(o_ref.dtype)
        return carry

    lax.fori_loop(0, n_chunk, chunk_body, 0)


def gated_delta_net(p, a_log, dt_bias, norm_g, l, B, S, ts, col_block0, H):
    W = H * DN_HEAD_DIM
    nt = S // ts
    NB = V7X_LANES
    gate_block = (col_block0 + 4) * (W // NB)
    C = DN_CHUNK
    vmem = (2 * 4 * ts * W * 4 + 2 * ts * NB * 4 + 2 * ts * W * 2
            + H * DN_HEAD_DIM * DN_HEAD_DIM * 4 + 2 * H * ts * V7X_LANES * 4 + H * V7X_SUBLANES * ts * 4
            + ts * W * 4 + 4 * ts * W * 2 + 48 * C * C * 4 * DN_HEADS_PER_STEP)

    def pblock(k):
        return pl.BlockSpec((ts, W), lambda b, s: (b * nt + s, col_block0 + k))

    return pl.pallas_call(
        _dn_body,
        grid=(B, nt),
        in_specs=[
            pblock(0), pblock(1), pblock(2), pblock(3),
            pl.BlockSpec((ts, NB), lambda b, s: (b * nt + s, gate_block)),
            pl.BlockSpec((None, 1, NB), lambda b, s: (l, 0, 0)),
            pl.BlockSpec((None, 1, NB), lambda b, s: (l, 0, 0)),
            pl.BlockSpec((None, 1, DN_HEAD_DIM), lambda b, s: (l, 0, 0)),
        ],
        out_specs=pl.BlockSpec((ts, W), lambda b, s: (b * nt + s, 0)),
        out_shape=jax.ShapeDtypeStruct((B * S, W), BF16),
        scratch_shapes=[
            pltpu.VMEM((H, DN_HEAD_DIM, DN_HEAD_DIM), F32),
            pltpu.VMEM((H, ts, V7X_LANES), F32),
            pltpu.VMEM((H, ts, V7X_LANES), F32),
            pltpu.VMEM((H, 1, ts), F32),
            pltpu.VMEM((ts // C, H, 1, V7X_LANES), F32),
            pltpu.VMEM((ts, W), F32),
            pltpu.VMEM((ts, W), BF16),
            pltpu.VMEM((ts, W), BF16),
            pltpu.VMEM((ts, W), BF16),
            pltpu.VMEM((ts, W), BF16),
        ],
        compiler_params=_compiler_params(2, vmem),
        name="gated_delta_net",
    )(p, p, p, p, p, a_log, dt_bias, norm_g)


def _mix_out_body(x_ref, yp_ref, yd_ref, w1_ref, w2_ref, o_ref):
    o_ref[...] = x_ref[...] + _dot(yp_ref[...], w1_ref[...]) + _dot(yd_ref[...], w2_ref[...])


def mix_out(x, y_pool, y_dn, w, l, tm, tn):
    R, D = x.shape
    K1, K2 = y_pool.shape[1], y_dn.shape[1]
    assert K1 == K2
    vmem = 2 * (2 * tm * tn * 4 + tm * (K1 + K2) * 2 + (K1 + K2) * tn * 2) + 2 * tm * tn * 4
    return pl.pallas_call(
        _mix_out_body,
        grid=(R // tm, D // tn),
        in_specs=[
            pl.BlockSpec((tm, tn), lambda i, j: (i, j)),
            pl.BlockSpec((tm, K1), lambda i, j: (i, 0)),
            pl.BlockSpec((tm, K2), lambda i, j: (i, 0)),
            pl.BlockSpec((None, K1, tn), lambda i, j: (l, 0, j)),
            pl.BlockSpec((None, K2, tn), lambda i, j: (l, 1, j)),
        ],
        out_specs=pl.BlockSpec((tm, tn), lambda i, j: (i, j)),
        out_shape=jax.ShapeDtypeStruct((R, D), F32),
        compiler_params=_compiler_params(2, vmem),
        name="mix_out",
    )(x, y_pool, y_dn, w, w)


def _xattn_body(x_ref, g_ref, wq_ref, k_ref, v_ref, wo_ref, o_ref, h_ref, a_ref):
    j = pl.program_id(1)
    dh = wq_ref.shape[1]
    tn = wo_ref.shape[1]

    @pl.when(j == 0)
    def _():
        _norm_rows_into(h_ref, x_ref, g_ref)

    @pl.when(j < XA_HEADS)
    def _():
        hcols = pl.ds(pl.multiple_of(j * dh, dh), dh)
        tm = h_ref.shape[0]
        sub = min(XA_SUB_ROWS, tm)

        def scores(r0):
            q = _dot(h_ref[r0:r0 + sub, :], wq_ref[...])
            return _dot_nt(q.astype(BF16), k_ref[...]) * (dh ** -0.5)

        sc = scores(0)
        for r0 in range(0, tm, sub):
            sc_next = scores(r0 + sub) if r0 + sub < tm else None
            e = jnp.exp(sc - jnp.max(sc, axis=-1, keepdims=True))
            pr = e / jnp.sum(e, axis=-1, keepdims=True)
            a_ref[r0:r0 + sub, hcols] = _dot(pr.astype(BF16), v_ref[...]).astype(BF16)
            sc = sc_next

    @pl.when(j >= XA_HEADS)
    def _():
        cols = pl.ds(pl.multiple_of((j - XA_HEADS) * tn, tn), tn)
        o_ref[...] = x_ref[:, cols] + _dot(a_ref[...], wo_ref[...])


def cross_attention(x, g, w_xq, kv, w_xo, l, B, S, M, tm, tn):
    R, D = x.shape
    dh = D // XA_HEADS
    tiles_per_batch = S // tm
    last = XA_HEADS - 1
    head = lambda j: jnp.minimum(j, last)
    otile = lambda j: jnp.maximum(j - XA_HEADS, 0)
    vmem = (2 * tm * D * 4 + 2 * tm * D * 2 + 4 * D * dh * 2 + 4 * M * dh * 2 + 4 * D * tn * 2 + 3 * tm * tn * 4
            + 3 * tm * dh * 4 + 3 * tm * M * 4)
    return pl.pallas_call(
        _xattn_body,
        grid=(R // tm, XA_HEADS + D // tn),
        in_specs=[
            pl.BlockSpec((tm, D), lambda i, j: (i, 0)),
            pl.BlockSpec((None, 1, D), lambda i, j: (l, 0, 0)),
            pl.BlockSpec((None, D, dh), lambda i, j: (l, 0, head(j))),
            pl.BlockSpec((None, M, dh), lambda i, j: (l, i // tiles_per_batch, head(j))),
            pl.BlockSpec((None, M, dh), lambda i, j: (l, i // tiles_per_batch, XA_HEADS + head(j))),
            pl.BlockSpec((None, D, tn), lambda i, j: (l, 0, otile(j))),
        ],
        out_specs=pl.BlockSpec((tm, tn), lambda i, j: (i, otile(j))),
        out_shape=jax.ShapeDtypeStruct((R, D), F32),
        scratch_shapes=[pltpu.VMEM((tm, D), BF16), pltpu.VMEM((tm, D), BF16)],
        compiler_params=_compiler_params(2, vmem),
        name="cross_attention",
    )(x, g, w_xq, kv, kv, w_xo)


def _ffn_body(x_ref, g_ref, wg_ref, wu_ref, cw_ref, cb_ref, wd_ref, og_ref, o_ref, h_ref, gbuf, carry, *,
              tiles_per_seq, norm_output):
    i = pl.program_id(0)
    j = pl.program_id(1)
    tm = x_ref.shape[0]
    K = cw_ref.shape[0]
    halo = V7X_SUBLANES

    @pl.when(j == 0)
    def _():
        _norm_rows_into(h_ref, x_ref, g_ref)
        o_ref[...] = x_ref[...]

    gbuf[0:halo, :] = jnp.where((i % tiles_per_seq) == 0, 0.0, carry[j])
    sub = min(FFN_SUB_ROWS, tm)

    def gate_up(r0):
        h = h_ref[r0:r0 + sub, :]
        return _dot(h, wg_ref[...]), _dot(h, wu_ref[...])

    gate, up = gate_up(0)
    for r0 in range(0, tm, sub):
        nxt = gate_up(r0 + sub) if r0 + sub < tm else None
        gbuf[halo + r0:halo + r0 + sub, :] = gate
        conv = gate * cw_ref[K - 1:K, :] + cb_ref[...]
        for t in range(K - 2, -1, -1):
            conv = conv + gbuf[halo - K + 1 + t + r0:halo - K + 1 + t + r0 + sub, :] * cw_ref[t:t + 1, :]
        act = _silu(conv) * up
        o_ref[r0:r0 + sub, :] += _dot(act.astype(BF16), wd_ref[...])
        if nxt is not None:
            gate, up = nxt
    carry[j] = gbuf[tm:tm + halo, :]

    if norm_output:
        @pl.when(j == pl.num_programs(1) - 1)
        def _():
            _norm_rows_into(o_ref, o_ref, og_ref)


def conv_glu_ffn(x, g, w_gate, w_up, conv_w, conv_b, w_down, out_g, l, S, tm, tf, norm_output):
    R, D = x.shape
    F = w_gate.shape[2]
    K = conv_w.shape[1]
    nf = F // tf
    vmem = (4 * tm * D * 4 + tm * D * 2 + 2 * 3 * D * tf * 2 + (tm + V7X_SUBLANES) * tf * 4
            + nf * V7X_SUBLANES * tf * 4 + 4 * tm * tf * 4)
    return pl.pallas_call(
        functools.partial(_ffn_body, tiles_per_seq=S // tm, norm_output=norm_output),
        grid=(R // tm, nf),
        in_specs=[
            pl.BlockSpec((tm, D), lambda i, j: (i, 0)),
            pl.BlockSpec((None, 1, D), lambda i, j: (l, 0, 0)),
            pl.BlockSpec((None, D, tf), lambda i, j: (l, 0, j)),
            pl.BlockSpec((None, D, tf), lambda i, j: (l, 0, j)),
            pl.BlockSpec((None, K, tf), lambda i, j: (l, 0, j)),
            pl.BlockSpec((None, 1, tf), lambda i, j: (l, 0, j)),
            pl.BlockSpec((None, tf, D), lambda i, j: (l, j, 0)),
            pl.BlockSpec((1, D), lambda i, j: (0, 0)),
        ],
        out_specs=pl.BlockSpec((tm, D), lambda i, j: (i, 0)),
        out_shape=jax.ShapeDtypeStruct((R, D), F32),
        scratch_shapes=[
            pltpu.VMEM((tm, D), BF16),
            pltpu.VMEM((tm + V7X_SUBLANES, tf), F32),
            pltpu.VMEM((nf, V7X_SUBLANES, tf), F32),
        ],
        compiler_params=_compiler_params(2, vmem),
        name="conv_glu_ffn",
    )(x, g, w_gate, w_up, conv_w, conv_b, w_down, out_g)


def _tiles(S):
    return dict(tm=min(1024, S), ts_pool=min(512, S), ts_dn=min(512, S))


def kernel(x, mem, mix_norm_g, w_in, w_pool, pool_scale, dn_conv_w, dn_a_log, dn_dt_bias, dn_norm_g, w_mix_out,
           xa_norm_g, mem_norm_g, w_xq, w_xkv, w_xo, ffn_norm_g, w_gate, w_up, ffn_conv_w, ffn_conv_b, w_down,
           final_norm_g):
    B, S, D = x.shape
    M = mem.shape[1]
    depth = w_in.shape[0]
    H = dn_a_log.shape[1]
    dn_w = H * DN_HEAD_DIM
    pool_w = w_pool.shape[1] * w_pool.shape[2]
    main = pool_w + 4 * dn_w
    assert pool_w == dn_w and w_in.shape[2] == main + 2 * H and 2 * H <= V7X_LANES
    t = _tiles(S)

    tn = 512
    n_in = -(-w_in.shape[2] // tn) * tn
    w_in_b = jnp.pad(w_in.astype(BF16), ((0, 0), (0, 0), (0, n_in - w_in.shape[2])))
    pad_gate = lambda a: jnp.pad(a, ((0, 0), (H, V7X_LANES - 2 * H)))[:, None, :]
    a_log_p, dt_bias_p = pad_gate(dn_a_log), pad_gate(dn_dt_bias)
    w_pool_b = w_pool.astype(BF16)
    w_mix_b = w_mix_out.astype(BF16)
    w_xq_b, w_xkv_b, w_xo_b = w_xq.astype(BF16), w_xkv.astype(BF16), w_xo.astype(BF16)
    w_gate_b, w_up_b, w_down_b = w_gate.astype(BF16), w_up.astype(BF16), w_down.astype(BF16)
    row = lambda a: a[:, None, :]

    xr = x.reshape(B * S, D)
    kv = norm_matmul(mem.reshape(B * M, D), mem_norm_g[None, :], w_xkv_b, BF16, tm=min(1024, B * M), tn=tn)
    for l in range(depth):
        p = in_proj(xr, mix_norm_g[l][None, :], w_in_b, dn_conv_w, l, S, tm=t["tm"], tn=tn, conv_col0=pool_w)
        y_pool = pool_mixer(p, w_pool_b, row(pool_scale), l, B, S, t["ts_pool"])
        y_dn = gated_delta_net(p, a_log_p, dt_bias_p, row(dn_norm_g), l, B, S, t["ts_dn"],
                               col_block0=pool_w // dn_w, H=H)
        xr = mix_out(xr, y_pool, y_dn, w_mix_b, l, tm=t["tm"], tn=1024)
        xr = cross_attention(xr, row(xa_norm_g), w_xq_b, kv, w_xo_b, l, B, S, M, tm=t["tm"], tn=2 * tn)
        xr = conv_glu_ffn(xr, row(ffn_norm_g), w_gate_b, w_up_b, ffn_conv_w, row(ffn_conv_b), w_down_b,
                          final_norm_g[None, :], l, S, tm=t["tm"], tf=tn, norm_output=(l == depth - 1))
    return xr.reshape(B, S, D)
```

```python
import functools

import jax
import jax.numpy as jnp
from jax import lax
from jax.experimental import pallas as pl
from jax.experimental.pallas import tpu as pltpu

F32 = jnp.float32
BF16 = jnp.bfloat16

EPS = 1e-6
POOL_WINDOWS = (2, 4, 8, 16)
POOL_HALO = 16
DN_HEAD_DIM = 128
DN_CHUNK = 128
DN_HEADS_PER_STEP = 2
XA_HEADS = 4
XA_HEADS_PER_STEP = 2

V7X_VMEM_BYTES = 64 * 1024 * 1024
V7X_LANES = 128
V7X_SUBLANES = 8
NORM_ROWS = 128
XA_SUB_ROWS = 512
FFN_SUB_ROWS = 512
IN_PROJ_SUB_ROWS = 128


def _compiler_params(n_axes, vmem_bytes):
    limit = min(int(vmem_bytes * 1.25) + (4 << 20), V7X_VMEM_BYTES * 7 // 8)
    return pltpu.CompilerParams(dimension_semantics=("arbitrary",) * n_axes, vmem_limit_bytes=limit)


def _dot(a, b):
    return jnp.dot(a, b, preferred_element_type=F32)


def _dot_nt(a, b):
    return lax.dot_general(a, b, (((1,), (1,)), ((), ())), preferred_element_type=F32)


def _silu(x):
    return x * (1.0 / (1.0 + jnp.exp(-x)))


def _rms_rows(x, g):
    ms = jnp.mean(x * x, axis=-1, keepdims=True)
    return x * lax.rsqrt(ms + EPS) * g


def _rows_shifted_down(a, prev, s):
    n = prev.shape[0]
    rows = lax.broadcasted_iota(jnp.int32, prev.shape, 0)
    rolled = [pltpu.roll(piece, s, axis=0) for piece in [prev] + [a[r:r + n, :] for r in range(0, a.shape[0], n)]]
    return jnp.concatenate([jnp.where(rows < s, lo, hi) for lo, hi in zip(rolled[:-1], rolled[1:])], axis=0)


def _norm_rows_into(h_ref, x_ref, g_ref):
    g = g_ref[...]

    def body(r, c):
        rows = pl.ds(pl.multiple_of(r * NORM_ROWS, NORM_ROWS), NORM_ROWS)
        h_ref[rows, :] = _rms_rows(x_ref[rows, :], g).astype(h_ref.dtype)
        return c

    lax.fori_loop(0, x_ref.shape[0] // NORM_ROWS, body, 0)


def _norm_matmul_body(x_ref, g_ref, w_ref, o_ref, h_ref):
    @pl.when((pl.program_id(1) == 0) & (pl.program_id(2) == 0))
    def _():
        _norm_rows_into(h_ref, x_ref, g_ref)

    o_ref[...] = _dot(h_ref[...], w_ref[...]).astype(o_ref.dtype)


def norm_matmul(x, g, w, out_dtype, tm, tn):
    R, K = x.shape
    L, _, N = w.shape
    osz = jnp.dtype(out_dtype).itemsize
    vmem = 2 * tm * K * 4 + tm * K * 2 + 2 * K * tn * 2 + 2 * tm * tn * osz + tm * tn * 4
    return pl.pallas_call(
        _norm_matmul_body,
        grid=(R // tm, L, N // tn),
        in_specs=[
            pl.BlockSpec((tm, K), lambda i, l, j: (i, 0)),
            pl.BlockSpec((1, K), lambda i, l, j: (0, 0)),
            pl.BlockSpec((None, K, tn), lambda i, l, j: (l, 0, j)),
        ],
        out_specs=pl.BlockSpec((None, tm, tn), lambda i, l, j: (l, i, j)),
        out_shape=jax.ShapeDtypeStruct((L, R, N), out_dtype),
        scratch_shapes=[pltpu.VMEM((tm, K), BF16)],
        compiler_params=_compiler_params(3, vmem),
        name="norm_matmul",
    )(x, g, w)


def _in_proj_body(x_ref, g_ref, w_ref, cw_ref, p_ref, h_ref, carry, *, tiles_per_seq, conv0, tiles_per_part):
    i = pl.program_id(0)
    j = pl.program_id(1)
    tm, tn = p_ref.shape
    K = cw_ref.shape[0]
    halo = V7X_SUBLANES

    @pl.when(j == 0)
    def _():
        _norm_rows_into(h_ref, x_ref, g_ref)

    is_conv = (j >= conv0) & (j < conv0 + 3 * tiles_per_part)

    @pl.when(jnp.logical_not(is_conv))
    def _():
        p_ref[...] = _dot(h_ref[...], w_ref[...])

    @pl.when(is_conv)
    def _():
        cj = j - conv0
        part = cj // tiles_per_part
        q_scale = jnp.where(part == 0, DN_HEAD_DIM ** -0.5, 1.0)
        prev = jnp.where((i % tiles_per_seq) == 0, 0.0, carry[cj])
        sub = min(IN_PROJ_SUB_ROWS, tm)
        for r0 in range(0, tm, sub):
            acc = _dot(h_ref[r0:r0 + sub, :], w_ref[...])
            y = acc * cw_ref[K - 1:K, :]
            for s in range(1, K):
                y = y + _rows_shifted_down(acc, prev, s) * cw_ref[K - 1 - s:K - s, :]
            prev = acc[sub - halo:sub, :]
            y = _silu(y)
            for c in range(0, tn, DN_HEAD_DIM):
                yh = y[:, c:c + DN_HEAD_DIM]
                inv = lax.rsqrt(jnp.sum(yh * yh, axis=-1, keepdims=True) + EPS) * q_scale
                p_ref[r0:r0 + sub, c:c + DN_HEAD_DIM] = yh * jnp.where(part == 2, 1.0, inv)
        carry[cj] = prev


def in_proj(x, g, w, conv_w, l, S, tm, tn, conv_col0):
    R, D = x.shape
    N = w.shape[2]
    K, conv_cols = conv_w.shape[1:]
    assert conv_col0 % tn == 0 and (conv_cols // 3) % tn == 0 and tn % DN_HEAD_DIM == 0
    conv0, n_conv = conv_col0 // tn, conv_cols // tn
    vmem = (2 * tm * D * 4 + tm * D * 2 + 2 * D * tn * 2 + 3 * tm * tn * 4 + (tm + V7X_SUBLANES) * tn * 4
            + n_conv * V7X_SUBLANES * tn * 4 + 3 * tm * tn * 4)
    return pl.pallas_call(
        functools.partial(_in_proj_body, tiles_per_seq=S // tm, conv0=conv0, tiles_per_part=n_conv // 3),
        grid=(R // tm, N // tn),
        in_specs=[
            pl.BlockSpec((tm, D), lambda i, j: (i, 0)),
            pl.BlockSpec((1, D), lambda i, j: (0, 0)),
            pl.BlockSpec((None, D, tn), lambda i, j: (l, 0, j)),
            pl.BlockSpec((None, K, tn), lambda i, j: (l, 0, jnp.clip(j - conv0, 0, n_conv - 1))),
        ],
        out_specs=pl.BlockSpec((tm, tn), lambda i, j: (i, j)),
        out_shape=jax.ShapeDtypeStruct((R, N), F32),
        scratch_shapes=[
            pltpu.VMEM((tm, D), BF16),
            pltpu.VMEM((n_conv, V7X_SUBLANES, tn), F32),
        ],
        compiler_params=_compiler_params(2, vmem),
        name="in_proj",
    )(x, g, w, conv_w)


def _pool_body(u_ref, wp_ref, sc_ref, o_ref, buf):
    s = pl.program_id(1)
    ts = u_ref.shape[0]
    G = wp_ref.shape[1]

    @pl.when(s == 0)
    def _():
        buf[0:POOL_HALO, :] = jnp.zeros((POOL_HALO, buf.shape[1]), F32)

    buf[POOL_HALO:POOL_HALO + ts, :] = u_ref[...]
    pos = (s * ts + 1 + lax.broadcasted_iota(jnp.int32, (ts, 1), 0)).astype(F32)
    for i, w in enumerate(POOL_WINDOWS):
        cols = slice(i * G, (i + 1) * G)
        u = buf[POOL_HALO:POOL_HALO + ts, cols]
        acc = u
        for k in range(1, w):
            acc = acc + buf[POOL_HALO - k:POOL_HALO - k + ts, cols]
        mixed = acc / jnp.minimum(pos, float(w)) - u
        y = _dot(mixed.astype(BF16), wp_ref[i]) * sc_ref[:, cols]
        o_ref[:, cols] = y.astype(o_ref.dtype)
    buf[0:POOL_HALO, :] = buf[ts:ts + POOL_HALO, :]


def pool_mixer(p, w_pool, pool_scale, l, B, S, ts):
    n_win, G, _ = w_pool.shape[1:]
    W = n_win * G
    nt = S // ts
    vmem = 2 * ts * W * 4 + (ts + POOL_HALO) * W * 4 + 2 * n_win * G * G * 2 + 2 * ts * W * 2 + 4 * ts * G * 4
    return pl.pallas_call(
        _pool_body,
        grid=(B, nt),
        in_specs=[
            pl.BlockSpec((ts, W), lambda b, s: (b * nt + s, 0)),
            pl.BlockSpec((None, n_win, G, G), lambda b, s: (l, 0, 0, 0)),
            pl.BlockSpec((None, 1, W), lambda b, s: (l, 0, 0)),
        ],
        out_specs=pl.BlockSpec((ts, W), lambda b, s: (b * nt + s, 0)),
        out_shape=jax.ShapeDtypeStruct((B * S, W), BF16),
        scratch_shapes=[pltpu.VMEM((ts + POOL_HALO, W), F32)],
        compiler_params=_compiler_params(2, vmem),
        name="pool_mixer",
    )(p, w_pool, pool_scale)


def _unit_lower_inverses(ls, eye, ii, jj):
    C = ls[0].shape[0]
    size = 2
    same = (ii // size) == (jj // size)
    xs = [eye - jnp.where(same, l_mat, 0.0) for l_mat in ls]
    while size < C:
        inner, same = same, (ii // (2 * size)) == (jj // (2 * size))
        offs = [jnp.where(same & jnp.logical_not(inner), l_mat, 0.0).astype(BF16) for l_mat in ls]
        xbs = [x.astype(BF16) for x in xs]
        ys = [_dot(off, xb) for off, xb in zip(offs, xbs)]
        xs = [x - _dot(xb, y.astype(BF16)) for x, xb, y in zip(xs, xbs, ys)]
        size *= 2
    return xs


def _dn_body(q_ref, k_ref, v_ref, z_ref, ba_ref, alog_ref, dtb_ref, ng_ref, o_ref,
             state, beta_rep, gc_rep, grow_s, cdec_s, u_s, w_s, qd_s, at_s, kdt_s):
    s = pl.program_id(1)
    ts, W = q_ref.shape
    H = W // DN_HEAD_DIM
    C = DN_CHUNK
    n_chunk = ts // C

    @pl.when(s == 0)
    def _():
        state[...] = jnp.zeros(state.shape, F32)

    ba = ba_ref[...]
    beta = 1.0 / (1.0 + jnp.exp(-ba))
    xs = ba + dtb_ref[...]
    softplus = jnp.maximum(xs, 0.0) + jnp.log1p(jnp.exp(-jnp.abs(xs)))
    g = -jnp.exp(alog_ref[...]) * softplus
    row_in_chunk = lax.broadcasted_iota(jnp.int32, (ts, V7X_LANES), 0) % C
    gc = g
    sh = 1
    while sh < C:
        gc = gc + jnp.where(row_in_chunk >= sh, pltpu.roll(gc, sh, axis=0), 0.0)
        sh *= 2
    gct = gc.T
    lane = lax.broadcasted_iota(jnp.int32, (ts, V7X_LANES), 1)
    for h in range(H):
        beta_rep[h] = jnp.broadcast_to(
            jnp.sum(jnp.where(lane == h, beta, 0.0), axis=-1, keepdims=True), (ts, V7X_LANES))
        gc_rep[h] = jnp.broadcast_to(
            jnp.sum(jnp.where(lane == H + h, gc, 0.0), axis=-1, keepdims=True), (ts, V7X_LANES))
        grow_s[h] = gct[H + h:H + h + 1, :]

    ii = lax.broadcasted_iota(jnp.int32, (C, C), 0)
    jj = lax.broadcasted_iota(jnp.int32, (C, C), 1)
    eye = (ii == jj).astype(F32)

    def head_group_body(hg, carry):
        pairs = []
        for hh in range(DN_HEADS_PER_STEP):
            h = hg * DN_HEADS_PER_STEP + hh
            cols = pl.ds(pl.multiple_of(h * DN_HEAD_DIM, DN_HEAD_DIM), DN_HEAD_DIM)
            bcol, gcol, grow_all = beta_rep[h], gc_rep[h], grow_s[h]
            egc = jnp.exp(gcol)
            for c in range(n_chunk):
                r = slice(c * C, (c + 1) * C)
                pairs.append(dict(h=h, c=c, r=r, cols=cols, q=q_ref[r, cols], k=k_ref[r, cols], v=v_ref[r, cols],
                                  b=bcol[r], g=gcol[r], e=egc[r], grow=grow_all[:, r]))
        kbs = [p["k"].astype(BF16) for p in pairs]
        decs = [jnp.exp(jnp.minimum(p["g"] - p["grow"], 0.0)) for p in pairs]
        kks = [_dot_nt(kb, kb) for kb in kbs]
        qks = [_dot_nt(p["q"].astype(BF16), kb) for p, kb in zip(pairs, kbs)]
        ls = [jnp.where(ii > jj, kk * dec, 0.0) * p["b"] for kk, dec, p in zip(kks, decs, pairs)]
        for p, qk, dec in zip(pairs, qks, decs):
            r, cols = p["r"], p["cols"]
            at_s[r, cols] = jnp.where(ii >= jj, qk * dec, 0.0).astype(BF16)
            qd_s[r, cols] = (p["q"] * p["e"]).astype(BF16)
            kd = p["k"] * jnp.exp(p["g"][C - 1:C, :] - p["g"])
            kdt_s[r, cols] = kd.T.astype(BF16)
            cdec_s[p["c"], p["h"]] = p["e"][C - 1:C, :]
        t_invs = _unit_lower_inverses(ls, eye, ii, jj)
        rhs = [jnp.concatenate([p["v"] * p["b"], p["k"] * (p["b"] * p["e"])], axis=1) for p in pairs]
        sols = [_dot(t_inv.astype(BF16), b.astype(BF16)) for t_inv, b in zip(t_invs, rhs)]
        for p, sol in zip(pairs, sols):
            u_s[p["r"], p["cols"]] = sol[:, :DN_HEAD_DIM]
            w_s[p["r"], p["cols"]] = sol[:, DN_HEAD_DIM:].astype(BF16)
        return carry

    lax.fori_loop(0, H // DN_HEADS_PER_STEP, head_group_body, 0)

    def chunk_body(c, carry):
        r0 = pl.multiple_of(c * C, C)
        rows = pl.ds(r0, C)
        hs = range(H)
        cols = [slice(h * DN_HEAD_DIM, (h + 1) * DN_HEAD_DIM) for h in hs]
        sts = [state[h] for h in hs]
        sbs = [st.astype(BF16) for st in sts]
        wss = [_dot(w_s[rows, cols[h]], sbs[h]) for h in hs]
        qss = [_dot(qd_s[rows, cols[h]], sbs[h]) for h in hs]
        vbs = [(u_s[rows, cols[h]] - wss[h]).astype(BF16) for h in hs]
        avs = [_dot(at_s[rows, cols[h]], vbs[h]) for h in hs]
        kvs = [_dot(kdt_s[rows, cols[h]], vbs[h]) for h in hs]
        for h in hs:
            state[h] = sts[h] * cdec_s[c, h] + kvs[h]
            gated = _rms_rows(qss[h] + avs[h], ng_ref[...]) * _silu(z_ref[rows, cols[h]])
            o_ref[rows, cols[h]] = gated.astype(o_ref.dtype)
        return carry

    lax.fori_loop(0, n_chunk, chunk_body, 0)


def gated_delta_net(p, a_log, dt_bias, norm_g, l, B, S, ts, col_block0, H):
    W = H * DN_HEAD_DIM
    nt = S // ts
    NB = V7X_LANES
    gate_block = (col_block0 + 4) * (W // NB)
    C = DN_CHUNK
    vmem = (2 * 4 * ts * W * 4 + 2 * ts * NB * 4 + 2 * ts * W * 2
            + H * DN_HEAD_DIM * DN_HEAD_DIM * 4 + 2 * H * ts * V7X_LANES * 4 + H * V7X_SUBLANES * ts * 4
            + ts * W * 4 + 4 * ts * W * 2 + 48 * C * C * 4 * DN_HEADS_PER_STEP)

    def pblock(k):
        return pl.BlockSpec((ts, W), lambda b, s: (b * nt + s, col_block0 + k))

    return pl.pallas_call(
        _dn_body,
        grid=(B, nt),
        in_specs=[
            pblock(0), pblock(1), pblock(2), pblock(3),
            pl.BlockSpec((ts, NB), lambda b, s: (b * nt + s, gate_block)),
            pl.BlockSpec((None, 1, NB), lambda b, s: (l, 0, 0)),
            pl.BlockSpec((None, 1, NB), lambda b, s: (l, 0, 0)),
            pl.BlockSpec((None, 1, DN_HEAD_DIM), lambda b, s: (l, 0, 0)),
        ],
        out_specs=pl.BlockSpec((ts, W), lambda b, s: (b * nt + s, 0)),
        out_shape=jax.ShapeDtypeStruct((B * S, W), BF16),
        scratch_shapes=[
            pltpu.VMEM((H, DN_HEAD_DIM, DN_HEAD_DIM), F32),
            pltpu.VMEM((H, ts, V7X_LANES), F32),
            pltpu.VMEM((H, ts, V7X_LANES), F32),
            pltpu.VMEM((H, 1, ts), F32),
            pltpu.VMEM((ts // C, H, 1, V7X_LANES), F32),
            pltpu.VMEM((ts, W), F32),
            pltpu.VMEM((ts, W), BF16),
            pltpu.VMEM((ts, W), BF16),
            pltpu.VMEM((ts, W), BF16),
            pltpu.VMEM((ts, W), BF16),
        ],
        compiler_params=_compiler_params(2, vmem),
        name="gated_delta_net",
    )(p, p, p, p, p, a_log, dt_bias, norm_g)


def _mix_out_body(x_ref, yp_ref, yd_ref, w1_ref, w2_ref, o_ref):
    o_ref[...] = x_ref[...] + _dot(yp_ref[...], w1_ref[...]) + _dot(yd_ref[...], w2_ref[...])


def mix_out(x, y_pool, y_dn, w, l, tm, tn):
    R, D = x.shape
    K1, K2 = y_pool.shape[1], y_dn.shape[1]
    assert K1 == K2
    vmem = 2 * (2 * tm * tn * 4 + tm * (K1 + K2) * 2 + (K1 + K2) * tn * 2) + 2 * tm * tn * 4
    return pl.pallas_call(
        _mix_out_body,
        grid=(R // tm, D // tn),
        in_specs=[
            pl.BlockSpec((tm, tn), lambda i, j: (i, j)),
            pl.BlockSpec((tm, K1), lambda i, j: (i, 0)),
            pl.BlockSpec((tm, K2), lambda i, j: (i, 0)),
            pl.BlockSpec((None, K1, tn), lambda i, j: (l, 0, j)),
            pl.BlockSpec((None, K2, tn), lambda i, j: (l, 1, j)),
        ],
        out_specs=pl.BlockSpec((tm, tn), lambda i, j: (i, j)),
        out_shape=jax.ShapeDtypeStruct((R, D), F32),
        compiler_params=_compiler_params(2, vmem),
        name="mix_out",
    )(x, y_pool, y_dn, w, w)


def _xattn_body(x_ref, g_ref, wq_ref, k_ref, v_ref, wo_ref, o_ref, h_ref, a_ref):
    j = pl.program_id(1)
    dh = wq_ref.shape[1] // XA_HEADS_PER_STEP
    tn = wo_ref.shape[1]
    n_att = XA_HEADS // XA_HEADS_PER_STEP

    @pl.when(j == 0)
    def _():
        _norm_rows_into(h_ref, x_ref, g_ref)

    @pl.when(j < n_att)
    def _():
        tm = h_ref.shape[0]
        sub = min(XA_SUB_ROWS, tm)
        col0 = pl.multiple_of(j * (XA_HEADS_PER_STEP * dh), XA_HEADS_PER_STEP * dh)
        units = [(hh, r0) for hh in range(XA_HEADS_PER_STEP) for r0 in range(0, tm, sub)]

        def scores(unit):
            hh, r0 = unit
            q = _dot(h_ref[r0:r0 + sub, :], wq_ref[:, hh * dh:(hh + 1) * dh])
            return _dot_nt(q.astype(BF16), k_ref[:, hh * dh:(hh + 1) * dh]) * (dh ** -0.5)

        sc = scores(units[0])
        for n, (hh, r0) in enumerate(units):
            sc_next = scores(units[n + 1]) if n + 1 < len(units) else None
            e = jnp.exp(sc - jnp.max(sc, axis=-1, keepdims=True))
            pr = e / jnp.sum(e, axis=-1, keepdims=True)
            o = _dot(pr.astype(BF16), v_ref[:, hh * dh:(hh + 1) * dh])
            a_ref[r0:r0 + sub, pl.ds(col0 + hh * dh, dh)] = o.astype(BF16)
            sc = sc_next

    @pl.when(j >= n_att)
    def _():
        cols = pl.ds(pl.multiple_of((j - n_att) * tn, tn), tn)
        o_ref[...] = x_ref[:, cols] + _dot(a_ref[...], wo_ref[...])


def cross_attention(x, g, w_xq, kv, w_xo, l, B, S, M, tm, tn):
    R, D = x.shape
    n_att = XA_HEADS // XA_HEADS_PER_STEP
    wh = D // n_att
    tiles_per_batch = S // tm
    att = lambda j: jnp.minimum(j, n_att - 1)
    otile = lambda j: jnp.maximum(j - n_att, 0)
    vmem = (2 * tm * D * 4 + 2 * tm * D * 2 + 4 * D * wh * 2 + 4 * M * wh * 2 + 4 * D * tn * 2 + 3 * tm * tn * 4
            + 3 * tm * wh * 4 + 3 * tm * M * 4)
    return pl.pallas_call(
        _xattn_body,
        grid=(R // tm, n_att + D // tn),
        in_specs=[
            pl.BlockSpec((tm, D), lambda i, j: (i, 0)),
            pl.BlockSpec((None, 1, D), lambda i, j: (l, 0, 0)),
            pl.BlockSpec((None, D, wh), lambda i, j: (l, 0, att(j))),
            pl.BlockSpec((None, M, wh), lambda i, j: (l, i // tiles_per_batch, att(j))),
            pl.BlockSpec((None, M, wh), lambda i, j: (l, i // tiles_per_batch, n_att + att(j))),
            pl.BlockSpec((None, D, tn), lambda i, j: (l, 0, otile(j))),
        ],
        out_specs=pl.BlockSpec((tm, tn), lambda i, j: (i, otile(j))),
        out_shape=jax.ShapeDtypeStruct((R, D), F32),
        scratch_shapes=[pltpu.VMEM((tm, D), BF16), pltpu.VMEM((tm, D), BF16)],
        compiler_params=_compiler_params(2, vmem),
        name="cross_attention",
    )(x, g, w_xq, kv, kv, w_xo)


def _ffn_body(x_ref, g_ref, wg_ref, wu_ref, cw_ref, cb_ref, wd_ref, og_ref, o_ref, h_ref, gbuf, carry, *,
              tiles_per_seq, norm_output):
    i = pl.program_id(0)
    j = pl.program_id(1)
    tm = x_ref.shape[0]
    K = cw_ref.shape[0]
    halo = V7X_SUBLANES

    @pl.when(j == 0)
    def _():
        _norm_rows_into(h_ref, x_ref, g_ref)
        o_ref[...] = x_ref[...]

    gbuf[0:halo, :] = jnp.where((i % tiles_per_seq) == 0, 0.0, carry[j])
    sub = min(FFN_SUB_ROWS, tm)

    def gate_up(r0):
        h = h_ref[r0:r0 + sub, :]
        return _dot(h, wg_ref[...]), _dot(h, wu_ref[...])

    gate, up = gate_up(0)
    for r0 in range(0, tm, sub):
        nxt = gate_up(r0 + sub) if r0 + sub < tm else None
        gbuf[halo + r0:halo + r0 + sub, :] = gate
        conv = gate * cw_ref[K - 1:K, :] + cb_ref[...]
        for t in range(K - 2, -1, -1):
            conv = conv + gbuf[halo - K + 1 + t + r0:halo - K + 1 + t + r0 + sub, :] * cw_ref[t:t + 1, :]
        act = _silu(conv) * up
        o_ref[r0:r0 + sub, :] += _dot(act.astype(BF16), wd_ref[...])
        if nxt is not None:
            gate, up = nxt
    carry[j] = gbuf[tm:tm + halo, :]

    if norm_output:
        @pl.when(j == pl.num_programs(1) - 1)
        def _():
            _norm_rows_into(o_ref, o_ref, og_ref)


def conv_glu_ffn(x, g, w_gate, w_up, conv_w, conv_b, w_down, out_g, l, S, tm, tf, norm_output):
    R, D = x.shape
    F = w_gate.shape[2]
    K = conv_w.shape[1]
    nf = F // tf
    vmem = (4 * tm * D * 4 + tm * D * 2 + 2 * 3 * D * tf * 2 + (tm + V7X_SUBLANES) * tf * 4
            + nf * V7X_SUBLANES * tf * 4 + 4 * tm * tf * 4)
    return pl.pallas_call(
        functools.partial(_ffn_body, tiles_per_seq=S // tm, norm_output=norm_output),
        grid=(R // tm, nf),
        in_specs=[
            pl.BlockSpec((tm, D), lambda i, j: (i, 0)),
            pl.BlockSpec((None, 1, D), lambda i, j: (l, 0, 0)),
            pl.BlockSpec((None, D, tf), lambda i, j: (l, 0, j)),
            pl.BlockSpec((None, D, tf), lambda i, j: (l, 0, j)),
            pl.BlockSpec((None, K, tf), lambda i, j: (l, 0, j)),
            pl.BlockSpec((None, 1, tf), lambda i, j: (l, 0, j)),
            pl.BlockSpec((None, tf, D), lambda i, j: (l, j, 0)),
            pl.BlockSpec((1, D), lambda i, j: (0, 0)),
        ],
        out_specs=pl.BlockSpec((tm, D), lambda i, j: (i, 0)),
        out_shape=jax.ShapeDtypeStruct((R, D), F32),
        scratch_shapes=[
            pltpu.VMEM((tm, D), BF16),
            pltpu.VMEM((tm + V7X_SUBLANES, tf), F32),
            pltpu.VMEM((nf, V7X_SUBLANES, tf), F32),
        ],
        compiler_params=_compiler_params(2, vmem),
        name="conv_glu_ffn",
    )(x, g, w_gate, w_up, conv_w, conv_b, w_down, out_g)


def _tiles(S):
    return dict(tm=min(1024, S), ts_pool=min(512, S), ts_dn=min(512, S))


def kernel(x, mem, mix_norm_g, w_in, w_pool, pool_scale, dn_conv_w, dn_a_log, dn_dt_bias, dn_norm_g, w_mix_out,
           xa_norm_g, mem_norm_g, w_xq, w_xkv, w_xo, ffn_norm_g, w_gate, w_up, ffn_conv_w, ffn_conv_b, w_down,
           final_norm_g):
    B, S, D = x.shape
    M = mem.shape[1]
    depth = w_in.shape[0]
    H = dn_a_log.shape[1]
    dn_w = H * DN_HEAD_DIM
    pool_w = w_pool.shape[1] * w_pool.shape[2]
    main = pool_w + 4 * dn_w
    assert pool_w == dn_w and w_in.shape[2] == main + 2 * H and 2 * H <= V7X_LANES
    t = _tiles(S)

    tn = 512
    n_in = -(-w_in.shape[2] // tn) * tn
    w_in_b = jnp.pad(w_in.astype(BF16), ((0, 0), (0, 0), (0, n_in - w_in.shape[2])))
    pad_gate = lambda a: jnp.pad(a, ((0, 0), (H, V7X_LANES - 2 * H)))[:, None, :]
    a_log_p, dt_bias_p = pad_gate(dn_a_log), pad_gate(dn_dt_bias)
    w_pool_b = w_pool.astype(BF16)
    w_mix_b = w_mix_out.astype(BF16)
    w_xq_b, w_xkv_b, w_xo_b = w_xq.astype(BF16), w_xkv.astype(BF16), w_xo.astype(BF16)
    w_gate_b, w_up_b, w_down_b = w_gate.astype(BF16), w_up.astype(BF16), w_down.astype(BF16)
    row = lambda a: a[:, None, :]

    xr = x.reshape(B * S, D)
    kv = norm_matmul(mem.reshape(B * M, D), mem_norm_g[None, :], w_xkv_b, BF16, tm=min(1024, B * M), tn=tn)
    for l in range(depth):
        p = in_proj(xr, mix_norm_g[l][None, :], w_in_b, dn_conv_w, l, S, tm=t["tm"], tn=tn, conv_col0=pool_w)
        y_pool = pool_mixer(p, w_pool_b, row(pool_scale), l, B, S, t["ts_pool"])
        y_dn = gated_delta_net(p, a_log_p, dt_bias_p, row(dn_norm_g), l, B, S, t["ts_dn"],
                               col_block0=pool_w // dn_w, H=H)
        xr = mix_out(xr, y_pool, y_dn, w_mix_b, l, tm=t["tm"], tn=1024)
        xr = cross_attention(xr, row(xa_norm_g), w_xq_b, kv, w_xo_b, l, B, S, M, tm=t["tm"], tn=2 * tn)
        xr = conv_glu_ffn(xr, row(ffn_norm_g), w_gate_b, w_up_b, ffn_conv_w, row(ffn_conv_b), w_down_b,
                          final_norm_g[None, :], l, S, tm=t["tm"], tf=tn, norm_output=(l == depth - 1))
    return xr.reshape(B, S, D)
```

```python
import functools

import jax
import jax.numpy as jnp
from jax import lax
from jax.experimental import pallas as pl
from jax.experimental.pallas import tpu as pltpu

F32 = jnp.float32
BF16 = jnp.bfloat16

EPS = 1e-6
POOL_WINDOWS = (2, 4, 8, 16)
POOL_HALO = 16
DN_HEAD_DIM = 128
DN_CHUNK = 128
DN_HEADS_PER_STEP = 2
XA_HEADS = 4
XA_HEADS_PER_STEP = 2

V7X_VMEM_BYTES = 64 * 1024 * 1024
V7X_LANES = 128
V7X_SUBLANES = 8
NORM_ROWS = 128
XA_SUB_ROWS = 512
FFN_SUB_ROWS = 512
IN_PROJ_SUB_ROWS = 128


def _compiler_params(n_axes, vmem_bytes):
    limit = min(int(vmem_bytes * 1.25) + (4 << 20), V7X_VMEM_BYTES * 7 // 8)
    return pltpu.CompilerParams(dimension_semantics=("arbitrary",) * n_axes, vmem_limit_bytes=limit)


def _dot(a, b):
    return jnp.dot(a, b, preferred_element_type=F32)


def _dot_nt(a, b):
    return lax.dot_general(a, b, (((1,), (1,)), ((), ())), preferred_element_type=F32)


def _silu(x):
    return x * (1.0 / (1.0 + jnp.exp(-x)))


def _rms_rows(x, g):
    ms = jnp.mean(x * x, axis=-1, keepdims=True)
    return x * lax.rsqrt(ms + EPS) * g


def _rows_shifted_down(a, prev, s):
    n = prev.shape[0]
    rows = lax.broadcasted_iota(jnp.int32, prev.shape, 0)
    rolled = [pltpu.roll(piece, s, axis=0) for piece in [prev] + [a[r:r + n, :] for r in range(0, a.shape[0], n)]]
    return jnp.concatenate([jnp.where(rows < s, lo, hi) for lo, hi in zip(rolled[:-1], rolled[1:])], axis=0)


def _norm_rows_into(h_ref, x_ref, g_ref):
    g = g_ref[...]

    def body(r, c):
        rows = pl.ds(pl.multiple_of(r * NORM_ROWS, NORM_ROWS), NORM_ROWS)
        h_ref[rows, :] = _rms_rows(x_ref[rows, :], g).astype(h_ref.dtype)
        return c

    lax.fori_loop(0, x_ref.shape[0] // NORM_ROWS, body, 0)


def _norm_matmul_body(x_ref, g_ref, w_ref, o_ref, h_ref):
    @pl.when((pl.program_id(1) == 0) & (pl.program_id(2) == 0))
    def _():
        _norm_rows_into(h_ref, x_ref, g_ref)

    o_ref[...] = _dot(h_ref[...], w_ref[...]).astype(o_ref.dtype)


def norm_matmul(x, g, w, out_dtype, tm, tn):
    R, K = x.shape
    L, _, N = w.shape
    osz = jnp.dtype(out_dtype).itemsize
    vmem = 2 * tm * K * 4 + tm * K * 2 + 2 * K * tn * 2 + 2 * tm * tn * osz + tm * tn * 4
    return pl.pallas_call(
        _norm_matmul_body,
        grid=(R // tm, L, N // tn),
        in_specs=[
            pl.BlockSpec((tm, K), lambda i, l, j: (i, 0)),
            pl.BlockSpec((1, K), lambda i, l, j: (0, 0)),
            pl.BlockSpec((None, K, tn), lambda i, l, j: (l, 0, j)),
        ],
        out_specs=pl.BlockSpec((None, tm, tn), lambda i, l, j: (l, i, j)),
        out_shape=jax.ShapeDtypeStruct((L, R, N), out_dtype),
        scratch_shapes=[pltpu.VMEM((tm, K), BF16)],
        compiler_params=_compiler_params(3, vmem),
        name="norm_matmul",
    )(x, g, w)


def _in_proj_body(x_ref, g_ref, w_ref, wg_ref, cw_ref, p_ref, gate_ref, h_ref, carry, *,
                  n_main, tiles_per_seq, conv0, tiles_per_part):
    i = pl.program_id(0)
    j = pl.program_id(1)
    tm, tn = p_ref.shape
    K = cw_ref.shape[0]
    halo = V7X_SUBLANES

    @pl.when(j == 0)
    def _():
        _norm_rows_into(h_ref, x_ref, g_ref)

    is_conv = (j >= conv0) & (j < conv0 + 3 * tiles_per_part)

    @pl.when(j == n_main)
    def _():
        gate_ref[...] = _dot(h_ref[...], wg_ref[...])

    @pl.when(jnp.logical_not(is_conv) & (j < n_main))
    def _():
        p_ref[...] = _dot(h_ref[...], w_ref[...])

    @pl.when(is_conv)
    def _():
        cj = j - conv0
        part = cj // tiles_per_part
        q_scale = jnp.where(part == 0, DN_HEAD_DIM ** -0.5, 1.0)
        prev = jnp.where((i % tiles_per_seq) == 0, 0.0, carry[cj])
        sub = min(IN_PROJ_SUB_ROWS, tm)
        for r0 in range(0, tm, sub):
            acc = _dot(h_ref[r0:r0 + sub, :], w_ref[...])
            y = acc * cw_ref[K - 1:K, :]
            for s in range(1, K):
                y = y + _rows_shifted_down(acc, prev, s) * cw_ref[K - 1 - s:K - s, :]
            prev = acc[sub - halo:sub, :]
            y = _silu(y)
            for c in range(0, tn, DN_HEAD_DIM):
                yh = y[:, c:c + DN_HEAD_DIM]
                inv = lax.rsqrt(jnp.sum(yh * yh, axis=-1, keepdims=True) + EPS) * q_scale
                p_ref[r0:r0 + sub, c:c + DN_HEAD_DIM] = yh * jnp.where(part == 2, 1.0, inv)
        carry[cj] = prev


def in_proj(x, g, w, w_gates, conv_w, l, S, tm, tn, conv_col0):
    R, D = x.shape
    N = w.shape[2]
    NG = w_gates.shape[2]
    K, conv_cols = conv_w.shape[1:]
    assert N % tn == 0 and conv_col0 % tn == 0 and (conv_cols // 3) % tn == 0 and tn % DN_HEAD_DIM == 0
    n_main, conv0, n_conv = N // tn, conv_col0 // tn, conv_cols // tn
    main = lambda j: jnp.minimum(j, n_main - 1)
    vmem = (2 * tm * D * 4 + tm * D * 2 + 2 * D * tn * 2 + 2 * D * NG * 2 + 3 * tm * tn * 4 + 2 * tm * NG * 4
            + n_conv * V7X_SUBLANES * tn * 4 + 8 * IN_PROJ_SUB_ROWS * tn * 4)
    return pl.pallas_call(
        functools.partial(_in_proj_body, n_main=n_main, tiles_per_seq=S // tm, conv0=conv0,
                          tiles_per_part=n_conv // 3),
        grid=(R // tm, n_main + 1),
        in_specs=[
            pl.BlockSpec((tm, D), lambda i, j: (i, 0)),
            pl.BlockSpec((1, D), lambda i, j: (0, 0)),
            pl.BlockSpec((None, D, tn), lambda i, j: (l, 0, main(j))),
            pl.BlockSpec((None, D, NG), lambda i, j: (l, 0, 0)),
            pl.BlockSpec((None, K, tn), lambda i, j: (l, 0, jnp.clip(j - conv0, 0, n_conv - 1))),
        ],
        out_specs=[
            pl.BlockSpec((tm, tn), lambda i, j: (i, main(j))),
            pl.BlockSpec((tm, NG), lambda i, j: (i, 0)),
        ],
        out_shape=[jax.ShapeDtypeStruct((R, N), F32), jax.ShapeDtypeStruct((R, NG), F32)],
        scratch_shapes=[
            pltpu.VMEM((tm, D), BF16),
            pltpu.VMEM((n_conv, V7X_SUBLANES, tn), F32),
        ],
        compiler_params=_compiler_params(2, vmem),
        name="in_proj",
    )(x, g, w, w_gates, conv_w)


def _pool_body(u_ref, wp_ref, sc_ref, o_ref, buf):
    s = pl.program_id(1)
    ts = u_ref.shape[0]
    G = wp_ref.shape[1]

    @pl.when(s == 0)
    def _():
        buf[0:POOL_HALO, :] = jnp.zeros((POOL_HALO, buf.shape[1]), F32)

    buf[POOL_HALO:POOL_HALO + ts, :] = u_ref[...]
    pos = (s * ts + 1 + lax.broadcasted_iota(jnp.int32, (ts, 1), 0)).astype(F32)
    for i, w in enumerate(POOL_WINDOWS):
        cols = slice(i * G, (i + 1) * G)
        u = buf[POOL_HALO:POOL_HALO + ts, cols]
        acc = u
        for k in range(1, w):
            acc = acc + buf[POOL_HALO - k:POOL_HALO - k + ts, cols]
        mixed = acc / jnp.minimum(pos, float(w)) - u
        y = _dot(mixed.astype(BF16), wp_ref[i]) * sc_ref[:, cols]
        o_ref[:, cols] = y.astype(o_ref.dtype)
    buf[0:POOL_HALO, :] = buf[ts:ts + POOL_HALO, :]


def pool_mixer(p, w_pool, pool_scale, l, B, S, ts):
    n_win, G, _ = w_pool.shape[1:]
    W = n_win * G
    nt = S // ts
    vmem = 2 * ts * W * 4 + (ts + POOL_HALO) * W * 4 + 2 * n_win * G * G * 2 + 2 * ts * W * 2 + 4 * ts * G * 4
    return pl.pallas_call(
        _pool_body,
        grid=(B, nt),
        in_specs=[
            pl.BlockSpec((ts, W), lambda b, s: (b * nt + s, 0)),
            pl.BlockSpec((None, n_win, G, G), lambda b, s: (l, 0, 0, 0)),
            pl.BlockSpec((None, 1, W), lambda b, s: (l, 0, 0)),
        ],
        out_specs=pl.BlockSpec((ts, W), lambda b, s: (b * nt + s, 0)),
        out_shape=jax.ShapeDtypeStruct((B * S, W), BF16),
        scratch_shapes=[pltpu.VMEM((ts + POOL_HALO, W), F32)],
        compiler_params=_compiler_params(2, vmem),
        name="pool_mixer",
    )(p, w_pool, pool_scale)


def _unit_lower_inverses(ls, eye, ii, jj):
    C = ls[0].shape[0]
    size = 2
    same = (ii // size) == (jj // size)
    xs = [eye - jnp.where(same, l_mat, 0.0) for l_mat in ls]
    while size < C:
        inner, same = same, (ii // (2 * size)) == (jj // (2 * size))
        offs = [jnp.where(same & jnp.logical_not(inner), l_mat, 0.0).astype(BF16) for l_mat in ls]
        xbs = [x.astype(BF16) for x in xs]
        ys = [_dot(off, xb) for off, xb in zip(offs, xbs)]
        xs = [x - _dot(xb, y.astype(BF16)) for x, xb, y in zip(xs, xbs, ys)]
        size *= 2
    return xs


def _dn_body(q_ref, k_ref, v_ref, z_ref, ba_ref, alog_ref, dtb_ref, ng_ref, o_ref,
             state, beta_rep, gc_rep, grow_s, cdec_s, u_s, w_s, qd_s, at_s, kdt_s):
    s = pl.program_id(1)
    ts, W = q_ref.shape
    H = W // DN_HEAD_DIM
    C = DN_CHUNK
    n_chunk = ts // C

    @pl.when(s == 0)
    def _():
        state[...] = jnp.zeros(state.shape, F32)

    ba = ba_ref[...]
    beta = 1.0 / (1.0 + jnp.exp(-ba))
    xs = ba + dtb_ref[...]
    softplus = jnp.maximum(xs, 0.0) + jnp.log1p(jnp.exp(-jnp.abs(xs)))
    g = -jnp.exp(alog_ref[...]) * softplus
    row_in_chunk = lax.broadcasted_iota(jnp.int32, (ts, V7X_LANES), 0) % C
    gc = g
    sh = 1
    while sh < C:
        gc = gc + jnp.where(row_in_chunk >= sh, pltpu.roll(gc, sh, axis=0), 0.0)
        sh *= 2
    gct = gc.T
    lane = lax.broadcasted_iota(jnp.int32, (ts, V7X_LANES), 1)
    for h in range(H):
        beta_rep[h] = jnp.broadcast_to(
            jnp.sum(jnp.where(lane == h, beta, 0.0), axis=-1, keepdims=True), (ts, V7X_LANES))
        gc_rep[h] = jnp.broadcast_to(
            jnp.sum(jnp.where(lane == H + h, gc, 0.0), axis=-1, keepdims=True), (ts, V7X_LANES))
        grow_s[h] = gct[H + h:H + h + 1, :]

    ii = lax.broadcasted_iota(jnp.int32, (C, C), 0)
    jj = lax.broadcasted_iota(jnp.int32, (C, C), 1)
    eye = (ii == jj).astype(F32)

    def head_group_body(hg, carry):
        pairs = []
        for hh in range(DN_HEADS_PER_STEP):
            h = hg * DN_HEADS_PER_STEP + hh
            cols = pl.ds(pl.multiple_of(h * DN_HEAD_DIM, DN_HEAD_DIM), DN_HEAD_DIM)
            bcol, gcol, grow_all = beta_rep[h], gc_rep[h], grow_s[h]
            egc = jnp.exp(gcol)
            for c in range(n_chunk):
                r = slice(c * C, (c + 1) * C)
                pairs.append(dict(h=h, c=c, r=r, cols=cols, q=q_ref[r, cols], k=k_ref[r, cols], v=v_ref[r, cols],
                                  b=bcol[r], g=gcol[r], e=egc[r], grow=grow_all[:, r]))
        kbs = [p["k"].astype(BF16) for p in pairs]
        decs = [jnp.exp(jnp.minimum(p["g"] - p["grow"], 0.0)) for p in pairs]
        kks = [_dot_nt(kb, kb) for kb in kbs]
        qks = [_dot_nt(p["q"].astype(BF16), kb) for p, kb in zip(pairs, kbs)]
        ls = [jnp.where(ii > jj, kk * dec, 0.0) * p["b"] for kk, dec, p in zip(kks, decs, pairs)]
        for p, qk, dec in zip(pairs, qks, decs):
            r, cols = p["r"], p["cols"]
            at_s[r, cols] = jnp.where(ii >= jj, qk * dec, 0.0).astype(BF16)
            qd_s[r, cols] = (p["q"] * p["e"]).astype(BF16)
            kd = p["k"] * jnp.exp(p["g"][C - 1:C, :] - p["g"])
            kdt_s[r, cols] = kd.T.astype(BF16)
            cdec_s[p["c"], p["h"]] = p["e"][C - 1:C, :]
        t_invs = _unit_lower_inverses(ls, eye, ii, jj)
        rhs = [jnp.concatenate([p["v"] * p["b"], p["k"] * (p["b"] * p["e"])], axis=1) for p in pairs]
        sols = [_dot(t_inv.astype(BF16), b.astype(BF16)) for t_inv, b in zip(t_invs, rhs)]
        for p, sol in zip(pairs, sols):
            u_s[p["r"], p["cols"]] = sol[:, :DN_HEAD_DIM]
            w_s[p["r"], p["cols"]] = sol[:, DN_HEAD_DIM:].astype(BF16)
        return carry

    lax.fori_loop(0, H // DN_HEADS_PER_STEP, head_group_body, 0)

    def chunk_body(c, carry):
        r0 = pl.multiple_of(c * C, C)
        rows = pl.ds(r0, C)
        hs = range(H)
        cols = [slice(h * DN_HEAD_DIM, (h + 1) * DN_HEAD_DIM) for h in hs]
        sts = [state[h] for h in hs]
        sbs = [st.astype(BF16) for st in sts]
        wss = [_dot(w_s[rows, cols[h]], sbs[h]) for h in hs]
        qss = [_dot(qd_s[rows, cols[h]], sbs[h]) for h in hs]
        vbs = [(u_s[rows, cols[h]] - wss[h]).astype(BF16) for h in hs]
        avs = [_dot(at_s[rows, cols[h]], vbs[h]) for h in hs]
        kvs = [_dot(kdt_s[rows, cols[h]], vbs[h]) for h in hs]
        for h in hs:
            state[h] = sts[h] * cdec_s[c, h] + kvs[h]
            gated = _rms_rows(qss[h] + avs[h], ng_ref[...]) * _silu(z_ref[rows, cols[h]])
            o_ref[rows, cols[h]] = gated.astype(o_ref.dtype)
        return carry

    lax.fori_loop(0, n_chunk, chunk_body, 0)


def gated_delta_net(p, gates, a_log, dt_bias, norm_g, l, B, S, ts, col_block0, H):
    W = H * DN_HEAD_DIM
    nt = S // ts
    NB = gates.shape[1]
    C = DN_CHUNK
    vmem = (2 * 4 * ts * W * 4 + 2 * ts * NB * 4 + 2 * ts * W * 2
            + H * DN_HEAD_DIM * DN_HEAD_DIM * 4 + 2 * H * ts * V7X_LANES * 4 + H * V7X_SUBLANES * ts * 4
            + ts * W * 4 + 4 * ts * W * 2 + 48 * C * C * 4 * DN_HEADS_PER_STEP)

    def pblock(k):
        return pl.BlockSpec((ts, W), lambda b, s: (b * nt + s, col_block0 + k))

    return pl.pallas_call(
        _dn_body,
        grid=(B, nt),
        in_specs=[
            pblock(0), pblock(1), pblock(2), pblock(3),
            pl.BlockSpec((ts, NB), lambda b, s: (b * nt + s, 0)),
            pl.BlockSpec((None, 1, NB), lambda b, s: (l, 0, 0)),
            pl.BlockSpec((None, 1, NB), lambda b, s: (l, 0, 0)),
            pl.BlockSpec((None, 1, DN_HEAD_DIM), lambda b, s: (l, 0, 0)),
        ],
        out_specs=pl.BlockSpec((ts, W), lambda b, s: (b * nt + s, 0)),
        out_shape=jax.ShapeDtypeStruct((B * S, W), BF16),
        scratch_shapes=[
            pltpu.VMEM((H, DN_HEAD_DIM, DN_HEAD_DIM), F32),
            pltpu.VMEM((H, ts, V7X_LANES), F32),
            pltpu.VMEM((H, ts, V7X_LANES), F32),
            pltpu.VMEM((H, 1, ts), F32),
            pltpu.VMEM((ts // C, H, 1, V7X_LANES), F32),
            pltpu.VMEM((ts, W), F32),
            pltpu.VMEM((ts, W), BF16),
            pltpu.VMEM((ts, W), BF16),
            pltpu.VMEM((ts, W), BF16),
            pltpu.VMEM((ts, W), BF16),
        ],
        compiler_params=_compiler_params(2, vmem),
        name="gated_delta_net",
    )(p, p, p, p, gates, a_log, dt_bias, norm_g)


def _mix_out_body(x_ref, yp_ref, yd_ref, w1_ref, w2_ref, o_ref):
    o_ref[...] = x_ref[...] + _dot(yp_ref[...], w1_ref[...]) + _dot(yd_ref[...], w2_ref[...])


def mix_out(x, y_pool, y_dn, w, l, tm, tn):
    R, D = x.shape
    K1, K2 = y_pool.shape[1], y_dn.shape[1]
    assert K1 == K2
    vmem = 2 * (2 * tm * tn * 4 + tm * (K1 + K2) * 2 + (K1 + K2) * tn * 2) + 2 * tm * tn * 4
    return pl.pallas_call(
        _mix_out_body,
        grid=(R // tm, D // tn),
        in_specs=[
            pl.BlockSpec((tm, tn), lambda i, j: (i, j)),
            pl.BlockSpec((tm, K1), lambda i, j: (i, 0)),
            pl.BlockSpec((tm, K2), lambda i, j: (i, 0)),
            pl.BlockSpec((None, K1, tn), lambda i, j: (l, 0, j)),
            pl.BlockSpec((None, K2, tn), lambda i, j: (l, 1, j)),
        ],
        out_specs=pl.BlockSpec((tm, tn), lambda i, j: (i, j)),
        out_shape=jax.ShapeDtypeStruct((R, D), F32),
        compiler_params=_compiler_params(2, vmem),
        name="mix_out",
    )(x, y_pool, y_dn, w, w)


def _xattn_body(x_ref, g_ref, wq_ref, k_ref, v_ref, wo_ref, o_ref, h_ref, a_ref):
    j = pl.program_id(1)
    dh = wq_ref.shape[1] // XA_HEADS_PER_STEP
    tn = wo_ref.shape[1]
    n_att = XA_HEADS // XA_HEADS_PER_STEP

    @pl.when(j == 0)
    def _():
        _norm_rows_into(h_ref, x_ref, g_ref)

    @pl.when(j < n_att)
    def _():
        tm = h_ref.shape[0]
        sub = min(XA_SUB_ROWS, tm)
        col0 = pl.multiple_of(j * (XA_HEADS_PER_STEP * dh), XA_HEADS_PER_STEP * dh)
        units = [(hh, r0) for hh in range(XA_HEADS_PER_STEP) for r0 in range(0, tm, sub)]

        def scores(unit):
            hh, r0 = unit
            q = _dot(h_ref[r0:r0 + sub, :], wq_ref[:, hh * dh:(hh + 1) * dh])
            return _dot_nt(q.astype(BF16), k_ref[:, hh * dh:(hh + 1) * dh]) * (dh ** -0.5)

        sc = scores(units[0])
        for n, (hh, r0) in enumerate(units):
            sc_next = scores(units[n + 1]) if n + 1 < len(units) else None
            e = jnp.exp(sc - jnp.max(sc, axis=-1, keepdims=True))
            pr = e / jnp.sum(e, axis=-1, keepdims=True)
            o = _dot(pr.astype(BF16), v_ref[:, hh * dh:(hh + 1) * dh])
            a_ref[r0:r0 + sub, pl.ds(col0 + hh * dh, dh)] = o.astype(BF16)
            sc = sc_next

    @pl.when(j >= n_att)
    def _():
        cols = pl.ds(pl.multiple_of((j - n_att) * tn, tn), tn)
        o_ref[...] = x_ref[:, cols] + _dot(a_ref[...], wo_ref[...])


def cross_attention(x, g, w_xq, kv, w_xo, l, B, S, M, tm, tn):
    R, D = x.shape
    n_att = XA_HEADS // XA_HEADS_PER_STEP
    wh = D // n_att
    tiles_per_batch = S // tm
    att = lambda j: jnp.minimum(j, n_att - 1)
    otile = lambda j: jnp.maximum(j - n_att, 0)
    vmem = (2 * tm * D * 4 + 2 * tm * D * 2 + 4 * D * wh * 2 + 4 * M * wh * 2 + 4 * D * tn * 2 + 3 * tm * tn * 4
            + 3 * tm * wh * 4 + 3 * tm * M * 4)
    return pl.pallas_call(
        _xattn_body,
        grid=(R // tm, n_att + D // tn),
        in_specs=[
            pl.BlockSpec((tm, D), lambda i, j: (i, 0)),
            pl.BlockSpec((None, 1, D), lambda i, j: (l, 0, 0)),
            pl.BlockSpec((None, D, wh), lambda i, j: (l, 0, att(j))),
            pl.BlockSpec((None, M, wh), lambda i, j: (l, i // tiles_per_batch, att(j))),
            pl.BlockSpec((None, M, wh), lambda i, j: (l, i // tiles_per_batch, n_att + att(j))),
            pl.BlockSpec((None, D, tn), lambda i, j: (l, 0, otile(j))),
        ],
        out_specs=pl.BlockSpec((tm, tn), lambda i, j: (i, otile(j))),
        out_shape=jax.ShapeDtypeStruct((R, D), F32),
        scratch_shapes=[pltpu.VMEM((tm, D), BF16), pltpu.VMEM((tm, D), BF16)],
        compiler_params=_compiler_params(2, vmem),
        name="cross_attention",
    )(x, g, w_xq, kv, kv, w_xo)


def _ffn_body(x_ref, g_ref, wg_ref, wu_ref, cw_ref, cb_ref, wd_ref, og_ref, o_ref, h_ref, gbuf, carry, *,
              tiles_per_seq, norm_output):
    i = pl.program_id(0)
    j = pl.program_id(1)
    tm = x_ref.shape[0]
    K = cw_ref.shape[0]
    halo = V7X_SUBLANES

    @pl.when(j == 0)
    def _():
        _norm_rows_into(h_ref, x_ref, g_ref)
        o_ref[...] = x_ref[...]

    gbuf[0:halo, :] = jnp.where((i % tiles_per_seq) == 0, 0.0, carry[j])
    sub = min(FFN_SUB_ROWS, tm)

    def gate_up(r0):
        h = h_ref[r0:r0 + sub, :]
        return _dot(h, wg_ref[...]), _dot(h, wu_ref[...])

    gate, up = gate_up(0)
    for r0 in range(0, tm, sub):
        nxt = gate_up(r0 + sub) if r0 + sub < tm else None
        gbuf[halo + r0:halo + r0 + sub, :] = gate
        conv = gate * cw_ref[K - 1:K, :] + cb_ref[...]
        for t in range(K - 2, -1, -1):
            conv = conv + gbuf[halo - K + 1 + t + r0:halo - K + 1 + t + r0 + sub, :] * cw_ref[t:t + 1, :]
        act = _silu(conv) * up
        o_ref[r0:r0 + sub, :] += _dot(act.astype(BF16), wd_ref[...])
        if nxt is not None:
            gate, up = nxt
    carry[j] = gbuf[tm:tm + halo, :]

    if norm_output:
        @pl.when(j == pl.num_programs(1) - 1)
        def _():
            _norm_rows_into(o_ref, o_ref, og_ref)


def conv_glu_ffn(x, g, w_gate, w_up, conv_w, conv_b, w_down, out_g, l, S, tm, tf, norm_output):
    R, D = x.shape
    F = w_gate.shape[2]
    K = conv_w.shape[1]
    nf = F // tf
    vmem = (4 * tm * D * 4 + tm * D * 2 + 2 * 3 * D * tf * 2 + (tm + V7X_SUBLANES) * tf * 4
            + nf * V7X_SUBLANES * tf * 4 + 4 * tm * tf * 4)
    return pl.pallas_call(
        functools.partial(_ffn_body, tiles_per_seq=S // tm, norm_output=norm_output),
        grid=(R // tm, nf),
        in_specs=[
            pl.BlockSpec((tm, D), lambda i, j: (i, 0)),
            pl.BlockSpec((None, 1, D), lambda i, j: (l, 0, 0)),
            pl.BlockSpec((None, D, tf), lambda i, j: (l, 0, j)),
            pl.BlockSpec((None, D, tf), lambda i, j: (l, 0, j)),
            pl.BlockSpec((None, K, tf), lambda i, j: (l, 0, j)),
            pl.BlockSpec((None, 1, tf), lambda i, j: (l, 0, j)),
            pl.BlockSpec((None, tf, D), lambda i, j: (l, j, 0)),
            pl.BlockSpec((1, D), lambda i, j: (0, 0)),
        ],
        out_specs=pl.BlockSpec((tm, D), lambda i, j: (i, 0)),
        out_shape=jax.ShapeDtypeStruct((R, D), F32),
        scratch_shapes=[
            pltpu.VMEM((tm, D), BF16),
            pltpu.VMEM((tm + V7X_SUBLANES, tf), F32),
            pltpu.VMEM((nf, V7X_SUBLANES, tf), F32),
        ],
        compiler_params=_compiler_params(2, vmem),
        name="conv_glu_ffn",
    )(x, g, w_gate, w_up, conv_w, conv_b, w_down, out_g)


def _tiles(S):
    return dict(tm=min(1024, S), ts_pool=min(512, S), ts_dn=min(512, S))


def kernel(x, mem, mix_norm_g, w_in, w_pool, pool_scale, dn_conv_w, dn_a_log, dn_dt_bias, dn_norm_g, w_mix_out,
           xa_norm_g, mem_norm_g, w_xq, w_xkv, w_xo, ffn_norm_g, w_gate, w_up, ffn_conv_w, ffn_conv_b, w_down,
           final_norm_g):
    B, S, D = x.shape
    M = mem.shape[1]
    depth = w_in.shape[0]
    H = dn_a_log.shape[1]
    dn_w = H * DN_HEAD_DIM
    pool_w = w_pool.shape[1] * w_pool.shape[2]
    main = pool_w + 4 * dn_w
    assert pool_w == dn_w and w_in.shape[2] == main + 2 * H and 2 * H <= V7X_LANES
    t = _tiles(S)

    tn = 512
    w_in_b = w_in[:, :, :main].astype(BF16)
    w_gates_b = jnp.pad(w_in[:, :, main:].astype(BF16), ((0, 0), (0, 0), (0, V7X_LANES - 2 * H)))
    pad_gate = lambda a: jnp.pad(a, ((0, 0), (H, V7X_LANES - 2 * H)))[:, None, :]
    a_log_p, dt_bias_p = pad_gate(dn_a_log), pad_gate(dn_dt_bias)
    w_pool_b = w_pool.astype(BF16)
    w_mix_b = w_mix_out.astype(BF16)
    w_xq_b, w_xkv_b, w_xo_b = w_xq.astype(BF16), w_xkv.astype(BF16), w_xo.astype(BF16)
    w_gate_b, w_up_b, w_down_b = w_gate.astype(BF16), w_up.astype(BF16), w_down.astype(BF16)
    row = lambda a: a[:, None, :]

    xr = x.reshape(B * S, D)
    kv = norm_matmul(mem.reshape(B * M, D), mem_norm_g[None, :], w_xkv_b, BF16, tm=min(1024, B * M), tn=tn)
    for l in range(depth):
        p, gates = in_proj(xr, mix_norm_g[l][None, :], w_in_b, w_gates_b, dn_conv_w, l, S, tm=t["tm"], tn=2 * tn,
                           conv_col0=pool_w)
        y_pool = pool_mixer(p, w_pool_b, row(pool_scale), l, B, S, t["ts_pool"])
        y_dn = gated_delta_net(p, gates, a_log_p, dt_bias_p, row(dn_norm_g), l, B, S, t["ts_dn"],
                               col_block0=pool_w // dn_w, H=H)
        xr = mix_out(xr, y_pool, y_dn, w_mix_b, l, tm=t["tm"], tn=1024)
        xr = cross_attention(xr, row(xa_norm_g), w_xq_b, kv, w_xo_b, l, B, S, M, tm=t["tm"], tn=2 * tn)
        xr = conv_glu_ffn(xr, row(ffn_norm_g), w_gate_b, w_up_b, ffn_conv_w, row(ffn_conv_b), w_down_b,
                          final_norm_g[None, :], l, S, tm=t["tm"], tf=tn, norm_output=(l == depth - 1))
    return xr.reshape(B, S, D)
```

```python
import functools

import jax
import jax.numpy as jnp
from jax import lax
from jax.experimental import pallas as pl
from jax.experimental.pallas import tpu as pltpu

F32 = jnp.float32
BF16 = jnp.bfloat16

EPS = 1e-6
POOL_WINDOWS = (2, 4, 8, 16)
POOL_HALO = 16
DN_HEAD_DIM = 128
DN_CHUNK = 128
DN_HEADS_PER_STEP = 2
XA_HEADS = 4
XA_HEADS_PER_STEP = 2

V7X_VMEM_BYTES = 64 * 1024 * 1024
V7X_LANES = 128
V7X_SUBLANES = 8
NORM_ROWS = 128
XA_SUB_ROWS = 512
FFN_SUB_ROWS = 512
IN_PROJ_SUB_ROWS = 128


def _compiler_params(n_axes, vmem_bytes):
    limit = min(int(vmem_bytes * 1.25) + (4 << 20), V7X_VMEM_BYTES * 7 // 8)
    return pltpu.CompilerParams(dimension_semantics=("arbitrary",) * n_axes, vmem_limit_bytes=limit)


def _dot(a, b):
    return jnp.dot(a, b, preferred_element_type=F32)


def _dot_nt(a, b):
    return lax.dot_general(a, b, (((1,), (1,)), ((), ())), preferred_element_type=F32)


def _silu(x):
    return x * (1.0 / (1.0 + jnp.exp(-x)))


def _rms_rows(x, g):
    ms = jnp.mean(x * x, axis=-1, keepdims=True)
    return x * lax.rsqrt(ms + EPS) * g


def _rows_shifted_down(a, prev, s):
    n = prev.shape[0]
    rows = lax.broadcasted_iota(jnp.int32, prev.shape, 0)
    rolled = [pltpu.roll(piece, s, axis=0) for piece in [prev] + [a[r:r + n, :] for r in range(0, a.shape[0], n)]]
    return jnp.concatenate([jnp.where(rows < s, lo, hi) for lo, hi in zip(rolled[:-1], rolled[1:])], axis=0)


def _norm_rows_into(h_ref, x_ref, g_ref):
    g = g_ref[...]

    def body(r, c):
        rows = pl.ds(pl.multiple_of(r * NORM_ROWS, NORM_ROWS), NORM_ROWS)
        h_ref[rows, :] = _rms_rows(x_ref[rows, :], g).astype(h_ref.dtype)
        return c

    lax.fori_loop(0, x_ref.shape[0] // NORM_ROWS, body, 0)


def _norm_matmul_body(x_ref, g_ref, w_ref, o_ref, h_ref):
    @pl.when((pl.program_id(1) == 0) & (pl.program_id(2) == 0))
    def _():
        _norm_rows_into(h_ref, x_ref, g_ref)

    o_ref[...] = _dot(h_ref[...], w_ref[...]).astype(o_ref.dtype)


def norm_matmul(x, g, w, out_dtype, tm, tn):
    R, K = x.shape
    L, _, N = w.shape
    osz = jnp.dtype(out_dtype).itemsize
    vmem = 2 * tm * K * 4 + tm * K * 2 + 2 * K * tn * 2 + 2 * tm * tn * osz + tm * tn * 4
    return pl.pallas_call(
        _norm_matmul_body,
        grid=(R // tm, L, N // tn),
        in_specs=[
            pl.BlockSpec((tm, K), lambda i, l, j: (i, 0)),
            pl.BlockSpec((1, K), lambda i, l, j: (0, 0)),
            pl.BlockSpec((None, K, tn), lambda i, l, j: (l, 0, j)),
        ],
        out_specs=pl.BlockSpec((None, tm, tn), lambda i, l, j: (l, i, j)),
        out_shape=jax.ShapeDtypeStruct((L, R, N), out_dtype),
        scratch_shapes=[pltpu.VMEM((tm, K), BF16)],
        compiler_params=_compiler_params(3, vmem),
        name="norm_matmul",
    )(x, g, w)


def _in_proj_body(x_ref, g_ref, w_ref, wg_ref, cw_ref, p_ref, gate_ref, h_ref, carry, *,
                  n_main, tiles_per_seq, conv0, tiles_per_part):
    i = pl.program_id(0)
    j = pl.program_id(1)
    tm, tn = p_ref.shape
    K = cw_ref.shape[0]
    halo = V7X_SUBLANES

    @pl.when(j == 0)
    def _():
        _norm_rows_into(h_ref, x_ref, g_ref)

    is_conv = (j >= conv0) & (j < conv0 + 3 * tiles_per_part)

    @pl.when(j == n_main)
    def _():
        gate_ref[...] = _dot(h_ref[...], wg_ref[...])

    @pl.when(jnp.logical_not(is_conv) & (j < n_main))
    def _():
        p_ref[...] = _dot(h_ref[...], w_ref[...])

    @pl.when(is_conv)
    def _():
        cj = j - conv0
        part = cj // tiles_per_part
        q_scale = jnp.where(part == 0, DN_HEAD_DIM ** -0.5, 1.0)
        prev = jnp.where((i % tiles_per_seq) == 0, 0.0, carry[cj])
        sub = min(IN_PROJ_SUB_ROWS, tm)
        for r0 in range(0, tm, sub):
            acc = _dot(h_ref[r0:r0 + sub, :], w_ref[...])
            y = acc * cw_ref[K - 1:K, :]
            for s in range(1, K):
                y = y + _rows_shifted_down(acc, prev, s) * cw_ref[K - 1 - s:K - s, :]
            prev = acc[sub - halo:sub, :]
            y = _silu(y)
            for c in range(0, tn, DN_HEAD_DIM):
                yh = y[:, c:c + DN_HEAD_DIM]
                inv = lax.rsqrt(jnp.sum(yh * yh, axis=-1, keepdims=True) + EPS) * q_scale
                p_ref[r0:r0 + sub, c:c + DN_HEAD_DIM] = yh * jnp.where(part == 2, 1.0, inv)
        carry[cj] = prev


def in_proj(x, g, w, w_gates, conv_w, l, S, tm, tn, conv_col0):
    R, D = x.shape
    N = w.shape[2]
    NG = w_gates.shape[2]
    K, conv_cols = conv_w.shape[1:]
    assert N % tn == 0 and conv_col0 % tn == 0 and (conv_cols // 3) % tn == 0 and tn % DN_HEAD_DIM == 0
    n_main, conv0, n_conv = N // tn, conv_col0 // tn, conv_cols // tn
    main = lambda j: jnp.minimum(j, n_main - 1)
    vmem = (2 * tm * D * 4 + tm * D * 2 + 2 * D * tn * 2 + 2 * D * NG * 2 + 3 * tm * tn * 4 + 2 * tm * NG * 4
            + n_conv * V7X_SUBLANES * tn * 4 + 8 * IN_PROJ_SUB_ROWS * tn * 4)
    return pl.pallas_call(
        functools.partial(_in_proj_body, n_main=n_main, tiles_per_seq=S // tm, conv0=conv0,
                          tiles_per_part=n_conv // 3),
        grid=(R // tm, n_main + 1),
        in_specs=[
            pl.BlockSpec((tm, D), lambda i, j: (i, 0)),
            pl.BlockSpec((1, D), lambda i, j: (0, 0)),
            pl.BlockSpec((None, D, tn), lambda i, j: (l, 0, main(j))),
            pl.BlockSpec((None, D, NG), lambda i, j: (l, 0, 0)),
            pl.BlockSpec((None, K, tn), lambda i, j: (l, 0, jnp.clip(j - conv0, 0, n_conv - 1))),
        ],
        out_specs=[
            pl.BlockSpec((tm, tn), lambda i, j: (i, main(j))),
            pl.BlockSpec((tm, NG), lambda i, j: (i, 0)),
        ],
        out_shape=[jax.ShapeDtypeStruct((R, N), F32), jax.ShapeDtypeStruct((R, NG), F32)],
        scratch_shapes=[
            pltpu.VMEM((tm, D), BF16),
            pltpu.VMEM((n_conv, V7X_SUBLANES, tn), F32),
        ],
        compiler_params=_compiler_params(2, vmem),
        name="in_proj",
    )(x, g, w, w_gates, conv_w)


def _pool_body(u_ref, wp_ref, sc_ref, o_ref, buf):
    s = pl.program_id(1)
    ts = u_ref.shape[0]
    G = wp_ref.shape[1]

    @pl.when(s == 0)
    def _():
        buf[0:POOL_HALO, :] = jnp.zeros((POOL_HALO, buf.shape[1]), F32)

    buf[POOL_HALO:POOL_HALO + ts, :] = u_ref[...]
    pos = (s * ts + 1 + lax.broadcasted_iota(jnp.int32, (ts, 1), 0)).astype(F32)
    for i, w in enumerate(POOL_WINDOWS):
        cols = slice(i * G, (i + 1) * G)
        u = buf[POOL_HALO:POOL_HALO + ts, cols]
        acc = u
        for k in range(1, w):
            acc = acc + buf[POOL_HALO - k:POOL_HALO - k + ts, cols]
        mixed = acc / jnp.minimum(pos, float(w)) - u
        y = _dot(mixed.astype(BF16), wp_ref[i]) * sc_ref[:, cols]
        o_ref[:, cols] = y.astype(o_ref.dtype)
    buf[0:POOL_HALO, :] = buf[ts:ts + POOL_HALO, :]


def pool_mixer(p, w_pool, pool_scale, l, B, S, ts):
    n_win, G, _ = w_pool.shape[1:]
    W = n_win * G
    nt = S // ts
    vmem = 2 * ts * W * 4 + (ts + POOL_HALO) * W * 4 + 2 * n_win * G * G * 2 + 2 * ts * W * 2 + 4 * ts * G * 4
    return pl.pallas_call(
        _pool_body,
        grid=(B, nt),
        in_specs=[
            pl.BlockSpec((ts, W), lambda b, s: (b * nt + s, 0)),
            pl.BlockSpec((None, n_win, G, G), lambda b, s: (l, 0, 0, 0)),
            pl.BlockSpec((None, 1, W), lambda b, s: (l, 0, 0)),
        ],
        out_specs=pl.BlockSpec((ts, W), lambda b, s: (b * nt + s, 0)),
        out_shape=jax.ShapeDtypeStruct((B * S, W), BF16),
        scratch_shapes=[pltpu.VMEM((ts + POOL_HALO, W), F32)],
        compiler_params=_compiler_params(2, vmem),
        name="pool_mixer",
    )(p, w_pool, pool_scale)


def _unit_lower_inverses(ls, eye, ii, jj):
    C = ls[0].shape[0]
    size = 2
    same = (ii // size) == (jj // size)
    xs = [eye - jnp.where(same, l_mat, 0.0) for l_mat in ls]
    while size < C:
        inner, same = same, (ii // (2 * size)) == (jj // (2 * size))
        offs = [jnp.where(same & jnp.logical_not(inner), l_mat, 0.0).astype(BF16) for l_mat in ls]
        xbs = [x.astype(BF16) for x in xs]
        ys = [_dot(off, xb) for off, xb in zip(offs, xbs)]
        xs = [x - _dot(xb, y.astype(BF16)) for x, xb, y in zip(xs, xbs, ys)]
        size *= 2
    return xs


def _dn_body(q_ref, k_ref, v_ref, z_ref, ba_ref, alog_ref, dtb_ref, ng_ref, o_ref,
             state, beta_rep, gc_rep, grow_s, cdec_s, u_s, w_s, qd_s, at_s, kdt_s):
    s = pl.program_id(1)
    ts, W = q_ref.shape
    H = W // DN_HEAD_DIM
    C = DN_CHUNK
    n_chunk = ts // C

    @pl.when(s == 0)
    def _():
        state[...] = jnp.zeros(state.shape, F32)

    ba = ba_ref[...]
    beta = 1.0 / (1.0 + jnp.exp(-ba))
    xs = ba + dtb_ref[...]
    softplus = jnp.maximum(xs, 0.0) + jnp.log1p(jnp.exp(-jnp.abs(xs)))
    g = -jnp.exp(alog_ref[...]) * softplus
    row_in_chunk = lax.broadcasted_iota(jnp.int32, (ts, V7X_LANES), 0) % C
    gc = g
    sh = 1
    while sh < C:
        gc = gc + jnp.where(row_in_chunk >= sh, pltpu.roll(gc, sh, axis=0), 0.0)
        sh *= 2
    gct = gc.T
    lane = lax.broadcasted_iota(jnp.int32, (ts, V7X_LANES), 1)
    for h in range(H):
        beta_rep[h] = jnp.broadcast_to(
            jnp.sum(jnp.where(lane == h, beta, 0.0), axis=-1, keepdims=True), (ts, V7X_LANES))
        gc_rep[h] = jnp.broadcast_to(
            jnp.sum(jnp.where(lane == H + h, gc, 0.0), axis=-1, keepdims=True), (ts, V7X_LANES))
        grow_s[h] = gct[H + h:H + h + 1, :]

    ii = lax.broadcasted_iota(jnp.int32, (C, C), 0)
    jj = lax.broadcasted_iota(jnp.int32, (C, C), 1)
    eye = (ii == jj).astype(F32)

    def head_group_body(hg, carry):
        pairs = []
        for hh in range(DN_HEADS_PER_STEP):
            h = hg * DN_HEADS_PER_STEP + hh
            cols = pl.ds(pl.multiple_of(h * DN_HEAD_DIM, DN_HEAD_DIM), DN_HEAD_DIM)
            bcol, gcol, grow_all = beta_rep[h], gc_rep[h], grow_s[h]
            egc = jnp.exp(gcol)
            for c in range(n_chunk):
                r = slice(c * C, (c + 1) * C)
                pairs.append(dict(h=h, c=c, r=r, cols=cols, q=q_ref[r, cols], k=k_ref[r, cols], v=v_ref[r, cols],
                                  b=bcol[r], g=gcol[r], e=egc[r], grow=grow_all[:, r]))
        kbs = [p["k"].astype(BF16) for p in pairs]
        decs = [jnp.exp(jnp.minimum(p["g"] - p["grow"], 0.0)) for p in pairs]
        kks = [_dot_nt(kb, kb) for kb in kbs]
        qks = [_dot_nt(p["q"].astype(BF16), kb) for p, kb in zip(pairs, kbs)]
        ls = [jnp.where(ii > jj, kk * dec, 0.0) * p["b"] for kk, dec, p in zip(kks, decs, pairs)]
        for p, qk, dec in zip(pairs, qks, decs):
            r, cols = p["r"], p["cols"]
            at_s[r, cols] = jnp.where(ii >= jj, qk * dec, 0.0).astype(BF16)
            qd_s[r, cols] = (p["q"] * p["e"]).astype(BF16)
            kd = p["k"] * jnp.exp(p["g"][C - 1:C, :] - p["g"])
            kdt_s[r, cols] = kd.T.astype(BF16)
            cdec_s[p["c"], p["h"]] = p["e"][C - 1:C, :]
        t_invs = _unit_lower_inverses(ls, eye, ii, jj)
        rhs = [jnp.concatenate([p["v"] * p["b"], p["k"] * (p["b"] * p["e"])], axis=1) for p in pairs]
        sols = [_dot(t_inv.astype(BF16), b.astype(BF16)) for t_inv, b in zip(t_invs, rhs)]
        for p, sol in zip(pairs, sols):
            u_s[p["r"], p["cols"]] = sol[:, :DN_HEAD_DIM]
            w_s[p["r"], p["cols"]] = sol[:, DN_HEAD_DIM:].astype(BF16)
        return carry

    lax.fori_loop(0, H // DN_HEADS_PER_STEP, head_group_body, 0)

    def chunk_body(c, carry):
        r0 = pl.multiple_of(c * C, C)
        rows = pl.ds(r0, C)
        hs = range(H)
        cols = [slice(h * DN_HEAD_DIM, (h + 1) * DN_HEAD_DIM) for h in hs]
        sts = [state[h] for h in hs]
        sbs = [st.astype(BF16) for st in sts]
        wss = [_dot(w_s[rows, cols[h]], sbs[h]) for h in hs]
        qss = [_dot(qd_s[rows, cols[h]], sbs[h]) for h in hs]
        vbs = [(u_s[rows, cols[h]] - wss[h]).astype(BF16) for h in hs]
        avs = [_dot(at_s[rows, cols[h]], vbs[h]) for h in hs]
        kvs = [_dot(kdt_s[rows, cols[h]], vbs[h]) for h in hs]
        for h in hs:
            state[h] = sts[h] * cdec_s[c, h] + kvs[h]
            gated = _rms_rows(qss[h] + avs[h], ng_ref[...]) * _silu(z_ref[rows, cols[h]])
            o_ref[rows, cols[h]] = gated.astype(o_ref.dtype)
        return carry

    lax.fori_loop(0, n_chunk, chunk_body, 0)


def gated_delta_net(p, gates, a_log, dt_bias, norm_g, l, B, S, ts, col_block0, H):
    W = H * DN_HEAD_DIM
    nt = S // ts
    NB = gates.shape[1]
    C = DN_CHUNK
    vmem = (2 * 4 * ts * W * 4 + 2 * ts * NB * 4 + 2 * ts * W * 2
            + H * DN_HEAD_DIM * DN_HEAD_DIM * 4 + 2 * H * ts * V7X_LANES * 4 + H * V7X_SUBLANES * ts * 4
            + ts * W * 4 + 4 * ts * W * 2 + 48 * C * C * 4 * DN_HEADS_PER_STEP)

    def pblock(k):
        return pl.BlockSpec((ts, W), lambda b, s: (b * nt + s, col_block0 + k))

    return pl.pallas_call(
        _dn_body,
        grid=(B, nt),
        in_specs=[
            pblock(0), pblock(1), pblock(2), pblock(3),
            pl.BlockSpec((ts, NB), lambda b, s: (b * nt + s, 0)),
            pl.BlockSpec((None, 1, NB), lambda b, s: (l, 0, 0)),
            pl.BlockSpec((None, 1, NB), lambda b, s: (l, 0, 0)),
            pl.BlockSpec((None, 1, DN_HEAD_DIM), lambda b, s: (l, 0, 0)),
        ],
        out_specs=pl.BlockSpec((ts, W), lambda b, s: (b * nt + s, 0)),
        out_shape=jax.ShapeDtypeStruct((B * S, W), BF16),
        scratch_shapes=[
            pltpu.VMEM((H, DN_HEAD_DIM, DN_HEAD_DIM), F32),
            pltpu.VMEM((H, ts, V7X_LANES), F32),
            pltpu.VMEM((H, ts, V7X_LANES), F32),
            pltpu.VMEM((H, 1, ts), F32),
            pltpu.VMEM((ts // C, H, 1, V7X_LANES), F32),
            pltpu.VMEM((ts, W), F32),
            pltpu.VMEM((ts, W), BF16),
            pltpu.VMEM((ts, W), BF16),
            pltpu.VMEM((ts, W), BF16),
            pltpu.VMEM((ts, W), BF16),
        ],
        compiler_params=_compiler_params(2, vmem),
        name="gated_delta_net",
    )(p, p, p, p, gates, a_log, dt_bias, norm_g)


def _mix_out_body(x_ref, yp_ref, yd_ref, w1_ref, w2_ref, o_ref):
    o_ref[...] = x_ref[...] + _dot(yp_ref[...], w1_ref[...]) + _dot(yd_ref[...], w2_ref[...])


def mix_out(x, y_pool, y_dn, w, l, tm, tn):
    R, D = x.shape
    K1, K2 = y_pool.shape[1], y_dn.shape[1]
    assert K1 == K2
    vmem = 2 * (2 * tm * tn * 4 + tm * (K1 + K2) * 2 + (K1 + K2) * tn * 2) + 2 * tm * tn * 4
    return pl.pallas_call(
        _mix_out_body,
        grid=(R // tm, D // tn),
        in_specs=[
            pl.BlockSpec((tm, tn), lambda i, j: (i, j)),
            pl.BlockSpec((tm, K1), lambda i, j: (i, 0)),
            pl.BlockSpec((tm, K2), lambda i, j: (i, 0)),
            pl.BlockSpec((None, K1, tn), lambda i, j: (l, 0, j)),
            pl.BlockSpec((None, K2, tn), lambda i, j: (l, 1, j)),
        ],
        out_specs=pl.BlockSpec((tm, tn), lambda i, j: (i, j)),
        out_shape=jax.ShapeDtypeStruct((R, D), F32),
        compiler_params=_compiler_params(2, vmem),
        name="mix_out",
    )(x, y_pool, y_dn, w, w)


def _xattn_body(x_ref, g_ref, wq_ref, k_ref, v_ref, wo_ref, o_ref, h_ref, a_ref):
    j = pl.program_id(1)
    dh = wq_ref.shape[1] // XA_HEADS_PER_STEP
    tn = wo_ref.shape[1]
    n_att = XA_HEADS // XA_HEADS_PER_STEP

    @pl.when(j == 0)
    def _():
        _norm_rows_into(h_ref, x_ref, g_ref)

    @pl.when(j < n_att)
    def _():
        tm = h_ref.shape[0]
        sub = min(XA_SUB_ROWS, tm)
        col0 = pl.multiple_of(j * (XA_HEADS_PER_STEP * dh), XA_HEADS_PER_STEP * dh)
        units = [(hh, r0) for hh in range(XA_HEADS_PER_STEP) for r0 in range(0, tm, sub)]

        def scores(unit):
            hh, r0 = unit
            q = _dot(h_ref[r0:r0 + sub, :], wq_ref[:, hh * dh:(hh + 1) * dh])
            return _dot_nt(q.astype(BF16), k_ref[:, hh * dh:(hh + 1) * dh]) * (dh ** -0.5)

        sc = scores(units[0])
        for n, (hh, r0) in enumerate(units):
            sc_next = scores(units[n + 1]) if n + 1 < len(units) else None
            e = jnp.exp(sc - jnp.max(sc, axis=-1, keepdims=True))
            pr = e / jnp.sum(e, axis=-1, keepdims=True)
            o = _dot(pr.astype(BF16), v_ref[:, hh * dh:(hh + 1) * dh])
            a_ref[r0:r0 + sub, pl.ds(col0 + hh * dh, dh)] = o.astype(BF16)
            sc = sc_next

    @pl.when(j >= n_att)
    def _():
        cols = pl.ds(pl.multiple_of((j - n_att) * tn, tn), tn)
        o_ref[...] = x_ref[:, cols] + _dot(a_ref[...], wo_ref[...])


def cross_attention(x, g, w_xq, kv, w_xo, l, B, S, M, tm, tn):
    R, D = x.shape
    n_att = XA_HEADS // XA_HEADS_PER_STEP
    wh = D // n_att
    tiles_per_batch = S // tm
    att = lambda j: jnp.minimum(j, n_att - 1)
    otile = lambda j: jnp.maximum(j - n_att, 0)
    vmem = (2 * tm * D * 4 + 2 * tm * D * 2 + 4 * D * wh * 2 + 4 * M * wh * 2 + 4 * D * tn * 2 + 3 * tm * tn * 4
            + 3 * tm * wh * 4 + 3 * tm * M * 4)
    return pl.pallas_call(
        _xattn_body,
        grid=(R // tm, n_att + D // tn),
        in_specs=[
            pl.BlockSpec((tm, D), lambda i, j: (i, 0)),
            pl.BlockSpec((None, 1, D), lambda i, j: (l, 0, 0)),
            pl.BlockSpec((None, D, wh), lambda i, j: (l, 0, att(j))),
            pl.BlockSpec((None, M, wh), lambda i, j: (l, i // tiles_per_batch, att(j))),
            pl.BlockSpec((None, M, wh), lambda i, j: (l, i // tiles_per_batch, n_att + att(j))),
            pl.BlockSpec((None, D, tn), lambda i, j: (l, 0, otile(j))),
        ],
        out_specs=pl.BlockSpec((tm, tn), lambda i, j: (i, otile(j))),
        out_shape=jax.ShapeDtypeStruct((R, D), F32),
        scratch_shapes=[pltpu.VMEM((tm, D), BF16), pltpu.VMEM((tm, D), BF16)],
        compiler_params=_compiler_params(2, vmem),
        name="cross_attention",
    )(x, g, w_xq, kv, kv, w_xo)


def _ffn_body(x_ref, g_ref, wg_ref, wu_ref, cw_ref, cb_ref, wd_ref, og_ref, o_ref, h_ref, gbuf, carry, *,
              tiles_per_seq, norm_output):
    i = pl.program_id(0)
    j = pl.program_id(1)
    tm = x_ref.shape[0]
    K = cw_ref.shape[0]
    halo = V7X_SUBLANES

    @pl.when(j == 0)
    def _():
        _norm_rows_into(h_ref, x_ref, g_ref)
        o_ref[...] = x_ref[...]

    gbuf[0:halo, :] = jnp.where((i % tiles_per_seq) == 0, 0.0, carry[j])
    sub = min(FFN_SUB_ROWS, tm)

    def gate_up(r0):
        h = h_ref[r0:r0 + sub, :]
        return _dot(h, wg_ref[...]), _dot(h, wu_ref[...])

    gate, up = gate_up(0)
    for r0 in range(0, tm, sub):
        nxt = gate_up(r0 + sub) if r0 + sub < tm else None
        gbuf[halo + r0:halo + r0 + sub, :] = gate
        conv = gate * cw_ref[K - 1:K, :] + cb_ref[...]
        for t in range(K - 2, -1, -1):
            conv = conv + gbuf[halo - K + 1 + t + r0:halo - K + 1 + t + r0 + sub, :] * cw_ref[t:t + 1, :]
        act = _silu(conv) * up
        o_ref[r0:r0 + sub, :] += _dot(act.astype(BF16), wd_ref[...])
        if nxt is not None:
            gate, up = nxt
    carry[j] = gbuf[tm:tm + halo, :]

    if norm_output:
        @pl.when(j == pl.num_programs(1) - 1)
        def _():
            _norm_rows_into(o_ref, o_ref, og_ref)


def conv_glu_ffn(x, g, w_gate, w_up, conv_w, conv_b, w_down, out_g, l, S, tm, tf, norm_output):
    R, D = x.shape
    F = w_gate.shape[2]
    K = conv_w.shape[1]
    nf = F // tf
    vmem = (4 * tm * D * 4 + tm * D * 2 + 2 * 3 * D * tf * 2 + (tm + V7X_SUBLANES) * tf * 4
            + nf * V7X_SUBLANES * tf * 4 + 4 * tm * tf * 4)
    return pl.pallas_call(
        functools.partial(_ffn_body, tiles_per_seq=S // tm, norm_output=norm_output),
        grid=(R // tm, nf),
        in_specs=[
            pl.BlockSpec((tm, D), lambda i, j: (i, 0)),
            pl.BlockSpec((None, 1, D), lambda i, j: (l, 0, 0)),
            pl.BlockSpec((None, D, tf), lambda i, j: (l, 0, j)),
            pl.BlockSpec((None, D, tf), lambda i, j: (l, 0, j)),
            pl.BlockSpec((None, K, tf), lambda i, j: (l, 0, j)),
            pl.BlockSpec((None, 1, tf), lambda i, j: (l, 0, j)),
            pl.BlockSpec((None, tf, D), lambda i, j: (l, j, 0)),
            pl.BlockSpec((1, D), lambda i, j: (0, 0)),
        ],
        out_specs=pl.BlockSpec((tm, D), lambda i, j: (i, 0)),
        out_shape=jax.ShapeDtypeStruct((R, D), F32),
        scratch_shapes=[
            pltpu.VMEM((tm, D), BF16),
            pltpu.VMEM((tm + V7X_SUBLANES, tf), F32),
            pltpu.VMEM((nf, V7X_SUBLANES, tf), F32),
        ],
        compiler_params=_compiler_params(2, vmem),
        name="conv_glu_ffn",
    )(x, g, w_gate, w_up, conv_w, conv_b, w_down, out_g)


def _tiles(S):
    return dict(tm=min(1024, S), ts_pool=min(512, S), ts_dn=min(512, S))


def kernel(x, mem, mix_norm_g, w_in, w_pool, pool_scale, dn_conv_w, dn_a_log, dn_dt_bias, dn_norm_g, w_mix_out,
           xa_norm_g, mem_norm_g, w_xq, w_xkv, w_xo, ffn_norm_g, w_gate, w_up, ffn_conv_w, ffn_conv_b, w_down,
           final_norm_g):
    B, S, D = x.shape
    M = mem.shape[1]
    depth = w_in.shape[0]
    H = dn_a_log.shape[1]
    dn_w = H * DN_HEAD_DIM
    pool_w = w_pool.shape[1] * w_pool.shape[2]
    main = pool_w + 4 * dn_w
    assert pool_w == dn_w and w_in.shape[2] == main + 2 * H and 2 * H <= V7X_LANES
    t = _tiles(S)

    tn = 512
    w_in_b = w_in[:, :, :main].astype(BF16)
    w_gates_b = jnp.pad(w_in[:, :, main:].astype(BF16), ((0, 0), (0, 0), (0, V7X_LANES - 2 * H)))
    pad_gate = lambda a: jnp.pad(a, ((0, 0), (H, V7X_LANES - 2 * H)))[:, None, :]
    a_log_p, dt_bias_p = pad_gate(dn_a_log), pad_gate(dn_dt_bias)
    w_pool_b = w_pool.astype(BF16)
    w_mix_b = w_mix_out.astype(BF16)
    w_xq_b, w_xkv_b, w_xo_b = w_xq.astype(BF16), w_xkv.astype(BF16), w_xo.astype(BF16)
    w_gate_b, w_up_b, w_down_b = w_gate.astype(BF16), w_up.astype(BF16), w_down.astype(BF16)
    row = lambda a: a[:, None, :]

    xr = x.reshape(B * S, D)
    kv = norm_matmul(mem.reshape(B * M, D), mem_norm_g[None, :], w_xkv_b, BF16, tm=min(1024, B * M), tn=tn)
    for l in range(depth):
        p, gates = in_proj(xr, mix_norm_g[l][None, :], w_in_b, w_gates_b, dn_conv_w, l, S, tm=t["tm"], tn=tn,
                           conv_col0=pool_w)
        y_pool = pool_mixer(p, w_pool_b, row(pool_scale), l, B, S, t["ts_pool"])
        y_dn = gated_delta_net(p, gates, a_log_p, dt_bias_p, row(dn_norm_g), l, B, S, t["ts_dn"],
                               col_block0=pool_w // dn_w, H=H)
        xr = mix_out(xr, y_pool, y_dn, w_mix_b, l, tm=t["tm"], tn=1024)
        xr = cross_attention(xr, row(xa_norm_g), w_xq_b, kv, w_xo_b, l, B, S, M, tm=t["tm"], tn=2 * tn)
        xr = conv_glu_ffn(xr, row(ffn_norm_g), w_gate_b, w_up_b, ffn_conv_w, row(ffn_conv_b), w_down_b,
                          final_norm_g[None, :], l, S, tm=t["tm"], tf=tn, norm_output=(l == depth - 1))
    return xr.reshape(B, S, D)
```

```python
import functools

import jax
import jax.numpy as jnp
from jax import lax
from jax.experimental import pallas as pl
from jax.experimental.pallas import tpu as pltpu

F32 = jnp.float32
BF16 = jnp.bfloat16

EPS = 1e-6
POOL_WINDOWS = (2, 4, 8, 16)
POOL_HALO = 16
DN_HEAD_DIM = 128
DN_CHUNK = 128
DN_HEADS_PER_STEP = 8
XA_HEADS = 4
XA_HEADS_PER_STEP = 2

V7X_VMEM_BYTES = 64 * 1024 * 1024
V7X_LANES = 128
V7X_SUBLANES = 8
NORM_ROWS = 128
XA_SUB_ROWS = 512
FFN_SUB_ROWS = 512
IN_PROJ_SUB_ROWS = 128


def _compiler_params(n_axes, vmem_bytes):
    limit = min(int(vmem_bytes * 1.25) + (4 << 20), V7X_VMEM_BYTES * 7 // 8)
    return pltpu.CompilerParams(dimension_semantics=("arbitrary",) * n_axes, vmem_limit_bytes=limit)


def _dot(a, b):
    return jnp.dot(a, b, preferred_element_type=F32)


def _dot_nt(a, b):
    return lax.dot_general(a, b, (((1,), (1,)), ((), ())), preferred_element_type=F32)


def _silu(x):
    return x * (1.0 / (1.0 + jnp.exp(-x)))


def _rms_rows(x, g):
    ms = jnp.mean(x * x, axis=-1, keepdims=True)
    return x * lax.rsqrt(ms + EPS) * g


def _rows_shifted_down(a, prev, s):
    n = prev.shape[0]
    rows = lax.broadcasted_iota(jnp.int32, prev.shape, 0)
    rolled = [pltpu.roll(piece, s, axis=0) for piece in [prev] + [a[r:r + n, :] for r in range(0, a.shape[0], n)]]
    return jnp.concatenate([jnp.where(rows < s, lo, hi) for lo, hi in zip(rolled[:-1], rolled[1:])], axis=0)


def _norm_rows_into(h_ref, x_ref, g_ref):
    g = g_ref[...]

    def body(r, c):
        rows = pl.ds(pl.multiple_of(r * NORM_ROWS, NORM_ROWS), NORM_ROWS)
        h_ref[rows, :] = _rms_rows(x_ref[rows, :], g).astype(h_ref.dtype)
        return c

    lax.fori_loop(0, x_ref.shape[0] // NORM_ROWS, body, 0)


def _norm_matmul_body(x_ref, g_ref, w_ref, o_ref, h_ref):
    @pl.when((pl.program_id(1) == 0) & (pl.program_id(2) == 0))
    def _():
        _norm_rows_into(h_ref, x_ref, g_ref)

    o_ref[...] = _dot(h_ref[...], w_ref[...]).astype(o_ref.dtype)


def norm_matmul(x, g, w, out_dtype, tm, tn):
    R, K = x.shape
    L, _, N = w.shape
    osz = jnp.dtype(out_dtype).itemsize
    vmem = 2 * tm * K * 4 + tm * K * 2 + 2 * K * tn * 2 + 2 * tm * tn * osz + tm * tn * 4
    return pl.pallas_call(
        _norm_matmul_body,
        grid=(R // tm, L, N // tn),
        in_specs=[
            pl.BlockSpec((tm, K), lambda i, l, j: (i, 0)),
            pl.BlockSpec((1, K), lambda i, l, j: (0, 0)),
            pl.BlockSpec((None, K, tn), lambda i, l, j: (l, 0, j)),
        ],
        out_specs=pl.BlockSpec((None, tm, tn), lambda i, l, j: (l, i, j)),
        out_shape=jax.ShapeDtypeStruct((L, R, N), out_dtype),
        scratch_shapes=[pltpu.VMEM((tm, K), BF16)],
        compiler_params=_compiler_params(3, vmem),
        name="norm_matmul",
    )(x, g, w)


def _in_proj_body(x_ref, g_ref, w_ref, wg_ref, cw_ref, p_ref, gate_ref, h_ref, carry, *,
                  n_main, tiles_per_seq, conv0, tiles_per_part):
    i = pl.program_id(0)
    j = pl.program_id(1)
    tm, tn = p_ref.shape
    K = cw_ref.shape[0]
    halo = V7X_SUBLANES

    @pl.when(j == 0)
    def _():
        _norm_rows_into(h_ref, x_ref, g_ref)

    is_conv = (j >= conv0) & (j < conv0 + 3 * tiles_per_part)

    @pl.when(j == n_main)
    def _():
        gate_ref[...] = _dot(h_ref[...], wg_ref[...])

    @pl.when(jnp.logical_not(is_conv) & (j < n_main))
    def _():
        p_ref[...] = _dot(h_ref[...], w_ref[...])

    @pl.when(is_conv)
    def _():
        cj = j - conv0
        part = cj // tiles_per_part
        q_scale = jnp.where(part == 0, DN_HEAD_DIM ** -0.5, 1.0)
        prev = jnp.where((i % tiles_per_seq) == 0, 0.0, carry[cj])
        sub = min(IN_PROJ_SUB_ROWS, tm)
        for r0 in range(0, tm, sub):
            acc = _dot(h_ref[r0:r0 + sub, :], w_ref[...])
            y = acc * cw_ref[K - 1:K, :]
            for s in range(1, K):
                y = y + _rows_shifted_down(acc, prev, s) * cw_ref[K - 1 - s:K - s, :]
            prev = acc[sub - halo:sub, :]
            y = _silu(y)
            for c in range(0, tn, DN_HEAD_DIM):
                yh = y[:, c:c + DN_HEAD_DIM]
                inv = lax.rsqrt(jnp.sum(yh * yh, axis=-1, keepdims=True) + EPS) * q_scale
                p_ref[r0:r0 + sub, c:c + DN_HEAD_DIM] = yh * jnp.where(part == 2, 1.0, inv)
        carry[cj] = prev


def in_proj(x, g, w, w_gates, conv_w, l, S, tm, tn, conv_col0):
    R, D = x.shape
    N = w.shape[2]
    NG = w_gates.shape[2]
    K, conv_cols = conv_w.shape[1:]
    assert N % tn == 0 and conv_col0 % tn == 0 and (conv_cols // 3) % tn == 0 and tn % DN_HEAD_DIM == 0
    n_main, conv0, n_conv = N // tn, conv_col0 // tn, conv_cols // tn
    main = lambda j: jnp.minimum(j, n_main - 1)
    vmem = (2 * tm * D * 4 + tm * D * 2 + 2 * D * tn * 2 + 2 * D * NG * 2 + 3 * tm * tn * 4 + 2 * tm * NG * 4
            + n_conv * V7X_SUBLANES * tn * 4 + 8 * IN_PROJ_SUB_ROWS * tn * 4)
    return pl.pallas_call(
        functools.partial(_in_proj_body, n_main=n_main, tiles_per_seq=S // tm, conv0=conv0,
                          tiles_per_part=n_conv // 3),
        grid=(R // tm, n_main + 1),
        in_specs=[
            pl.BlockSpec((tm, D), lambda i, j: (i, 0)),
            pl.BlockSpec((1, D), lambda i, j: (0, 0)),
            pl.BlockSpec((None, D, tn), lambda i, j: (l, 0, main(j))),
            pl.BlockSpec((None, D, NG), lambda i, j: (l, 0, 0)),
            pl.BlockSpec((None, K, tn), lambda i, j: (l, 0, jnp.clip(j - conv0, 0, n_conv - 1))),
        ],
        out_specs=[
            pl.BlockSpec((tm, tn), lambda i, j: (i, main(j))),
            pl.BlockSpec((tm, NG), lambda i, j: (i, 0)),
        ],
        out_shape=[jax.ShapeDtypeStruct((R, N), F32), jax.ShapeDtypeStruct((R, NG), F32)],
        scratch_shapes=[
            pltpu.VMEM((tm, D), BF16),
            pltpu.VMEM((n_conv, V7X_SUBLANES, tn), F32),
        ],
        compiler_params=_compiler_params(2, vmem),
        name="in_proj",
    )(x, g, w, w_gates, conv_w)


def _pool_body(u_ref, wp_ref, sc_ref, o_ref, buf):
    s = pl.program_id(1)
    ts = u_ref.shape[0]
    G = wp_ref.shape[1]

    @pl.when(s == 0)
    def _():
        buf[0:POOL_HALO, :] = jnp.zeros((POOL_HALO, buf.shape[1]), F32)

    buf[POOL_HALO:POOL_HALO + ts, :] = u_ref[...]
    pos = (s * ts + 1 + lax.broadcasted_iota(jnp.int32, (ts, 1), 0)).astype(F32)
    for i, w in enumerate(POOL_WINDOWS):
        cols = slice(i * G, (i + 1) * G)
        u = buf[POOL_HALO:POOL_HALO + ts, cols]
        acc = u
        for k in range(1, w):
            acc = acc + buf[POOL_HALO - k:POOL_HALO - k + ts, cols]
        mixed = acc / jnp.minimum(pos, float(w)) - u
        y = _dot(mixed.astype(BF16), wp_ref[i]) * sc_ref[:, cols]
        o_ref[:, cols] = y.astype(o_ref.dtype)
    buf[0:POOL_HALO, :] = buf[ts:ts + POOL_HALO, :]


def pool_mixer(p, w_pool, pool_scale, l, B, S, ts):
    n_win, G, _ = w_pool.shape[1:]
    W = n_win * G
    nt = S // ts
    vmem = 2 * ts * W * 4 + (ts + POOL_HALO) * W * 4 + 2 * n_win * G * G * 2 + 2 * ts * W * 2 + 4 * ts * G * 4
    return pl.pallas_call(
        _pool_body,
        grid=(B, nt),
        in_specs=[
            pl.BlockSpec((ts, W), lambda b, s: (b * nt + s, 0)),
            pl.BlockSpec((None, n_win, G, G), lambda b, s: (l, 0, 0, 0)),
            pl.BlockSpec((None, 1, W), lambda b, s: (l, 0, 0)),
        ],
        out_specs=pl.BlockSpec((ts, W), lambda b, s: (b * nt + s, 0)),
        out_shape=jax.ShapeDtypeStruct((B * S, W), BF16),
        scratch_shapes=[pltpu.VMEM((ts + POOL_HALO, W), F32)],
        compiler_params=_compiler_params(2, vmem),
        name="pool_mixer",
    )(p, w_pool, pool_scale)


def _unit_lower_inverses(ls, eye, ii, jj):
    C = ls[0].shape[0]
    size = 2
    same = (ii // size) == (jj // size)
    xs = [eye - jnp.where(same, l_mat, 0.0) for l_mat in ls]
    while size < C:
        inner, same = same, (ii // (2 * size)) == (jj // (2 * size))
        offs = [jnp.where(same & jnp.logical_not(inner), l_mat, 0.0).astype(BF16) for l_mat in ls]
        xbs = [x.astype(BF16) for x in xs]
        ys = [_dot(off, xb) for off, xb in zip(offs, xbs)]
        xs = [x - _dot(xb, y.astype(BF16)) for x, xb, y in zip(xs, xbs, ys)]
        size *= 2
    return xs


def _dn_body(q_ref, k_ref, v_ref, z_ref, ba_ref, alog_ref, dtb_ref, ng_ref, o_ref,
             state, beta_rep, gc_rep, grow_s, cdec_s, u_s, w_s, qd_s, at_s, kdt_s):
    s = pl.program_id(1)
    ts, W = q_ref.shape
    H = W // DN_HEAD_DIM
    C = DN_CHUNK
    n_chunk = ts // C

    @pl.when(s == 0)
    def _():
        state[...] = jnp.zeros(state.shape, F32)

    ba = ba_ref[...]
    beta = 1.0 / (1.0 + jnp.exp(-ba))
    xs = ba + dtb_ref[...]
    softplus = jnp.maximum(xs, 0.0) + jnp.log1p(jnp.exp(-jnp.abs(xs)))
    g = -jnp.exp(alog_ref[...]) * softplus
    row_in_chunk = lax.broadcasted_iota(jnp.int32, (ts, V7X_LANES), 0) % C
    gc = g
    sh = 1
    while sh < C:
        gc = gc + jnp.where(row_in_chunk >= sh, pltpu.roll(gc, sh, axis=0), 0.0)
        sh *= 2
    gct = gc.T
    lane = lax.broadcasted_iota(jnp.int32, (ts, V7X_LANES), 1)
    for h in range(H):
        beta_rep[h] = jnp.broadcast_to(
            jnp.sum(jnp.where(lane == h, beta, 0.0), axis=-1, keepdims=True), (ts, V7X_LANES))
        gc_rep[h] = jnp.broadcast_to(
            jnp.sum(jnp.where(lane == H + h, gc, 0.0), axis=-1, keepdims=True), (ts, V7X_LANES))
        grow_s[h] = gct[H + h:H + h + 1, :]

    ii = lax.broadcasted_iota(jnp.int32, (C, C), 0)
    jj = lax.broadcasted_iota(jnp.int32, (C, C), 1)
    eye = (ii == jj).astype(F32)

    def head_group_body(hg, carry):
        pairs = []
        for hh in range(DN_HEADS_PER_STEP):
            h = hg * DN_HEADS_PER_STEP + hh
            cols = pl.ds(pl.multiple_of(h * DN_HEAD_DIM, DN_HEAD_DIM), DN_HEAD_DIM)
            bcol, gcol, grow_all = beta_rep[h], gc_rep[h], grow_s[h]
            egc = jnp.exp(gcol)
            for c in range(n_chunk):
                r = slice(c * C, (c + 1) * C)
                pairs.append(dict(h=h, c=c, r=r, cols=cols, q=q_ref[r, cols], k=k_ref[r, cols], v=v_ref[r, cols],
                                  b=bcol[r], g=gcol[r], e=egc[r], grow=grow_all[:, r]))
        kbs = [p["k"].astype(BF16) for p in pairs]
        decs = [jnp.exp(jnp.minimum(p["g"] - p["grow"], 0.0)) for p in pairs]
        kks = [_dot_nt(kb, kb) for kb in kbs]
        qks = [_dot_nt(p["q"].astype(BF16), kb) for p, kb in zip(pairs, kbs)]
        ls = [jnp.where(ii > jj, kk * dec, 0.0) * p["b"] for kk, dec, p in zip(kks, decs, pairs)]
        for p, qk, dec in zip(pairs, qks, decs):
            r, cols = p["r"], p["cols"]
            at_s[r, cols] = jnp.where(ii >= jj, qk * dec, 0.0).astype(BF16)
            qd_s[r, cols] = (p["q"] * p["e"]).astype(BF16)
            kd = p["k"] * jnp.exp(p["g"][C - 1:C, :] - p["g"])
            kdt_s[r, cols] = kd.T.astype(BF16)
            cdec_s[p["c"], p["h"]] = p["e"][C - 1:C, :]
        t_invs = _unit_lower_inverses(ls, eye, ii, jj)
        rhs = [jnp.concatenate([p["v"] * p["b"], p["k"] * (p["b"] * p["e"])], axis=1) for p in pairs]
        sols = [_dot(t_inv.astype(BF16), b.astype(BF16)) for t_inv, b in zip(t_invs, rhs)]
        for p, sol in zip(pairs, sols):
            u_s[p["r"], p["cols"]] = sol[:, :DN_HEAD_DIM]
            w_s[p["r"], p["cols"]] = sol[:, DN_HEAD_DIM:].astype(BF16)
        return carry

    lax.fori_loop(0, H // DN_HEADS_PER_STEP, head_group_body, 0)

    def chunk_body(c, carry):
        r0 = pl.multiple_of(c * C, C)
        rows = pl.ds(r0, C)
        hs = range(H)
        cols = [slice(h * DN_HEAD_DIM, (h + 1) * DN_HEAD_DIM) for h in hs]
        sts = [state[h] for h in hs]
        sbs = [st.astype(BF16) for st in sts]
        wss = [_dot(w_s[rows, cols[h]], sbs[h]) for h in hs]
        qss = [_dot(qd_s[rows, cols[h]], sbs[h]) for h in hs]
        vbs = [(u_s[rows, cols[h]] - wss[h]).astype(BF16) for h in hs]
        avs = [_dot(at_s[rows, cols[h]], vbs[h]) for h in hs]
        kvs = [_dot(kdt_s[rows, cols[h]], vbs[h]) for h in hs]
        for h in hs:
            state[h] = sts[h] * cdec_s[c, h] + kvs[h]
            gated = _rms_rows(qss[h] + avs[h], ng_ref[...]) * _silu(z_ref[rows, cols[h]])
            o_ref[rows, cols[h]] = gated.astype(o_ref.dtype)
        return carry

    lax.fori_loop(0, n_chunk, chunk_body, 0)


def gated_delta_net(p, gates, a_log, dt_bias, norm_g, l, B, S, ts, col_block0, H):
    W = H * DN_HEAD_DIM
    nt = S // ts
    NB = gates.shape[1]
    C = DN_CHUNK
    vmem = (2 * 4 * ts * W * 4 + 2 * ts * NB * 4 + 2 * ts * W * 2
            + H * DN_HEAD_DIM * DN_HEAD_DIM * 4 + 2 * H * ts * V7X_LANES * 4 + H * V7X_SUBLANES * ts * 4
            + ts * W * 4 + 4 * ts * W * 2 + 48 * C * C * 4 * DN_HEADS_PER_STEP)

    def pblock(k):
        return pl.BlockSpec((ts, W), lambda b, s: (b * nt + s, col_block0 + k))

    return pl.pallas_call(
        _dn_body,
        grid=(B, nt),
        in_specs=[
            pblock(0), pblock(1), pblock(2), pblock(3),
            pl.BlockSpec((ts, NB), lambda b, s: (b * nt + s, 0)),
            pl.BlockSpec((None, 1, NB), lambda b, s: (l, 0, 0)),
            pl.BlockSpec((None, 1, NB), lambda b, s: (l, 0, 0)),
            pl.BlockSpec((None, 1, DN_HEAD_DIM), lambda b, s: (l, 0, 0)),
        ],
        out_specs=pl.BlockSpec((ts, W), lambda b, s: (b * nt + s, 0)),
        out_shape=jax.ShapeDtypeStruct((B * S, W), BF16),
        scratch_shapes=[
            pltpu.VMEM((H, DN_HEAD_DIM, DN_HEAD_DIM), F32),
            pltpu.VMEM((H, ts, V7X_LANES), F32),
            pltpu.VMEM((H, ts, V7X_LANES), F32),
            pltpu.VMEM((H, 1, ts), F32),
            pltpu.VMEM((ts // C, H, 1, V7X_LANES), F32),
            pltpu.VMEM((ts, W), F32),
            pltpu.VMEM((ts, W), BF16),
            pltpu.VMEM((ts, W), BF16),
            pltpu.VMEM((ts, W), BF16),
            pltpu.VMEM((ts, W), BF16),
        ],
        compiler_params=_compiler_params(2, vmem),
        name="gated_delta_net",
    )(p, p, p, p, gates, a_log, dt_bias, norm_g)


def _mix_out_body(x_ref, yp_ref, yd_ref, w1_ref, w2_ref, o_ref):
    o_ref[...] = x_ref[...] + _dot(yp_ref[...], w1_ref[...]) + _dot(yd_ref[...], w2_ref[...])


def mix_out(x, y_pool, y_dn, w, l, tm, tn):
    R, D = x.shape
    K1, K2 = y_pool.shape[1], y_dn.shape[1]
    assert K1 == K2
    vmem = 2 * (2 * tm * tn * 4 + tm * (K1 + K2) * 2 + (K1 + K2) * tn * 2) + 2 * tm * tn * 4
    return pl.pallas_call(
        _mix_out_body,
        grid=(R // tm, D // tn),
        in_specs=[
            pl.BlockSpec((tm, tn), lambda i, j: (i, j)),
            pl.BlockSpec((tm, K1), lambda i, j: (i, 0)),
            pl.BlockSpec((tm, K2), lambda i, j: (i, 0)),
            pl.BlockSpec((None, K1, tn), lambda i, j: (l, 0, j)),
            pl.BlockSpec((None, K2, tn), lambda i, j: (l, 1, j)),
        ],
        out_specs=pl.BlockSpec((tm, tn), lambda i, j: (i, j)),
        out_shape=jax.ShapeDtypeStruct((R, D), F32),
        compiler_params=_compiler_params(2, vmem),
        name="mix_out",
    )(x, y_pool, y_dn, w, w)


def _xattn_body(x_ref, g_ref, wq_ref, k_ref, v_ref, wo_ref, o_ref, h_ref, a_ref):
    j = pl.program_id(1)
    dh = wq_ref.shape[1] // XA_HEADS_PER_STEP
    tn = wo_ref.shape[1]
    n_att = XA_HEADS // XA_HEADS_PER_STEP

    @pl.when(j == 0)
    def _():
        _norm_rows_into(h_ref, x_ref, g_ref)

    @pl.when(j < n_att)
    def _():
        tm = h_ref.shape[0]
        sub = min(XA_SUB_ROWS, tm)
        col0 = pl.multiple_of(j * (XA_HEADS_PER_STEP * dh), XA_HEADS_PER_STEP * dh)
        units = [(hh, r0) for hh in range(XA_HEADS_PER_STEP) for r0 in range(0, tm, sub)]

        def scores(unit):
            hh, r0 = unit
            q = _dot(h_ref[r0:r0 + sub, :], wq_ref[:, hh * dh:(hh + 1) * dh])
            return _dot_nt(q.astype(BF16), k_ref[:, hh * dh:(hh + 1) * dh]) * (dh ** -0.5)

        sc = scores(units[0])
        for n, (hh, r0) in enumerate(units):
            sc_next = scores(units[n + 1]) if n + 1 < len(units) else None
            e = jnp.exp(sc - jnp.max(sc, axis=-1, keepdims=True))
            pr = e / jnp.sum(e, axis=-1, keepdims=True)
            o = _dot(pr.astype(BF16), v_ref[:, hh * dh:(hh + 1) * dh])
            a_ref[r0:r0 + sub, pl.ds(col0 + hh * dh, dh)] = o.astype(BF16)
            sc = sc_next

    @pl.when(j >= n_att)
    def _():
        cols = pl.ds(pl.multiple_of((j - n_att) * tn, tn), tn)
        o_ref[...] = x_ref[:, cols] + _dot(a_ref[...], wo_ref[...])


def cross_attention(x, g, w_xq, kv, w_xo, l, B, S, M, tm, tn):
    R, D = x.shape
    n_att = XA_HEADS // XA_HEADS_PER_STEP
    wh = D // n_att
    tiles_per_batch = S // tm
    att = lambda j: jnp.minimum(j, n_att - 1)
    otile = lambda j: jnp.maximum(j - n_att, 0)
    vmem = (2 * tm * D * 4 + 2 * tm * D * 2 + 4 * D * wh * 2 + 4 * M * wh * 2 + 4 * D * tn * 2 + 3 * tm * tn * 4
            + 3 * tm * wh * 4 + 3 * tm * M * 4)
    return pl.pallas_call(
        _xattn_body,
        grid=(R // tm, n_att + D // tn),
        in_specs=[
            pl.BlockSpec((tm, D), lambda i, j: (i, 0)),
            pl.BlockSpec((None, 1, D), lambda i, j: (l, 0, 0)),
            pl.BlockSpec((None, D, wh), lambda i, j: (l, 0, att(j))),
            pl.BlockSpec((None, M, wh), lambda i, j: (l, i // tiles_per_batch, att(j))),
            pl.BlockSpec((None, M, wh), lambda i, j: (l, i // tiles_per_batch, n_att + att(j))),
            pl.BlockSpec((None, D, tn), lambda i, j: (l, 0, otile(j))),
        ],
        out_specs=pl.BlockSpec((tm, tn), lambda i, j: (i, otile(j))),
        out_shape=jax.ShapeDtypeStruct((R, D), F32),
        scratch_shapes=[pltpu.VMEM((tm, D), BF16), pltpu.VMEM((tm, D), BF16)],
        compiler_params=_compiler_params(2, vmem),
        name="cross_attention",
    )(x, g, w_xq, kv, kv, w_xo)


def _ffn_body(x_ref, g_ref, wg_ref, wu_ref, cw_ref, cb_ref, wd_ref, og_ref, o_ref, h_ref, gbuf, carry, *,
              tiles_per_seq, norm_output):
    i = pl.program_id(0)
    j = pl.program_id(1)
    tm = x_ref.shape[0]
    K = cw_ref.shape[0]
    halo = V7X_SUBLANES

    @pl.when(j == 0)
    def _():
        _norm_rows_into(h_ref, x_ref, g_ref)
        o_ref[...] = x_ref[...]

    gbuf[0:halo, :] = jnp.where((i % tiles_per_seq) == 0, 0.0, carry[j])
    sub = min(FFN_SUB_ROWS, tm)

    def gate_up(r0):
        h = h_ref[r0:r0 + sub, :]
        return _dot(h, wg_ref[...]), _dot(h, wu_ref[...])

    gate, up = gate_up(0)
    for r0 in range(0, tm, sub):
        nxt = gate_up(r0 + sub) if r0 + sub < tm else None
        gbuf[halo + r0:halo + r0 + sub, :] = gate
        conv = gate * cw_ref[K - 1:K, :] + cb_ref[...]
        for t in range(K - 2, -1, -1):
            conv = conv + gbuf[halo - K + 1 + t + r0:halo - K + 1 + t + r0 + sub, :] * cw_ref[t:t + 1, :]
        act = _silu(conv) * up
        o_ref[r0:r0 + sub, :] += _dot(act.astype(BF16), wd_ref[...])
        if nxt is not None:
            gate, up = nxt
    carry[j] = gbuf[tm:tm + halo, :]

    if norm_output:
        @pl.when(j == pl.num_programs(1) - 1)
        def _():
            _norm_rows_into(o_ref, o_ref, og_ref)


def conv_glu_ffn(x, g, w_gate, w_up, conv_w, conv_b, w_down, out_g, l, S, tm, tf, norm_output):
    R, D = x.shape
    F = w_gate.shape[2]
    K = conv_w.shape[1]
    nf = F // tf
    vmem = (4 * tm * D * 4 + tm * D * 2 + 2 * 3 * D * tf * 2 + (tm + V7X_SUBLANES) * tf * 4
            + nf * V7X_SUBLANES * tf * 4 + 4 * tm * tf * 4)
    return pl.pallas_call(
        functools.partial(_ffn_body, tiles_per_seq=S // tm, norm_output=norm_output),
        grid=(R // tm, nf),
        in_specs=[
            pl.BlockSpec((tm, D), lambda i, j: (i, 0)),
            pl.BlockSpec((None, 1, D), lambda i, j: (l, 0, 0)),
            pl.BlockSpec((None, D, tf), lambda i, j: (l, 0, j)),
            pl.BlockSpec((None, D, tf), lambda i, j: (l, 0, j)),
            pl.BlockSpec((None, K, tf), lambda i, j: (l, 0, j)),
            pl.BlockSpec((None, 1, tf), lambda i, j: (l, 0, j)),
            pl.BlockSpec((None, tf, D), lambda i, j: (l, j, 0)),
            pl.BlockSpec((1, D), lambda i, j: (0, 0)),
        ],
        out_specs=pl.BlockSpec((tm, D), lambda i, j: (i, 0)),
        out_shape=jax.ShapeDtypeStruct((R, D), F32),
        scratch_shapes=[
            pltpu.VMEM((tm, D), BF16),
            pltpu.VMEM((tm + V7X_SUBLANES, tf), F32),
            pltpu.VMEM((nf, V7X_SUBLANES, tf), F32),
        ],
        compiler_params=_compiler_params(2, vmem),
        name="conv_glu_ffn",
    )(x, g, w_gate, w_up, conv_w, conv_b, w_down, out_g)


def _tiles(S):
    return dict(tm=min(1024, S), ts_pool=min(512, S), ts_dn=min(512, S))


def kernel(x, mem, mix_norm_g, w_in, w_pool, pool_scale, dn_conv_w, dn_a_log, dn_dt_bias, dn_norm_g, w_mix_out,
           xa_norm_g, mem_norm_g, w_xq, w_xkv, w_xo, ffn_norm_g, w_gate, w_up, ffn_conv_w, ffn_conv_b, w_down,
           final_norm_g):
    B, S, D = x.shape
    M = mem.shape[1]
    depth = w_in.shape[0]
    H = dn_a_log.shape[1]
    dn_w = H * DN_HEAD_DIM
    pool_w = w_pool.shape[1] * w_pool.shape[2]
    main = pool_w + 4 * dn_w
    assert pool_w == dn_w and w_in.shape[2] == main + 2 * H and 2 * H <= V7X_LANES
    t = _tiles(S)

    tn = 512
    w_in_b = w_in[:, :, :main].astype(BF16)
    w_gates_b = jnp.pad(w_in[:, :, main:].astype(BF16), ((0, 0), (0, 0), (0, V7X_LANES - 2 * H)))
    pad_gate = lambda a: jnp.pad(a, ((0, 0), (H, V7X_LANES - 2 * H)))[:, None, :]
    a_log_p, dt_bias_p = pad_gate(dn_a_log), pad_gate(dn_dt_bias)
    w_pool_b = w_pool.astype(BF16)
    w_mix_b = w_mix_out.astype(BF16)
    w_xq_b, w_xkv_b, w_xo_b = w_xq.astype(BF16), w_xkv.astype(BF16), w_xo.astype(BF16)
    w_gate_b, w_up_b, w_down_b = w_gate.astype(BF16), w_up.astype(BF16), w_down.astype(BF16)
    row = lambda a: a[:, None, :]

    xr = x.reshape(B * S, D)
    kv = norm_matmul(mem.reshape(B * M, D), mem_norm_g[None, :], w_xkv_b, BF16, tm=min(1024, B * M), tn=tn)
    for l in range(depth):
        p, gates = in_proj(xr, mix_norm_g[l][None, :], w_in_b, w_gates_b, dn_conv_w, l, S, tm=t["tm"], tn=tn,
                           conv_col0=pool_w)
        y_pool = pool_mixer(p, w_pool_b, row(pool_scale), l, B, S, t["ts_pool"])
        y_dn = gated_delta_net(p, gates, a_log_p, dt_bias_p, row(dn_norm_g), l, B, S, t["ts_dn"],
                               col_block0=pool_w // dn_w, H=H)
        xr = mix_out(xr, y_pool, y_dn, w_mix_b, l, tm=t["tm"], tn=1024)
        xr = cross_attention(xr, row(xa_norm_g), w_xq_b, kv, w_xo_b, l, B, S, M, tm=t["tm"], tn=2 * tn)
        xr = conv_glu_ffn(xr, row(ffn_norm_g), w_gate_b, w_up_b, ffn_conv_w, row(ffn_conv_b), w_down_b,
                          final_norm_g[None, :], l, S, tm=t["tm"], tf=tn, norm_output=(l == depth - 1))
    return xr.reshape(B, S, D)
```

```python
import functools

import jax
import jax.numpy as jnp
from jax import lax
from jax.experimental import pallas as pl
from jax.experimental.pallas import tpu as pltpu

F32 = jnp.float32
BF16 = jnp.bfloat16

EPS = 1e-6
POOL_WINDOWS = (2, 4, 8, 16)
POOL_HALO = 16
DN_HEAD_DIM = 128
DN_CHUNK = 128
DN_HEADS_PER_STEP = 8
XA_HEADS = 4
XA_HEADS_PER_STEP = 2

V7X_VMEM_BYTES = 64 * 1024 * 1024
V7X_LANES = 128
V7X_SUBLANES = 8
NORM_ROWS = 128
XA_SUB_ROWS = 512
FFN_SUB_ROWS = 512
IN_PROJ_SUB_ROWS = 128


def _compiler_params(n_axes, vmem_bytes):
    limit = min(int(vmem_bytes * 1.25) + (4 << 20), V7X_VMEM_BYTES * 7 // 8)
    return pltpu.CompilerParams(dimension_semantics=("arbitrary",) * n_axes, vmem_limit_bytes=limit)


def _dot(a, b):
    return jnp.dot(a, b, preferred_element_type=F32)


def _dot_nt(a, b):
    return lax.dot_general(a, b, (((1,), (1,)), ((), ())), preferred_element_type=F32)


def _silu(x):
    return x * (1.0 / (1.0 + jnp.exp(-x)))


def _rms_rows(x, g):
    ms = jnp.mean(x * x, axis=-1, keepdims=True)
    return x * lax.rsqrt(ms + EPS) * g


def _rows_shifted_down(a, prev, s):
    n = prev.shape[0]
    rows = lax.broadcasted_iota(jnp.int32, prev.shape, 0)
    rolled = [pltpu.roll(piece, s, axis=0) for piece in [prev] + [a[r:r + n, :] for r in range(0, a.shape[0], n)]]
    return jnp.concatenate([jnp.where(rows < s, lo, hi) for lo, hi in zip(rolled[:-1], rolled[1:])], axis=0)


def _norm_rows_into(h_ref, x_ref, g_ref):
    g = g_ref[...]

    def body(r, c):
        rows = pl.ds(pl.multiple_of(r * NORM_ROWS, NORM_ROWS), NORM_ROWS)
        h_ref[rows, :] = _rms_rows(x_ref[rows, :], g).astype(h_ref.dtype)
        return c

    lax.fori_loop(0, x_ref.shape[0] // NORM_ROWS, body, 0)


def _norm_matmul_body(x_ref, g_ref, w_ref, o_ref, h_ref):
    @pl.when((pl.program_id(1) == 0) & (pl.program_id(2) == 0))
    def _():
        _norm_rows_into(h_ref, x_ref, g_ref)

    o_ref[...] = _dot(h_ref[...], w_ref[...]).astype(o_ref.dtype)


def norm_matmul(x, g, w, out_dtype, tm, tn):
    R, K = x.shape
    L, _, N = w.shape
    osz = jnp.dtype(out_dtype).itemsize
    vmem = 2 * tm * K * 4 + tm * K * 2 + 2 * K * tn * 2 + 2 * tm * tn * osz + tm * tn * 4
    return pl.pallas_call(
        _norm_matmul_body,
        grid=(R // tm, L, N // tn),
        in_specs=[
            pl.BlockSpec((tm, K), lambda i, l, j: (i, 0)),
            pl.BlockSpec((1, K), lambda i, l, j: (0, 0)),
            pl.BlockSpec((None, K, tn), lambda i, l, j: (l, 0, j)),
        ],
        out_specs=pl.BlockSpec((None, tm, tn), lambda i, l, j: (l, i, j)),
        out_shape=jax.ShapeDtypeStruct((L, R, N), out_dtype),
        scratch_shapes=[pltpu.VMEM((tm, K), BF16)],
        compiler_params=_compiler_params(3, vmem),
        name="norm_matmul",
    )(x, g, w)


def _in_proj_body(x_ref, g_ref, w_ref, wg_ref, cw_ref, p_ref, gate_ref, h_ref, carry, *,
                  n_main, tiles_per_seq, conv0, tiles_per_part):
    i = pl.program_id(0)
    j = pl.program_id(1)
    tm, tn = p_ref.shape
    K = cw_ref.shape[0]
    halo = V7X_SUBLANES

    @pl.when(j == 0)
    def _():
        _norm_rows_into(h_ref, x_ref, g_ref)

    is_conv = (j >= conv0) & (j < conv0 + 3 * tiles_per_part)

    @pl.when(j == n_main)
    def _():
        gate_ref[...] = _dot(h_ref[...], wg_ref[...])

    @pl.when(jnp.logical_not(is_conv) & (j < n_main))
    def _():
        p_ref[...] = _dot(h_ref[...], w_ref[...])

    @pl.when(is_conv)
    def _():
        cj = j - conv0
        part = cj // tiles_per_part
        q_scale = jnp.where(part == 0, DN_HEAD_DIM ** -0.5, 1.0)
        prev = jnp.where((i % tiles_per_seq) == 0, 0.0, carry[cj])
        sub = min(IN_PROJ_SUB_ROWS, tm)
        for r0 in range(0, tm, sub):
            acc = _dot(h_ref[r0:r0 + sub, :], w_ref[...])
            y = acc * cw_ref[K - 1:K, :]
            for s in range(1, K):
                y = y + _rows_shifted_down(acc, prev, s) * cw_ref[K - 1 - s:K - s, :]
            prev = acc[sub - halo:sub, :]
            y = _silu(y)
            for c in range(0, tn, DN_HEAD_DIM):
                yh = y[:, c:c + DN_HEAD_DIM]
                inv = lax.rsqrt(jnp.sum(yh * yh, axis=-1, keepdims=True) + EPS) * q_scale
                p_ref[r0:r0 + sub, c:c + DN_HEAD_DIM] = yh * jnp.where(part == 2, 1.0, inv)
        carry[cj] = prev


def in_proj(x, g, w, w_gates, conv_w, l, S, tm, tn, conv_col0):
    R, D = x.shape
    N = w.shape[2]
    NG = w_gates.shape[2]
    K, conv_cols = conv_w.shape[1:]
    assert N % tn == 0 and conv_col0 % tn == 0 and (conv_cols // 3) % tn == 0 and tn % DN_HEAD_DIM == 0
    n_main, conv0, n_conv = N // tn, conv_col0 // tn, conv_cols // tn
    main = lambda j: jnp.minimum(j, n_main - 1)
    vmem = (2 * tm * D * 4 + tm * D * 2 + 2 * D * tn * 2 + 2 * D * NG * 2 + 3 * tm * tn * 4 + 2 * tm * NG * 4
            + n_conv * V7X_SUBLANES * tn * 4 + 8 * IN_PROJ_SUB_ROWS * tn * 4)
    return pl.pallas_call(
        functools.partial(_in_proj_body, n_main=n_main, tiles_per_seq=S // tm, conv0=conv0,
                          tiles_per_part=n_conv // 3),
        grid=(R // tm, n_main + 1),
        in_specs=[
            pl.BlockSpec((tm, D), lambda i, j: (i, 0)),
            pl.BlockSpec((1, D), lambda i, j: (0, 0)),
            pl.BlockSpec((None, D, tn), lambda i, j: (l, 0, main(j))),
            pl.BlockSpec((None, D, NG), lambda i, j: (l, 0, 0)),
            pl.BlockSpec((None, K, tn), lambda i, j: (l, 0, jnp.clip(j - conv0, 0, n_conv - 1))),
        ],
        out_specs=[
            pl.BlockSpec((tm, tn), lambda i, j: (i, main(j))),
            pl.BlockSpec((tm, NG), lambda i, j: (i, 0)),
        ],
        out_shape=[jax.ShapeDtypeStruct((R, N), F32), jax.ShapeDtypeStruct((R, NG), F32)],
        scratch_shapes=[
            pltpu.VMEM((tm, D), BF16),
            pltpu.VMEM((n_conv, V7X_SUBLANES, tn), F32),
        ],
        compiler_params=_compiler_params(2, vmem),
        name="in_proj",
    )(x, g, w, w_gates, conv_w)


def _pool_body(u_ref, wp_ref, sc_ref, o_ref, buf):
    s = pl.program_id(1)
    ts = u_ref.shape[0]
    G = wp_ref.shape[1]

    @pl.when(s == 0)
    def _():
        buf[0:POOL_HALO, :] = jnp.zeros((POOL_HALO, buf.shape[1]), F32)

    buf[POOL_HALO:POOL_HALO + ts, :] = u_ref[...]
    pos = (s * ts + 1 + lax.broadcasted_iota(jnp.int32, (ts, 1), 0)).astype(F32)
    for i, w in enumerate(POOL_WINDOWS):
        cols = slice(i * G, (i + 1) * G)
        u = buf[POOL_HALO:POOL_HALO + ts, cols]
        acc = u
        for k in range(1, w):
            acc = acc + buf[POOL_HALO - k:POOL_HALO - k + ts, cols]
        mixed = acc / jnp.minimum(pos, float(w)) - u
        y = _dot(mixed.astype(BF16), wp_ref[i]) * sc_ref[:, cols]
        o_ref[:, cols] = y.astype(o_ref.dtype)
    buf[0:POOL_HALO, :] = buf[ts:ts + POOL_HALO, :]


def pool_mixer(p, w_pool, pool_scale, l, B, S, ts):
    n_win, G, _ = w_pool.shape[1:]
    W = n_win * G
    nt = S // ts
    vmem = 2 * ts * W * 4 + (ts + POOL_HALO) * W * 4 + 2 * n_win * G * G * 2 + 2 * ts * W * 2 + 4 * ts * G * 4
    return pl.pallas_call(
        _pool_body,
        grid=(B, nt),
        in_specs=[
            pl.BlockSpec((ts, W), lambda b, s: (b * nt + s, 0)),
            pl.BlockSpec((None, n_win, G, G), lambda b, s: (l, 0, 0, 0)),
            pl.BlockSpec((None, 1, W), lambda b, s: (l, 0, 0)),
        ],
        out_specs=pl.BlockSpec((ts, W), lambda b, s: (b * nt + s, 0)),
        out_shape=jax.ShapeDtypeStruct((B * S, W), BF16),
        scratch_shapes=[pltpu.VMEM((ts + POOL_HALO, W), F32)],
        compiler_params=_compiler_params(2, vmem),
        name="pool_mixer",
    )(p, w_pool, pool_scale)


def _unit_lower_inverses(ls, eye, ii, jj):
    C = ls[0].shape[0]
    size = 2
    same = (ii // size) == (jj // size)
    xs = [eye - jnp.where(same, l_mat, 0.0) for l_mat in ls]
    while size < C:
        inner, same = same, (ii // (2 * size)) == (jj // (2 * size))
        offs = [jnp.where(same & jnp.logical_not(inner), l_mat, 0.0).astype(BF16) for l_mat in ls]
        xbs = [x.astype(BF16) for x in xs]
        ys = [_dot(off, xb) for off, xb in zip(offs, xbs)]
        xs = [x - _dot(xb, y.astype(BF16)) for x, xb, y in zip(xs, xbs, ys)]
        size *= 2
    return xs


def _dn_body(q_ref, k_ref, v_ref, z_ref, ba_ref, alog_ref, dtb_ref, ng_ref, o_ref,
             state, beta_rep, gc_rep, grow_s, cdec_s, u_s, w_s, qd_s, at_s, kdt_s):
    s = pl.program_id(1)
    ts, W = q_ref.shape
    H = W // DN_HEAD_DIM
    C = DN_CHUNK
    n_chunk = ts // C

    @pl.when(s == 0)
    def _():
        state[...] = jnp.zeros(state.shape, F32)

    ba = ba_ref[...]
    beta = 1.0 / (1.0 + jnp.exp(-ba))
    xs = ba + dtb_ref[...]
    softplus = jnp.maximum(xs, 0.0) + jnp.log1p(jnp.exp(-jnp.abs(xs)))
    g = -jnp.exp(alog_ref[...]) * softplus
    row_in_chunk = lax.broadcasted_iota(jnp.int32, (ts, V7X_LANES), 0) % C
    gc = g
    sh = 1
    while sh < C:
        gc = gc + jnp.where(row_in_chunk >= sh, pltpu.roll(gc, sh, axis=0), 0.0)
        sh *= 2
    gct = gc.T
    lane = lax.broadcasted_iota(jnp.int32, (ts, V7X_LANES), 1)
    for h in range(H):
        beta_rep[h] = jnp.broadcast_to(
            jnp.sum(jnp.where(lane == h, beta, 0.0), axis=-1, keepdims=True), (ts, V7X_LANES))
        gc_rep[h] = jnp.broadcast_to(
            jnp.sum(jnp.where(lane == H + h, gc, 0.0), axis=-1, keepdims=True), (ts, V7X_LANES))
        grow_s[h] = gct[H + h:H + h + 1, :]

    ii = lax.broadcasted_iota(jnp.int32, (C, C), 0)
    jj = lax.broadcasted_iota(jnp.int32, (C, C), 1)
    eye = (ii == jj).astype(F32)

    def head_group_body(hg, carry):
        pairs = []
        for hh in range(DN_HEADS_PER_STEP):
            h = hg * DN_HEADS_PER_STEP + hh
            cols = pl.ds(pl.multiple_of(h * DN_HEAD_DIM, DN_HEAD_DIM), DN_HEAD_DIM)
            bcol, gcol, grow_all = beta_rep[h], gc_rep[h], grow_s[h]
            egc = jnp.exp(gcol)
            for c in range(n_chunk):
                r = slice(c * C, (c + 1) * C)
                pairs.append(dict(h=h, c=c, r=r, cols=cols, q=q_ref[r, cols], k=k_ref[r, cols], v=v_ref[r, cols],
                                  b=bcol[r], g=gcol[r], e=egc[r], grow=grow_all[:, r]))
        kbs = [p["k"].astype(BF16) for p in pairs]
        decs = [jnp.exp(jnp.minimum(p["g"] - p["grow"], 0.0)) for p in pairs]
        kks = [_dot_nt(kb, kb) for kb in kbs]
        qks = [_dot_nt(p["q"].astype(BF16), kb) for p, kb in zip(pairs, kbs)]
        ls = [jnp.where(ii > jj, kk * dec, 0.0) * p["b"] for kk, dec, p in zip(kks, decs, pairs)]
        for p, qk, dec in zip(pairs, qks, decs):
            r, cols = p["r"], p["cols"]
            at_s[r, cols] = jnp.where(ii >= jj, qk * dec, 0.0).astype(BF16)
            qd_s[r, cols] = (p["q"] * p["e"]).astype(BF16)
            kd = p["k"] * jnp.exp(p["g"][C - 1:C, :] - p["g"])
            kdt_s[r, cols] = kd.T.astype(BF16)
            cdec_s[p["c"], p["h"]] = p["e"][C - 1:C, :]
        t_invs = _unit_lower_inverses(ls, eye, ii, jj)
        rhs = [jnp.concatenate([p["v"] * p["b"], p["k"] * (p["b"] * p["e"])], axis=1) for p in pairs]
        sols = [_dot(t_inv.astype(BF16), b.astype(BF16)) for t_inv, b in zip(t_invs, rhs)]
        for p, sol in zip(pairs, sols):
            u_s[p["r"], p["cols"]] = sol[:, :DN_HEAD_DIM]
            w_s[p["r"], p["cols"]] = sol[:, DN_HEAD_DIM:].astype(BF16)
        return carry

    lax.fori_loop(0, H // DN_HEADS_PER_STEP, head_group_body, 0)

    def chunk_body(c, carry):
        r0 = pl.multiple_of(c * C, C)
        rows = pl.ds(r0, C)
        hs = range(H)
        cols = [slice(h * DN_HEAD_DIM, (h + 1) * DN_HEAD_DIM) for h in hs]
        sts = [state[h] for h in hs]
        sbs = [st.astype(BF16) for st in sts]
        wss = [_dot(w_s[rows, cols[h]], sbs[h]) for h in hs]
        qss = [_dot(qd_s[rows, cols[h]], sbs[h]) for h in hs]
        vbs = [(u_s[rows, cols[h]] - wss[h]).astype(BF16) for h in hs]
        avs = [_dot(at_s[rows, cols[h]], vbs[h]) for h in hs]
        kvs = [_dot(kdt_s[rows, cols[h]], vbs[h]) for h in hs]
        for h in hs:
            state[h] = sts[h] * cdec_s[c, h] + kvs[h]
            gated = _rms_rows(qss[h] + avs[h], ng_ref[...]) * _silu(z_ref[rows, cols[h]])
            o_ref[rows, cols[h]] = gated.astype(o_ref.dtype)
        return carry

    lax.fori_loop(0, n_chunk, chunk_body, 0, unroll=True)


def gated_delta_net(p, gates, a_log, dt_bias, norm_g, l, B, S, ts, col_block0, H):
    W = H * DN_HEAD_DIM
    nt = S // ts
    NB = gates.shape[1]
    C = DN_CHUNK
    vmem = (2 * 4 * ts * W * 4 + 2 * ts * NB * 4 + 2 * ts * W * 2
            + H * DN_HEAD_DIM * DN_HEAD_DIM * 4 + 2 * H * ts * V7X_LANES * 4 + H * V7X_SUBLANES * ts * 4
            + ts * W * 4 + 4 * ts * W * 2 + 48 * C * C * 4 * DN_HEADS_PER_STEP)

    def pblock(k):
        return pl.BlockSpec((ts, W), lambda b, s: (b * nt + s, col_block0 + k))

    return pl.pallas_call(
        _dn_body,
        grid=(B, nt),
        in_specs=[
            pblock(0), pblock(1), pblock(2), pblock(3),
            pl.BlockSpec((ts, NB), lambda b, s: (b * nt + s, 0)),
            pl.BlockSpec((None, 1, NB), lambda b, s: (l, 0, 0)),
            pl.BlockSpec((None, 1, NB), lambda b, s: (l, 0, 0)),
            pl.BlockSpec((None, 1, DN_HEAD_DIM), lambda b, s: (l, 0, 0)),
        ],
        out_specs=pl.BlockSpec((ts, W), lambda b, s: (b * nt + s, 0)),
        out_shape=jax.ShapeDtypeStruct((B * S, W), BF16),
        scratch_shapes=[
            pltpu.VMEM((H, DN_HEAD_DIM, DN_HEAD_DIM), F32),
            pltpu.VMEM((H, ts, V7X_LANES), F32),
            pltpu.VMEM((H, ts, V7X_LANES), F32),
            pltpu.VMEM((H, 1, ts), F32),
            pltpu.VMEM((ts // C, H, 1, V7X_LANES), F32),
            pltpu.VMEM((ts, W), F32),
            pltpu.VMEM((ts, W), BF16),
            pltpu.VMEM((ts, W), BF16),
            pltpu.VMEM((ts, W), BF16),
            pltpu.VMEM((ts, W), BF16),
        ],
        compiler_params=_compiler_params(2, vmem),
        name="gated_delta_net",
    )(p, p, p, p, gates, a_log, dt_bias, norm_g)


def _mix_out_body(x_ref, yp_ref, yd_ref, w1_ref, w2_ref, o_ref):
    o_ref[...] = x_ref[...] + _dot(yp_ref[...], w1_ref[...]) + _dot(yd_ref[...], w2_ref[...])


def mix_out(x, y_pool, y_dn, w, l, tm, tn):
    R, D = x.shape
    K1, K2 = y_pool.shape[1], y_dn.shape[1]
    assert K1 == K2
    vmem = 2 * (2 * tm * tn * 4 + tm * (K1 + K2) * 2 + (K1 + K2) * tn * 2) + 2 * tm * tn * 4
    return pl.pallas_call(
        _mix_out_body,
        grid=(R // tm, D // tn),
        in_specs=[
            pl.BlockSpec((tm, tn), lambda i, j: (i, j)),
            pl.BlockSpec((tm, K1), lambda i, j: (i, 0)),
            pl.BlockSpec((tm, K2), lambda i, j: (i, 0)),
            pl.BlockSpec((None, K1, tn), lambda i, j: (l, 0, j)),
            pl.BlockSpec((None, K2, tn), lambda i, j: (l, 1, j)),
        ],
        out_specs=pl.BlockSpec((tm, tn), lambda i, j: (i, j)),
        out_shape=jax.ShapeDtypeStruct((R, D), F32),
        compiler_params=_compiler_params(2, vmem),
        name="mix_out",
    )(x, y_pool, y_dn, w, w)


def _xattn_body(x_ref, g_ref, wq_ref, k_ref, v_ref, wo_ref, o_ref, h_ref, a_ref):
    j = pl.program_id(1)
    dh = wq_ref.shape[1] // XA_HEADS_PER_STEP
    tn = wo_ref.shape[1]
    n_att = XA_HEADS // XA_HEADS_PER_STEP

    @pl.when(j == 0)
    def _():
        _norm_rows_into(h_ref, x_ref, g_ref)

    @pl.when(j < n_att)
    def _():
        tm = h_ref.shape[0]
        sub = min(XA_SUB_ROWS, tm)
        col0 = pl.multiple_of(j * (XA_HEADS_PER_STEP * dh), XA_HEADS_PER_STEP * dh)
        units = [(hh, r0) for hh in range(XA_HEADS_PER_STEP) for r0 in range(0, tm, sub)]

        def scores(unit):
            hh, r0 = unit
            q = _dot(h_ref[r0:r0 + sub, :], wq_ref[:, hh * dh:(hh + 1) * dh])
            return _dot_nt(q.astype(BF16), k_ref[:, hh * dh:(hh + 1) * dh]) * (dh ** -0.5)

        sc = scores(units[0])
        for n, (hh, r0) in enumerate(units):
            sc_next = scores(units[n + 1]) if n + 1 < len(units) else None
            e = jnp.exp(sc - jnp.max(sc, axis=-1, keepdims=True))
            pr = e / jnp.sum(e, axis=-1, keepdims=True)
            o = _dot(pr.astype(BF16), v_ref[:, hh * dh:(hh + 1) * dh])
            a_ref[r0:r0 + sub, pl.ds(col0 + hh * dh, dh)] = o.astype(BF16)
            sc = sc_next

    @pl.when(j >= n_att)
    def _():
        cols = pl.ds(pl.multiple_of((j - n_att) * tn, tn), tn)
        o_ref[...] = x_ref[:, cols] + _dot(a_ref[...], wo_ref[...])


def cross_attention(x, g, w_xq, kv, w_xo, l, B, S, M, tm, tn):
    R, D = x.shape
    n_att = XA_HEADS // XA_HEADS_PER_STEP
    wh = D // n_att
    tiles_per_batch = S // tm
    att = lambda j: jnp.minimum(j, n_att - 1)
    otile = lambda j: jnp.maximum(j - n_att, 0)
    vmem = (2 * tm * D * 4 + 2 * tm * D * 2 + 4 * D * wh * 2 + 4 * M * wh * 2 + 4 * D * tn * 2 + 3 * tm * tn * 4
            + 3 * tm * wh * 4 + 3 * tm * M * 4)
    return pl.pallas_call(
        _xattn_body,
        grid=(R // tm, n_att + D // tn),
        in_specs=[
            pl.BlockSpec((tm, D), lambda i, j: (i, 0)),
            pl.BlockSpec((None, 1, D), lambda i, j: (l, 0, 0)),
            pl.BlockSpec((None, D, wh), lambda i, j: (l, 0, att(j))),
            pl.BlockSpec((None, M, wh), lambda i, j: (l, i // tiles_per_batch, att(j))),
            pl.BlockSpec((None, M, wh), lambda i, j: (l, i // tiles_per_batch, n_att + att(j))),
            pl.BlockSpec((None, D, tn), lambda i, j: (l, 0, otile(j))),
        ],
        out_specs=pl.BlockSpec((tm, tn), lambda i, j: (i, otile(j))),
        out_shape=jax.ShapeDtypeStruct((R, D), F32),
        scratch_shapes=[pltpu.VMEM((tm, D), BF16), pltpu.VMEM((tm, D), BF16)],
        compiler_params=_compiler_params(2, vmem),
        name="cross_attention",
    )(x, g, w_xq, kv, kv, w_xo)


def _ffn_body(x_ref, g_ref, wg_ref, wu_ref, cw_ref, cb_ref, wd_ref, og_ref, o_ref, h_ref, gbuf, carry, *,
              tiles_per_seq, norm_output):
    i = pl.program_id(0)
    j = pl.program_id(1)
    tm = x_ref.shape[0]
    K = cw_ref.shape[0]
    halo = V7X_SUBLANES

    @pl.when(j == 0)
    def _():
        _norm_rows_into(h_ref, x_ref, g_ref)
        o_ref[...] = x_ref[...]

    gbuf[0:halo, :] = jnp.where((i % tiles_per_seq) == 0, 0.0, carry[j])
    sub = min(FFN_SUB_ROWS, tm)

    def gate_up(r0):
        h = h_ref[r0:r0 + sub, :]
        return _dot(h, wg_ref[...]), _dot(h, wu_ref[...])

    gate, up = gate_up(0)
    for r0 in range(0, tm, sub):
        nxt = gate_up(r0 + sub) if r0 + sub < tm else None
        gbuf[halo + r0:halo + r0 + sub, :] = gate
        conv = gate * cw_ref[K - 1:K, :] + cb_ref[...]
        for t in range(K - 2, -1, -1):
            conv = conv + gbuf[halo - K + 1 + t + r0:halo - K + 1 + t + r0 + sub, :] * cw_ref[t:t + 1, :]
        act = _silu(conv) * up
        o_ref[r0:r0 + sub, :] += _dot(act.astype(BF16), wd_ref[...])
        if nxt is not None:
            gate, up = nxt
    carry[j] = gbuf[tm:tm + halo, :]

    if norm_output:
        @pl.when(j == pl.num_programs(1) - 1)
        def _():
            _norm_rows_into(o_ref, o_ref, og_ref)


def conv_glu_ffn(x, g, w_gate, w_up, conv_w, conv_b, w_down, out_g, l, S, tm, tf, norm_output):
    R, D = x.shape
    F = w_gate.shape[2]
    K = conv_w.shape[1]
    nf = F // tf
    vmem = (4 * tm * D * 4 + tm * D * 2 + 2 * 3 * D * tf * 2 + (tm + V7X_SUBLANES) * tf * 4
            + nf * V7X_SUBLANES * tf * 4 + 4 * tm * tf * 4)
    return pl.pallas_call(
        functools.partial(_ffn_body, tiles_per_seq=S // tm, norm_output=norm_output),
        grid=(R // tm, nf),
        in_specs=[
            pl.BlockSpec((tm, D), lambda i, j: (i, 0)),
            pl.BlockSpec((None, 1, D), lambda i, j: (l, 0, 0)),
            pl.BlockSpec((None, D, tf), lambda i, j: (l, 0, j)),
            pl.BlockSpec((None, D, tf), lambda i, j: (l, 0, j)),
            pl.BlockSpec((None, K, tf), lambda i, j: (l, 0, j)),
            pl.BlockSpec((None, 1, tf), lambda i, j: (l, 0, j)),
            pl.BlockSpec((None, tf, D), lambda i, j: (l, j, 0)),
            pl.BlockSpec((1, D), lambda i, j: (0, 0)),
        ],
        out_specs=pl.BlockSpec((tm, D), lambda i, j: (i, 0)),
        out_shape=jax.ShapeDtypeStruct((R, D), F32),
        scratch_shapes=[
            pltpu.VMEM((tm, D), BF16),
            pltpu.VMEM((tm + V7X_SUBLANES, tf), F32),
            pltpu.VMEM((nf, V7X_SUBLANES, tf), F32),
        ],
        compiler_params=_compiler_params(2, vmem),
        name="conv_glu_ffn",
    )(x, g, w_gate, w_up, conv_w, conv_b, w_down, out_g)


def _tiles(S):
    return dict(tm=min(1024, S), ts_pool=min(512, S), ts_dn=min(512, S))


def kernel(x, mem, mix_norm_g, w_in, w_pool, pool_scale, dn_conv_w, dn_a_log, dn_dt_bias, dn_norm_g, w_mix_out,
           xa_norm_g, mem_norm_g, w_xq, w_xkv, w_xo, ffn_norm_g, w_gate, w_up, ffn_conv_w, ffn_conv_b, w_down,
           final_norm_g):
    B, S, D = x.shape
    M = mem.shape[1]
    depth = w_in.shape[0]
    H = dn_a_log.shape[1]
    dn_w = H * DN_HEAD_DIM
    pool_w = w_pool.shape[1] * w_pool.shape[2]
    main = pool_w + 4 * dn_w
    assert pool_w == dn_w and w_in.shape[2] == main + 2 * H and 2 * H <= V7X_LANES
    t = _tiles(S)

    tn = 512
    w_in_b = w_in[:, :, :main].astype(BF16)
    w_gates_b = jnp.pad(w_in[:, :, main:].astype(BF16), ((0, 0), (0, 0), (0, V7X_LANES - 2 * H)))
    pad_gate = lambda a: jnp.pad(a, ((0, 0), (H, V7X_LANES - 2 * H)))[:, None, :]
    a_log_p, dt_bias_p = pad_gate(dn_a_log), pad_gate(dn_dt_bias)
    w_pool_b = w_pool.astype(BF16)
    w_mix_b = w_mix_out.astype(BF16)
    w_xq_b, w_xkv_b, w_xo_b = w_xq.astype(BF16), w_xkv.astype(BF16), w_xo.astype(BF16)
    w_gate_b, w_up_b, w_down_b = w_gate.astype(BF16), w_up.astype(BF16), w_down.astype(BF16)
    row = lambda a: a[:, None, :]

    xr = x.reshape(B * S, D)
    kv = norm_matmul(mem.reshape(B * M, D), mem_norm_g[None, :], w_xkv_b, BF16, tm=min(1024, B * M), tn=tn)
    for l in range(depth):
        p, gates = in_proj(xr, mix_norm_g[l][None, :], w_in_b, w_gates_b, dn_conv_w, l, S, tm=t["tm"], tn=tn,
                           conv_col0=pool_w)
        y_pool = pool_mixer(p, w_pool_b, row(pool_scale), l, B, S, t["ts_pool"])
        y_dn = gated_delta_net(p, gates, a_log_p, dt_bias_p, row(dn_norm_g), l, B, S, t["ts_dn"],
                               col_block0=pool_w // dn_w, H=H)
        xr = mix_out(xr, y_pool, y_dn, w_mix_b, l, tm=t["tm"], tn=1024)
        xr = cross_attention(xr, row(xa_norm_g), w_xq_b, kv, w_xo_b, l, B, S, M, tm=t["tm"], tn=2 * tn)
        xr = conv_glu_ffn(xr, row(ffn_norm_g), w_gate_b, w_up_b, ffn_conv_w, row(ffn_conv_b), w_down_b,
                          final_norm_g[None, :], l, S, tm=t["tm"], tf=tn, norm_output=(l == depth - 1))
    return xr.reshape(B, S, D)
```

```python
import functools

import jax
import jax.numpy as jnp
from jax import lax
from jax.experimental import pallas as pl
from jax.experimental.pallas import tpu as pltpu

F32 = jnp.float32
BF16 = jnp.bfloat16

EPS = 1e-6
POOL_WINDOWS = (2, 4, 8, 16)
POOL_HALO = 16
DN_HEAD_DIM = 128
DN_CHUNK = 128
DN_HEADS_PER_STEP = 8
XA_HEADS = 4
XA_HEADS_PER_STEP = 2

V7X_VMEM_BYTES = 64 * 1024 * 1024
V7X_LANES = 128
V7X_SUBLANES = 8
NORM_ROWS = 128
XA_SUB_ROWS = 512
FFN_SUB_ROWS = 512
FFN_FIRST_SUB_ROWS = 256
NORM_SUB_ROWS = 256
IN_PROJ_SUB_ROWS = 128


def _compiler_params(n_axes, vmem_bytes):
    limit = min(int(vmem_bytes * 1.25) + (4 << 20), V7X_VMEM_BYTES * 7 // 8)
    return pltpu.CompilerParams(dimension_semantics=("arbitrary",) * n_axes, vmem_limit_bytes=limit)


def _dot(a, b):
    return jnp.dot(a, b, preferred_element_type=F32)


def _dot_nt(a, b):
    return lax.dot_general(a, b, (((1,), (1,)), ((), ())), preferred_element_type=F32)


def _silu(x):
    return x * (1.0 / (1.0 + jnp.exp(-x)))


def _rms_rows(x, g):
    ms = jnp.mean(x * x, axis=-1, keepdims=True)
    return x * lax.rsqrt(ms + EPS) * g


def _rows_shifted_down(a, prev, s):
    n = prev.shape[0]
    rows = lax.broadcasted_iota(jnp.int32, prev.shape, 0)
    rolled = [pltpu.roll(piece, s, axis=0) for piece in [prev] + [a[r:r + n, :] for r in range(0, a.shape[0], n)]]
    return jnp.concatenate([jnp.where(rows < s, lo, hi) for lo, hi in zip(rolled[:-1], rolled[1:])], axis=0)


def _norm_rows_into(h_ref, x_ref, g_ref):
    g = g_ref[...]

    def body(r, c):
        rows = pl.ds(pl.multiple_of(r * NORM_ROWS, NORM_ROWS), NORM_ROWS)
        h_ref[rows, :] = _rms_rows(x_ref[rows, :], g).astype(h_ref.dtype)
        return c

    lax.fori_loop(0, x_ref.shape[0] // NORM_ROWS, body, 0)


def _norm_matmul_body(x_ref, g_ref, w_ref, o_ref, h_ref):
    @pl.when((pl.program_id(1) == 0) & (pl.program_id(2) == 0))
    def _():
        _norm_rows_into(h_ref, x_ref, g_ref)

    o_ref[...] = _dot(h_ref[...], w_ref[...]).astype(o_ref.dtype)


def norm_matmul(x, g, w, out_dtype, tm, tn):
    R, K = x.shape
    L, _, N = w.shape
    osz = jnp.dtype(out_dtype).itemsize
    vmem = 2 * tm * K * 4 + tm * K * 2 + 2 * K * tn * 2 + 2 * tm * tn * osz + tm * tn * 4
    return pl.pallas_call(
        _norm_matmul_body,
        grid=(R // tm, L, N // tn),
        in_specs=[
            pl.BlockSpec((tm, K), lambda i, l, j: (i, 0)),
            pl.BlockSpec((1, K), lambda i, l, j: (0, 0)),
            pl.BlockSpec((None, K, tn), lambda i, l, j: (l, 0, j)),
        ],
        out_specs=pl.BlockSpec((None, tm, tn), lambda i, l, j: (l, i, j)),
        out_shape=jax.ShapeDtypeStruct((L, R, N), out_dtype),
        scratch_shapes=[pltpu.VMEM((tm, K), BF16)],
        compiler_params=_compiler_params(3, vmem),
        name="norm_matmul",
    )(x, g, w)


def _in_proj_body(x_ref, g_ref, w_ref, wg_ref, cw_ref, p_ref, gate_ref, h_ref, carry, *,
                  n_main, tiles_per_seq, conv0, tiles_per_part):
    i = pl.program_id(0)
    j = pl.program_id(1)
    tm, tn = p_ref.shape
    K = cw_ref.shape[0]
    halo = V7X_SUBLANES

    is_conv = (j >= conv0) & (j < conv0 + 3 * tiles_per_part)

    @pl.when(j == 0)
    def _():
        sub = min(NORM_SUB_ROWS, tm)
        for r0 in range(0, tm, sub):
            h = _rms_rows(x_ref[r0:r0 + sub, :], g_ref[...]).astype(BF16)
            h_ref[r0:r0 + sub, :] = h
            p_ref[r0:r0 + sub, :] = _dot(h, w_ref[...])

    @pl.when(j == n_main)
    def _():
        gate_ref[...] = _dot(h_ref[...], wg_ref[...])

    @pl.when(jnp.logical_not(is_conv) & (j > 0) & (j < n_main))
    def _():
        p_ref[...] = _dot(h_ref[...], w_ref[...])

    @pl.when(is_conv)
    def _():
        cj = j - conv0
        part = cj // tiles_per_part
        q_scale = jnp.where(part == 0, DN_HEAD_DIM ** -0.5, 1.0)
        prev = jnp.where((i % tiles_per_seq) == 0, 0.0, carry[cj])
        sub = min(IN_PROJ_SUB_ROWS, tm)
        for r0 in range(0, tm, sub):
            acc = _dot(h_ref[r0:r0 + sub, :], w_ref[...])
            y = acc * cw_ref[K - 1:K, :]
            for s in range(1, K):
                y = y + _rows_shifted_down(acc, prev, s) * cw_ref[K - 1 - s:K - s, :]
            prev = acc[sub - halo:sub, :]
            y = _silu(y)
            for c in range(0, tn, DN_HEAD_DIM):
                yh = y[:, c:c + DN_HEAD_DIM]
                inv = lax.rsqrt(jnp.sum(yh * yh, axis=-1, keepdims=True) + EPS) * q_scale
                p_ref[r0:r0 + sub, c:c + DN_HEAD_DIM] = yh * jnp.where(part == 2, 1.0, inv)
        carry[cj] = prev


def in_proj(x, g, w, w_gates, conv_w, l, S, tm, tn, conv_col0):
    R, D = x.shape
    N = w.shape[2]
    NG = w_gates.shape[2]
    K, conv_cols = conv_w.shape[1:]
    assert N % tn == 0 and conv_col0 % tn == 0 and (conv_cols // 3) % tn == 0 and tn % DN_HEAD_DIM == 0
    n_main, conv0, n_conv = N // tn, conv_col0 // tn, conv_cols // tn
    assert conv0 >= 1
    main = lambda j: jnp.minimum(j, n_main - 1)
    vmem = (2 * tm * D * 4 + tm * D * 2 + 2 * D * tn * 2 + 2 * D * NG * 2 + 3 * tm * tn * 4 + 2 * tm * NG * 4
            + n_conv * V7X_SUBLANES * tn * 4 + 8 * IN_PROJ_SUB_ROWS * tn * 4)
    return pl.pallas_call(
        functools.partial(_in_proj_body, n_main=n_main, tiles_per_seq=S // tm, conv0=conv0,
                          tiles_per_part=n_conv // 3),
        grid=(R // tm, n_main + 1),
        in_specs=[
            pl.BlockSpec((tm, D), lambda i, j: (i, 0)),
            pl.BlockSpec((1, D), lambda i, j: (0, 0)),
            pl.BlockSpec((None, D, tn), lambda i, j: (l, 0, main(j))),
            pl.BlockSpec((None, D, NG), lambda i, j: (l, 0, 0)),
            pl.BlockSpec((None, K, tn), lambda i, j: (l, 0, jnp.clip(j - conv0, 0, n_conv - 1))),
        ],
        out_specs=[
            pl.BlockSpec((tm, tn), lambda i, j: (i, main(j))),
            pl.BlockSpec((tm, NG), lambda i, j: (i, 0)),
        ],
        out_shape=[jax.ShapeDtypeStruct((R, N), F32), jax.ShapeDtypeStruct((R, NG), F32)],
        scratch_shapes=[
            pltpu.VMEM((tm, D), BF16),
            pltpu.VMEM((n_conv, V7X_SUBLANES, tn), F32),
        ],
        compiler_params=_compiler_params(2, vmem),
        name="in_proj",
    )(x, g, w, w_gates, conv_w)


def _pool_body(u_ref, wp_ref, sc_ref, o_ref, buf):
    s = pl.program_id(1)
    ts = u_ref.shape[0]
    G = wp_ref.shape[1]

    @pl.when(s == 0)
    def _():
        buf[0:POOL_HALO, :] = jnp.zeros((POOL_HALO, buf.shape[1]), F32)

    buf[POOL_HALO:POOL_HALO + ts, :] = u_ref[...]
    pos = (s * ts + 1 + lax.broadcasted_iota(jnp.int32, (ts, 1), 0)).astype(F32)
    for i, w in enumerate(POOL_WINDOWS):
        cols = slice(i * G, (i + 1) * G)
        u = buf[POOL_HALO:POOL_HALO + ts, cols]
        acc = u
        for k in range(1, w):
            acc = acc + buf[POOL_HALO - k:POOL_HALO - k + ts, cols]
        mixed = acc / jnp.minimum(pos, float(w)) - u
        y = _dot(mixed.astype(BF16), wp_ref[i]) * sc_ref[:, cols]
        o_ref[:, cols] = y.astype(o_ref.dtype)
    buf[0:POOL_HALO, :] = buf[ts:ts + POOL_HALO, :]


def pool_mixer(p, w_pool, pool_scale, l, B, S, ts):
    n_win, G, _ = w_pool.shape[1:]
    W = n_win * G
    nt = S // ts
    vmem = 2 * ts * W * 4 + (ts + POOL_HALO) * W * 4 + 2 * n_win * G * G * 2 + 2 * ts * W * 2 + 4 * ts * G * 4
    return pl.pallas_call(
        _pool_body,
        grid=(B, nt),
        in_specs=[
            pl.BlockSpec((ts, W), lambda b, s: (b * nt + s, 0)),
            pl.BlockSpec((None, n_win, G, G), lambda b, s: (l, 0, 0, 0)),
            pl.BlockSpec((None, 1, W), lambda b, s: (l, 0, 0)),
        ],
        out_specs=pl.BlockSpec((ts, W), lambda b, s: (b * nt + s, 0)),
        out_shape=jax.ShapeDtypeStruct((B * S, W), BF16),
        scratch_shapes=[pltpu.VMEM((ts + POOL_HALO, W), F32)],
        compiler_params=_compiler_params(2, vmem),
        name="pool_mixer",
    )(p, w_pool, pool_scale)


def _unit_lower_inverses(ls, eye, ii, jj):
    C = ls[0].shape[0]
    size = 2
    same = (ii // size) == (jj // size)
    xs = [eye - jnp.where(same, l_mat, 0.0) for l_mat in ls]
    while size < C:
        inner, same = same, (ii // (2 * size)) == (jj // (2 * size))
        offs = [jnp.where(same & jnp.logical_not(inner), l_mat, 0.0).astype(BF16) for l_mat in ls]
        xbs = [x.astype(BF16) for x in xs]
        ys = [_dot(off, xb) for off, xb in zip(offs, xbs)]
        xs = [x - _dot(xb, y.astype(BF16)) for x, xb, y in zip(xs, xbs, ys)]
        size *= 2
    return xs


def _dn_body(q_ref, k_ref, v_ref, z_ref, ba_ref, alog_ref, dtb_ref, ng_ref, o_ref,
             state, beta_rep, gc_rep, grow_s, cdec_s, u_s, w_s, qd_s, at_s, kdt_s):
    s = pl.program_id(1)
    ts, W = q_ref.shape
    H = W // DN_HEAD_DIM
    C = DN_CHUNK
    n_chunk = ts // C

    @pl.when(s == 0)
    def _():
        state[...] = jnp.zeros(state.shape, F32)

    ba = ba_ref[...]
    beta = 1.0 / (1.0 + jnp.exp(-ba))
    xs = ba + dtb_ref[...]
    softplus = jnp.maximum(xs, 0.0) + jnp.log1p(jnp.exp(-jnp.abs(xs)))
    g = -jnp.exp(alog_ref[...]) * softplus
    row_in_chunk = lax.broadcasted_iota(jnp.int32, (ts, V7X_LANES), 0) % C
    gc = g
    sh = 1
    while sh < C:
        gc = gc + jnp.where(row_in_chunk >= sh, pltpu.roll(gc, sh, axis=0), 0.0)
        sh *= 2
    gct = gc.T
    lane = lax.broadcasted_iota(jnp.int32, (ts, V7X_LANES), 1)
    for h in range(H):
        beta_rep[h] = jnp.broadcast_to(
            jnp.sum(jnp.where(lane == h, beta, 0.0), axis=-1, keepdims=True), (ts, V7X_LANES))
        gc_rep[h] = jnp.broadcast_to(
            jnp.sum(jnp.where(lane == H + h, gc, 0.0), axis=-1, keepdims=True), (ts, V7X_LANES))
        grow_s[h] = gct[H + h:H + h + 1, :]

    ii = lax.broadcasted_iota(jnp.int32, (C, C), 0)
    jj = lax.broadcasted_iota(jnp.int32, (C, C), 1)
    eye = (ii == jj).astype(F32)

    def head_group_body(hg, carry):
        pairs = []
        for hh in range(DN_HEADS_PER_STEP):
            h = hg * DN_HEADS_PER_STEP + hh
            cols = pl.ds(pl.multiple_of(h * DN_HEAD_DIM, DN_HEAD_DIM), DN_HEAD_DIM)
            bcol, gcol, grow_all = beta_rep[h], gc_rep[h], grow_s[h]
            egc = jnp.exp(gcol)
            for c in range(n_chunk):
                r = slice(c * C, (c + 1) * C)
                pairs.append(dict(h=h, c=c, r=r, cols=cols, q=q_ref[r, cols], k=k_ref[r, cols], v=v_ref[r, cols],
                                  b=bcol[r], g=gcol[r], e=egc[r], grow=grow_all[:, r]))
        kbs = [p["k"].astype(BF16) for p in pairs]
        decs = [jnp.exp(jnp.minimum(p["g"] - p["grow"], 0.0)) for p in pairs]
        kks = [_dot_nt(kb, kb) for kb in kbs]
        qks = [_dot_nt(p["q"].astype(BF16), kb) for p, kb in zip(pairs, kbs)]
        ls = [jnp.where(ii > jj, kk * dec, 0.0) * p["b"] for kk, dec, p in zip(kks, decs, pairs)]
        for p, qk, dec in zip(pairs, qks, decs):
            r, cols = p["r"], p["cols"]
            at_s[r, cols] = jnp.where(ii >= jj, qk * dec, 0.0).astype(BF16)
            qd_s[r, cols] = (p["q"] * p["e"]).astype(BF16)
            kd = p["k"] * jnp.exp(p["g"][C - 1:C, :] - p["g"])
            kdt_s[r, cols] = kd.T.astype(BF16)
            cdec_s[p["c"], p["h"]] = p["e"][C - 1:C, :]
        t_invs = _unit_lower_inverses(ls, eye, ii, jj)
        rhs = [jnp.concatenate([p["v"] * p["b"], p["k"] * (p["b"] * p["e"])], axis=1) for p in pairs]
        sols = [_dot(t_inv.astype(BF16), b.astype(BF16)) for t_inv, b in zip(t_invs, rhs)]
        for p, sol in zip(pairs, sols):
            u_s[p["r"], p["cols"]] = sol[:, :DN_HEAD_DIM]
            w_s[p["r"], p["cols"]] = sol[:, DN_HEAD_DIM:].astype(BF16)
        return carry

    lax.fori_loop(0, H // DN_HEADS_PER_STEP, head_group_body, 0)

    def chunk_body(c, carry):
        r0 = pl.multiple_of(c * C, C)
        rows = pl.ds(r0, C)
        hs = range(H)
        cols = [slice(h * DN_HEAD_DIM, (h + 1) * DN_HEAD_DIM) for h in hs]
        sts = [state[h] for h in hs]
        sbs = [st.astype(BF16) for st in sts]
        wss = [_dot(w_s[rows, cols[h]], sbs[h]) for h in hs]
        qss = [_dot(qd_s[rows, cols[h]], sbs[h]) for h in hs]
        vbs = [(u_s[rows, cols[h]] - wss[h]).astype(BF16) for h in hs]
        avs = [_dot(at_s[rows, cols[h]], vbs[h]) for h in hs]
        kvs = [_dot(kdt_s[rows, cols[h]], vbs[h]) for h in hs]
        for h in hs:
            state[h] = sts[h] * cdec_s[c, h] + kvs[h]
            gated = _rms_rows(qss[h] + avs[h], ng_ref[...]) * _silu(z_ref[rows, cols[h]])
            o_ref[rows, cols[h]] = gated.astype(o_ref.dtype)
        return carry

    lax.fori_loop(0, n_chunk, chunk_body, 0, unroll=True)


def gated_delta_net(p, gates, a_log, dt_bias, norm_g, l, B, S, ts, col_block0, H):
    W = H * DN_HEAD_DIM
    nt = S // ts
    NB = gates.shape[1]
    C = DN_CHUNK
    vmem = (2 * 4 * ts * W * 4 + 2 * ts * NB * 4 + 2 * ts * W * 2
            + H * DN_HEAD_DIM * DN_HEAD_DIM * 4 + 2 * H * ts * V7X_LANES * 4 + H * V7X_SUBLANES * ts * 4
            + ts * W * 4 + 4 * ts * W * 2 + 48 * C * C * 4 * DN_HEADS_PER_STEP)

    def pblock(k):
        return pl.BlockSpec((ts, W), lambda b, s: (b * nt + s, col_block0 + k))

    return pl.pallas_call(
        _dn_body,
        grid=(B, nt),
        in_specs=[
            pblock(0), pblock(1), pblock(2), pblock(3),
            pl.BlockSpec((ts, NB), lambda b, s: (b * nt + s, 0)),
            pl.BlockSpec((None, 1, NB), lambda b, s: (l, 0, 0)),
            pl.BlockSpec((None, 1, NB), lambda b, s: (l, 0, 0)),
            pl.BlockSpec((None, 1, DN_HEAD_DIM), lambda b, s: (l, 0, 0)),
        ],
        out_specs=pl.BlockSpec((ts, W), lambda b, s: (b * nt + s, 0)),
        out_shape=jax.ShapeDtypeStruct((B * S, W), BF16),
        scratch_shapes=[
            pltpu.VMEM((H, DN_HEAD_DIM, DN_HEAD_DIM), F32),
            pltpu.VMEM((H, ts, V7X_LANES), F32),
            pltpu.VMEM((H, ts, V7X_LANES), F32),
            pltpu.VMEM((H, 1, ts), F32),
            pltpu.VMEM((ts // C, H, 1, V7X_LANES), F32),
            pltpu.VMEM((ts, W), F32),
            pltpu.VMEM((ts, W), BF16),
            pltpu.VMEM((ts, W), BF16),
            pltpu.VMEM((ts, W), BF16),
            pltpu.VMEM((ts, W), BF16),
        ],
        compiler_params=_compiler_params(2, vmem),
        name="gated_delta_net",
    )(p, p, p, p, gates, a_log, dt_bias, norm_g)


def _mix_out_body(x_ref, yp_ref, yd_ref, w1_ref, w2_ref, o_ref):
    o_ref[...] = x_ref[...] + _dot(yp_ref[...], w1_ref[...]) + _dot(yd_ref[...], w2_ref[...])


def mix_out(x, y_pool, y_dn, w, l, tm, tn):
    R, D = x.shape
    K1, K2 = y_pool.shape[1], y_dn.shape[1]
    assert K1 == K2
    vmem = 2 * (2 * tm * tn * 4 + tm * (K1 + K2) * 2 + (K1 + K2) * tn * 2) + 2 * tm * tn * 4
    return pl.pallas_call(
        _mix_out_body,
        grid=(R // tm, D // tn),
        in_specs=[
            pl.BlockSpec((tm, tn), lambda i, j: (i, j)),
            pl.BlockSpec((tm, K1), lambda i, j: (i, 0)),
            pl.BlockSpec((tm, K2), lambda i, j: (i, 0)),
            pl.BlockSpec((None, K1, tn), lambda i, j: (l, 0, j)),
            pl.BlockSpec((None, K2, tn), lambda i, j: (l, 1, j)),
        ],
        out_specs=pl.BlockSpec((tm, tn), lambda i, j: (i, j)),
        out_shape=jax.ShapeDtypeStruct((R, D), F32),
        compiler_params=_compiler_params(2, vmem),
        name="mix_out",
    )(x, y_pool, y_dn, w, w)


def _xattn_body(x_ref, g_ref, wq_ref, k_ref, v_ref, wo_ref, o_ref, h_ref, a_ref):
    j = pl.program_id(1)
    dh = wq_ref.shape[1] // XA_HEADS_PER_STEP
    tn = wo_ref.shape[1]
    n_att = XA_HEADS // XA_HEADS_PER_STEP

    def attention_step(first):
        tm = h_ref.shape[0]
        sub = min(XA_SUB_ROWS, tm)
        col0 = pl.multiple_of(j * (XA_HEADS_PER_STEP * dh), XA_HEADS_PER_STEP * dh)
        units = [(hh, r0) for hh in range(XA_HEADS_PER_STEP) for r0 in range(0, tm, sub)]

        def scores(unit):
            hh, r0 = unit
            if first and hh == 0:
                h_ref[r0:r0 + sub, :] = _rms_rows(x_ref[r0:r0 + sub, :], g_ref[...]).astype(BF16)
            q = _dot(h_ref[r0:r0 + sub, :], wq_ref[:, hh * dh:(hh + 1) * dh])
            return _dot_nt(q.astype(BF16), k_ref[:, hh * dh:(hh + 1) * dh]) * (dh ** -0.5)

        sc = scores(units[0])
        for n, (hh, r0) in enumerate(units):
            sc_next = scores(units[n + 1]) if n + 1 < len(units) else None
            e = jnp.exp(sc - jnp.max(sc, axis=-1, keepdims=True))
            pr = e / jnp.sum(e, axis=-1, keepdims=True)
            o = _dot(pr.astype(BF16), v_ref[:, hh * dh:(hh + 1) * dh])
            a_ref[r0:r0 + sub, pl.ds(col0 + hh * dh, dh)] = o.astype(BF16)
            sc = sc_next

    pl.when(j == 0)(functools.partial(attention_step, True))
    pl.when((j > 0) & (j < n_att))(functools.partial(attention_step, False))

    @pl.when(j >= n_att)
    def _():
        cols = pl.ds(pl.multiple_of((j - n_att) * tn, tn), tn)
        o_ref[...] = x_ref[:, cols] + _dot(a_ref[...], wo_ref[...])


def cross_attention(x, g, w_xq, kv, w_xo, l, B, S, M, tm, tn):
    R, D = x.shape
    n_att = XA_HEADS // XA_HEADS_PER_STEP
    wh = D // n_att
    tiles_per_batch = S // tm
    att = lambda j: jnp.minimum(j, n_att - 1)
    otile = lambda j: jnp.maximum(j - n_att, 0)
    vmem = (2 * tm * D * 4 + 2 * tm * D * 2 + 4 * D * wh * 2 + 4 * M * wh * 2 + 4 * D * tn * 2 + 3 * tm * tn * 4
            + 3 * tm * wh * 4 + 3 * tm * M * 4)
    return pl.pallas_call(
        _xattn_body,
        grid=(R // tm, n_att + D // tn),
        in_specs=[
            pl.BlockSpec((tm, D), lambda i, j: (i, 0)),
            pl.BlockSpec((None, 1, D), lambda i, j: (l, 0, 0)),
            pl.BlockSpec((None, D, wh), lambda i, j: (l, 0, att(j))),
            pl.BlockSpec((None, M, wh), lambda i, j: (l, i // tiles_per_batch, att(j))),
            pl.BlockSpec((None, M, wh), lambda i, j: (l, i // tiles_per_batch, n_att + att(j))),
            pl.BlockSpec((None, D, tn), lambda i, j: (l, 0, otile(j))),
        ],
        out_specs=pl.BlockSpec((tm, tn), lambda i, j: (i, otile(j))),
        out_shape=jax.ShapeDtypeStruct((R, D), F32),
        scratch_shapes=[pltpu.VMEM((tm, D), BF16), pltpu.VMEM((tm, D), BF16)],
        compiler_params=_compiler_params(2, vmem),
        name="cross_attention",
    )(x, g, w_xq, kv, kv, w_xo)


def _ffn_body(x_ref, g_ref, wg_ref, wu_ref, cw_ref, cb_ref, wd_ref, og_ref, o_ref, h_ref, gbuf, carry, *,
              tiles_per_seq, norm_output):
    i = pl.program_id(0)
    j = pl.program_id(1)
    tm = x_ref.shape[0]
    K = cw_ref.shape[0]
    halo = V7X_SUBLANES

    def column_step(first):
        gbuf[0:halo, :] = jnp.where((i % tiles_per_seq) == 0, 0.0, carry[j])
        sub = min(FFN_FIRST_SUB_ROWS if first else FFN_SUB_ROWS, tm)

        def gate_up(r0):
            rows = slice(r0, r0 + sub)
            if first:
                x = x_ref[rows, :]
                h_ref[rows, :] = _rms_rows(x, g_ref[...]).astype(BF16)
                o_ref[rows, :] = x
            h = h_ref[rows, :]
            return _dot(h, wg_ref[...]), _dot(h, wu_ref[...])

        gate, up = gate_up(0)
        for r0 in range(0, tm, sub):
            nxt = gate_up(r0 + sub) if r0 + sub < tm else None
            gbuf[halo + r0:halo + r0 + sub, :] = gate
            conv = gate * cw_ref[K - 1:K, :] + cb_ref[...]
            for t in range(K - 2, -1, -1):
                conv = conv + gbuf[halo - K + 1 + t + r0:halo - K + 1 + t + r0 + sub, :] * cw_ref[t:t + 1, :]
            act = _silu(conv) * up
            o_ref[r0:r0 + sub, :] += _dot(act.astype(BF16), wd_ref[...])
            if nxt is not None:
                gate, up = nxt
        carry[j] = gbuf[tm:tm + halo, :]

    pl.when(j == 0)(functools.partial(column_step, True))
    pl.when(j > 0)(functools.partial(column_step, False))

    if norm_output:
        @pl.when(j == pl.num_programs(1) - 1)
        def _():
            _norm_rows_into(o_ref, o_ref, og_ref)


def conv_glu_ffn(x, g, w_gate, w_up, conv_w, conv_b, w_down, out_g, l, S, tm, tf, norm_output):
    R, D = x.shape
    F = w_gate.shape[2]
    K = conv_w.shape[1]
    nf = F // tf
    vmem = (4 * tm * D * 4 + tm * D * 2 + 2 * 3 * D * tf * 2 + (tm + V7X_SUBLANES) * tf * 4
            + nf * V7X_SUBLANES * tf * 4 + 4 * tm * tf * 4)
    return pl.pallas_call(
        functools.partial(_ffn_body, tiles_per_seq=S // tm, norm_output=norm_output),
        grid=(R // tm, nf),
        in_specs=[
            pl.BlockSpec((tm, D), lambda i, j: (i, 0)),
            pl.BlockSpec((None, 1, D), lambda i, j: (l, 0, 0)),
            pl.BlockSpec((None, D, tf), lambda i, j: (l, 0, j)),
            pl.BlockSpec((None, D, tf), lambda i, j: (l, 0, j)),
            pl.BlockSpec((None, K, tf), lambda i, j: (l, 0, j)),
            pl.BlockSpec((None, 1, tf), lambda i, j: (l, 0, j)),
            pl.BlockSpec((None, tf, D), lambda i, j: (l, j, 0)),
            pl.BlockSpec((1, D), lambda i, j: (0, 0)),
        ],
        out_specs=pl.BlockSpec((tm, D), lambda i, j: (i, 0)),
        out_shape=jax.ShapeDtypeStruct((R, D), F32),
        scratch_shapes=[
            pltpu.VMEM((tm, D), BF16),
            pltpu.VMEM((tm + V7X_SUBLANES, tf), F32),
            pltpu.VMEM((nf, V7X_SUBLANES, tf), F32),
        ],
        compiler_params=_compiler_params(2, vmem),
        name="conv_glu_ffn",
    )(x, g, w_gate, w_up, conv_w, conv_b, w_down, out_g)


def _tiles(S):
    return dict(tm=min(1024, S), ts_pool=min(512, S), ts_dn=min(512, S))


def kernel(x, mem, mix_norm_g, w_in, w_pool, pool_scale, dn_conv_w, dn_a_log, dn_dt_bias, dn_norm_g, w_mix_out,
           xa_norm_g, mem_norm_g, w_xq, w_xkv, w_xo, ffn_norm_g, w_gate, w_up, ffn_conv_w, ffn_conv_b, w_down,
           final_norm_g):
    B, S, D = x.shape
    M = mem.shape[1]
    depth = w_in.shape[0]
    H = dn_a_log.shape[1]
    dn_w = H * DN_HEAD_DIM
    pool_w = w_pool.shape[1] * w_pool.shape[2]
    main = pool_w + 4 * dn_w
    assert pool_w == dn_w and w_in.shape[2] == main + 2 * H and 2 * H <= V7X_LANES
    t = _tiles(S)

    tn = 512
    w_in_b = w_in[:, :, :main].astype(BF16)
    w_gates_b = jnp.pad(w_in[:, :, main:].astype(BF16), ((0, 0), (0, 0), (0, V7X_LANES - 2 * H)))
    pad_gate = lambda a: jnp.pad(a, ((0, 0), (H, V7X_LANES - 2 * H)))[:, None, :]
    a_log_p, dt_bias_p = pad_gate(dn_a_log), pad_gate(dn_dt_bias)
    w_pool_b = w_pool.astype(BF16)
    w_mix_b = w_mix_out.astype(BF16)
    w_xq_b, w_xkv_b, w_xo_b = w_xq.astype(BF16), w_xkv.astype(BF16), w_xo.astype(BF16)
    w_gate_b, w_up_b, w_down_b = w_gate.astype(BF16), w_up.astype(BF16), w_down.astype(BF16)
    row = lambda a: a[:, None, :]

    xr = x.reshape(B * S, D)
    kv = norm_matmul(mem.reshape(B * M, D), mem_norm_g[None, :], w_xkv_b, BF16, tm=min(1024, B * M), tn=tn)
    for l in range(depth):
        p, gates = in_proj(xr, mix_norm_g[l][None, :], w_in_b, w_gates_b, dn_conv_w, l, S, tm=t["tm"], tn=tn,
                           conv_col0=pool_w)
        y_pool = pool_mixer(p, w_pool_b, row(pool_scale), l, B, S, t["ts_pool"])
        y_dn = gated_delta_net(p, gates, a_log_p, dt_bias_p, row(dn_norm_g), l, B, S, t["ts_dn"],
                               col_block0=pool_w // dn_w, H=H)
        xr = mix_out(xr, y_pool, y_dn, w_mix_b, l, tm=t["tm"], tn=1024)
        xr = cross_attention(xr, row(xa_norm_g), w_xq_b, kv, w_xo_b, l, B, S, M, tm=t["tm"], tn=2 * tn)
        xr = conv_glu_ffn(xr, row(ffn_norm_g), w_gate_b, w_up_b, ffn_conv_w, row(ffn_conv_b), w_down_b,
                          final_norm_g[None, :], l, S, tm=t["tm"], tf=tn, norm_output=(l == depth - 1))
    return xr.reshape(B, S, D)
```

```python
import functools

import jax
import jax.numpy as jnp
from jax import lax
from jax.experimental import pallas as pl
from jax.experimental.pallas import tpu as pltpu

F32 = jnp.float32
BF16 = jnp.bfloat16

EPS = 1e-6
POOL_WINDOWS = (2, 4, 8, 16)
POOL_HALO = 16
DN_HEAD_DIM = 128
DN_CHUNK = 128
DN_HEADS_PER_STEP = 8
XA_HEADS = 4
XA_HEADS_PER_STEP = 2

V7X_VMEM_BYTES = 64 * 1024 * 1024
V7X_LANES = 128
V7X_SUBLANES = 8
NORM_ROWS = 128
XA_SUB_ROWS = 512
FFN_SUB_ROWS = 512
FFN_FIRST_SUB_ROWS = 256
NORM_SUB_ROWS = 256
IN_PROJ_SUB_ROWS = 128


def _compiler_params(n_axes, vmem_bytes):
    limit = min(int(vmem_bytes * 1.25) + (4 << 20), V7X_VMEM_BYTES * 7 // 8)
    return pltpu.CompilerParams(dimension_semantics=("arbitrary",) * n_axes, vmem_limit_bytes=limit)


def _dot(a, b):
    return jnp.dot(a, b, preferred_element_type=F32)


def _dot_nt(a, b):
    return lax.dot_general(a, b, (((1,), (1,)), ((), ())), preferred_element_type=F32)


def _silu(x):
    return x * (1.0 / (1.0 + jnp.exp(-x)))


def _rms_rows(x, g):
    ms = jnp.mean(x * x, axis=-1, keepdims=True)
    return x * lax.rsqrt(ms + EPS) * g


def _rows_shifted_down(a, prev, s):
    n = prev.shape[0]
    rows = lax.broadcasted_iota(jnp.int32, prev.shape, 0)
    rolled = [pltpu.roll(piece, s, axis=0) for piece in [prev] + [a[r:r + n, :] for r in range(0, a.shape[0], n)]]
    return jnp.concatenate([jnp.where(rows < s, lo, hi) for lo, hi in zip(rolled[:-1], rolled[1:])], axis=0)


def _norm_rows_into(h_ref, x_ref, g_ref):
    g = g_ref[...]

    def body(r, c):
        rows = pl.ds(pl.multiple_of(r * NORM_ROWS, NORM_ROWS), NORM_ROWS)
        h_ref[rows, :] = _rms_rows(x_ref[rows, :], g).astype(h_ref.dtype)
        return c

    lax.fori_loop(0, x_ref.shape[0] // NORM_ROWS, body, 0)


def _norm_matmul_body(x_ref, g_ref, w_ref, o_ref, h_ref):
    @pl.when((pl.program_id(1) == 0) & (pl.program_id(2) == 0))
    def _():
        _norm_rows_into(h_ref, x_ref, g_ref)

    o_ref[...] = _dot(h_ref[...], w_ref[...]).astype(o_ref.dtype)


def norm_matmul(x, g, w, out_dtype, tm, tn):
    R, K = x.shape
    L, _, N = w.shape
    osz = jnp.dtype(out_dtype).itemsize
    vmem = 2 * tm * K * 4 + tm * K * 2 + 2 * K * tn * 2 + 2 * tm * tn * osz + tm * tn * 4
    return pl.pallas_call(
        _norm_matmul_body,
        grid=(R // tm, L, N // tn),
        in_specs=[
            pl.BlockSpec((tm, K), lambda i, l, j: (i, 0)),
            pl.BlockSpec((1, K), lambda i, l, j: (0, 0)),
            pl.BlockSpec((None, K, tn), lambda i, l, j: (l, 0, j)),
        ],
        out_specs=pl.BlockSpec((None, tm, tn), lambda i, l, j: (l, i, j)),
        out_shape=jax.ShapeDtypeStruct((L, R, N), out_dtype),
        scratch_shapes=[pltpu.VMEM((tm, K), BF16)],
        compiler_params=_compiler_params(3, vmem),
        name="norm_matmul",
    )(x, g, w)


def _in_proj_body(x_ref, g_ref, w_ref, wg_ref, cw_ref, p_ref, gate_ref, h_ref, carry, *,
                  n_main, tiles_per_seq, conv0, tiles_per_part):
    i = pl.program_id(0)
    j = pl.program_id(1)
    tm, tn = p_ref.shape
    K = cw_ref.shape[0]
    halo = V7X_SUBLANES

    is_conv = (j >= conv0) & (j < conv0 + 3 * tiles_per_part)

    @pl.when(j == 0)
    def _():
        sub = min(NORM_SUB_ROWS, tm)
        for r0 in range(0, tm, sub):
            h = _rms_rows(x_ref[r0:r0 + sub, :], g_ref[...]).astype(BF16)
            h_ref[r0:r0 + sub, :] = h
            p_ref[r0:r0 + sub, :] = _dot(h, w_ref[...])

    @pl.when(j == n_main)
    def _():
        gate_ref[...] = _dot(h_ref[...], wg_ref[...])

    @pl.when(jnp.logical_not(is_conv) & (j > 0) & (j < n_main))
    def _():
        p_ref[...] = _dot(h_ref[...], w_ref[...])

    @pl.when(is_conv)
    def _():
        cj = j - conv0
        part = cj // tiles_per_part
        q_scale = jnp.where(part == 0, DN_HEAD_DIM ** -0.5, 1.0)
        prev = jnp.where((i % tiles_per_seq) == 0, 0.0, carry[cj])
        sub = min(IN_PROJ_SUB_ROWS, tm)
        for r0 in range(0, tm, sub):
            acc = _dot(h_ref[r0:r0 + sub, :], w_ref[...])
            y = acc * cw_ref[K - 1:K, :]
            for s in range(1, K):
                y = y + _rows_shifted_down(acc, prev, s) * cw_ref[K - 1 - s:K - s, :]
            prev = acc[sub - halo:sub, :]
            y = _silu(y)
            for c in range(0, tn, DN_HEAD_DIM):
                yh = y[:, c:c + DN_HEAD_DIM]
                inv = lax.rsqrt(jnp.sum(yh * yh, axis=-1, keepdims=True) + EPS) * q_scale
                p_ref[r0:r0 + sub, c:c + DN_HEAD_DIM] = yh * jnp.where(part == 2, 1.0, inv)
        carry[cj] = prev


def in_proj(x, g, w, w_gates, conv_w, l, S, tm, tn, conv_col0):
    R, D = x.shape
    N = w.shape[2]
    NG = w_gates.shape[2]
    K, conv_cols = conv_w.shape[1:]
    assert N % tn == 0 and conv_col0 % tn == 0 and (conv_cols // 3) % tn == 0 and tn % DN_HEAD_DIM == 0
    n_main, conv0, n_conv = N // tn, conv_col0 // tn, conv_cols // tn
    assert conv0 >= 1
    main = lambda j: jnp.minimum(j, n_main - 1)
    vmem = (2 * tm * D * 4 + tm * D * 2 + 2 * D * tn * 2 + 2 * D * NG * 2 + 3 * tm * tn * 4 + 2 * tm * NG * 4
            + n_conv * V7X_SUBLANES * tn * 4 + 8 * IN_PROJ_SUB_ROWS * tn * 4)
    return pl.pallas_call(
        functools.partial(_in_proj_body, n_main=n_main, tiles_per_seq=S // tm, conv0=conv0,
                          tiles_per_part=n_conv // 3),
        grid=(R // tm, n_main + 1),
        in_specs=[
            pl.BlockSpec((tm, D), lambda i, j: (i, 0)),
            pl.BlockSpec((1, D), lambda i, j: (0, 0)),
            pl.BlockSpec((None, D, tn), lambda i, j: (l, 0, main(j))),
            pl.BlockSpec((None, D, NG), lambda i, j: (l, 0, 0)),
            pl.BlockSpec((None, K, tn), lambda i, j: (l, 0, jnp.clip(j - conv0, 0, n_conv - 1))),
        ],
        out_specs=[
            pl.BlockSpec((tm, tn), lambda i, j: (i, main(j))),
            pl.BlockSpec((tm, NG), lambda i, j: (i, 0)),
        ],
        out_shape=[jax.ShapeDtypeStruct((R, N), F32), jax.ShapeDtypeStruct((R, NG), F32)],
        scratch_shapes=[
            pltpu.VMEM((tm, D), BF16),
            pltpu.VMEM((n_conv, V7X_SUBLANES, tn), F32),
        ],
        compiler_params=_compiler_params(2, vmem),
        name="in_proj",
    )(x, g, w, w_gates, conv_w)


def _pool_body(u_ref, wp_ref, sc_ref, o_ref, buf):
    s = pl.program_id(1)
    ts = u_ref.shape[0]
    G = wp_ref.shape[1]

    @pl.when(s == 0)
    def _():
        buf[0:POOL_HALO, :] = jnp.zeros((POOL_HALO, buf.shape[1]), F32)

    buf[POOL_HALO:POOL_HALO + ts, :] = u_ref[...]
    pos = (s * ts + 1 + lax.broadcasted_iota(jnp.int32, (ts, 1), 0)).astype(F32)
    for i, w in enumerate(POOL_WINDOWS):
        cols = slice(i * G, (i + 1) * G)
        u = buf[POOL_HALO:POOL_HALO + ts, cols]
        acc = u
        for k in range(1, w):
            acc = acc + buf[POOL_HALO - k:POOL_HALO - k + ts, cols]
        mixed = acc / jnp.minimum(pos, float(w)) - u
        y = _dot(mixed.astype(BF16), wp_ref[i]) * sc_ref[:, cols]
        o_ref[:, cols] = y.astype(o_ref.dtype)
    buf[0:POOL_HALO, :] = buf[ts:ts + POOL_HALO, :]


def _unit_lower_inverses(ls, eye, ii, jj):
    C = ls[0].shape[0]
    size = 2
    same = (ii // size) == (jj // size)
    xs = [eye - jnp.where(same, l_mat, 0.0) for l_mat in ls]
    while size < C:
        inner, same = same, (ii // (2 * size)) == (jj // (2 * size))
        offs = [jnp.where(same & jnp.logical_not(inner), l_mat, 0.0).astype(BF16) for l_mat in ls]
        xbs = [x.astype(BF16) for x in xs]
        ys = [_dot(off, xb) for off, xb in zip(offs, xbs)]
        xs = [x - _dot(xb, y.astype(BF16)) for x, xb, y in zip(xs, xbs, ys)]
        size *= 2
    return xs


def _dn_body(q_ref, k_ref, v_ref, z_ref, ba_ref, alog_ref, dtb_ref, ng_ref, pu_ref, wp_ref, psc_ref, o_ref, yp_ref,
             state, beta_rep, gc_rep, grow_s, cdec_s, u_s, w_s, qd_s, at_s, kdt_s, pool_buf):
    s = pl.program_id(1)
    ts, W = q_ref.shape
    H = W // DN_HEAD_DIM
    C = DN_CHUNK
    n_chunk = ts // C

    @pl.when(s == 0)
    def _():
        state[...] = jnp.zeros(state.shape, F32)

    ba = ba_ref[...]
    beta = 1.0 / (1.0 + jnp.exp(-ba))
    xs = ba + dtb_ref[...]
    softplus = jnp.maximum(xs, 0.0) + jnp.log1p(jnp.exp(-jnp.abs(xs)))
    g = -jnp.exp(alog_ref[...]) * softplus
    row_in_chunk = lax.broadcasted_iota(jnp.int32, (ts, V7X_LANES), 0) % C
    gc = g
    sh = 1
    while sh < C:
        gc = gc + jnp.where(row_in_chunk >= sh, pltpu.roll(gc, sh, axis=0), 0.0)
        sh *= 2
    gct = gc.T
    lane = lax.broadcasted_iota(jnp.int32, (ts, V7X_LANES), 1)
    for h in range(H):
        beta_rep[h] = jnp.broadcast_to(
            jnp.sum(jnp.where(lane == h, beta, 0.0), axis=-1, keepdims=True), (ts, V7X_LANES))
        gc_rep[h] = jnp.broadcast_to(
            jnp.sum(jnp.where(lane == H + h, gc, 0.0), axis=-1, keepdims=True), (ts, V7X_LANES))
        grow_s[h] = gct[H + h:H + h + 1, :]

    ii = lax.broadcasted_iota(jnp.int32, (C, C), 0)
    jj = lax.broadcasted_iota(jnp.int32, (C, C), 1)
    eye = (ii == jj).astype(F32)

    def head_group_body(hg, carry):
        pairs = []
        for hh in range(DN_HEADS_PER_STEP):
            h = hg * DN_HEADS_PER_STEP + hh
            cols = pl.ds(pl.multiple_of(h * DN_HEAD_DIM, DN_HEAD_DIM), DN_HEAD_DIM)
            bcol, gcol, grow_all = beta_rep[h], gc_rep[h], grow_s[h]
            egc = jnp.exp(gcol)
            for c in range(n_chunk):
                r = slice(c * C, (c + 1) * C)
                pairs.append(dict(h=h, c=c, r=r, cols=cols, q=q_ref[r, cols], k=k_ref[r, cols], v=v_ref[r, cols],
                                  b=bcol[r], g=gcol[r], e=egc[r], grow=grow_all[:, r]))
        kbs = [p["k"].astype(BF16) for p in pairs]
        decs = [jnp.exp(jnp.minimum(p["g"] - p["grow"], 0.0)) for p in pairs]
        kks = [_dot_nt(kb, kb) for kb in kbs]
        qks = [_dot_nt(p["q"].astype(BF16), kb) for p, kb in zip(pairs, kbs)]
        ls = [jnp.where(ii > jj, kk * dec, 0.0) * p["b"] for kk, dec, p in zip(kks, decs, pairs)]
        for p, qk, dec in zip(pairs, qks, decs):
            r, cols = p["r"], p["cols"]
            at_s[r, cols] = jnp.where(ii >= jj, qk * dec, 0.0).astype(BF16)
            qd_s[r, cols] = (p["q"] * p["e"]).astype(BF16)
            kd = p["k"] * jnp.exp(p["g"][C - 1:C, :] - p["g"])
            kdt_s[r, cols] = kd.T.astype(BF16)
            cdec_s[p["c"], p["h"]] = p["e"][C - 1:C, :]
        t_invs = _unit_lower_inverses(ls, eye, ii, jj)
        rhs = [jnp.concatenate([p["v"] * p["b"], p["k"] * (p["b"] * p["e"])], axis=1) for p in pairs]
        sols = [_dot(t_inv.astype(BF16), b.astype(BF16)) for t_inv, b in zip(t_invs, rhs)]
        for p, sol in zip(pairs, sols):
            u_s[p["r"], p["cols"]] = sol[:, :DN_HEAD_DIM]
            w_s[p["r"], p["cols"]] = sol[:, DN_HEAD_DIM:].astype(BF16)
        return carry

    lax.fori_loop(0, H // DN_HEADS_PER_STEP, head_group_body, 0)

    def chunk_body(c, carry):
        r0 = pl.multiple_of(c * C, C)
        rows = pl.ds(r0, C)
        hs = range(H)
        cols = [slice(h * DN_HEAD_DIM, (h + 1) * DN_HEAD_DIM) for h in hs]
        sts = [state[h] for h in hs]
        sbs = [st.astype(BF16) for st in sts]
        wss = [_dot(w_s[rows, cols[h]], sbs[h]) for h in hs]
        qss = [_dot(qd_s[rows, cols[h]], sbs[h]) for h in hs]
        vbs = [(u_s[rows, cols[h]] - wss[h]).astype(BF16) for h in hs]
        avs = [_dot(at_s[rows, cols[h]], vbs[h]) for h in hs]
        kvs = [_dot(kdt_s[rows, cols[h]], vbs[h]) for h in hs]
        for h in hs:
            state[h] = sts[h] * cdec_s[c, h] + kvs[h]
            gated = _rms_rows(qss[h] + avs[h], ng_ref[...]) * _silu(z_ref[rows, cols[h]])
            o_ref[rows, cols[h]] = gated.astype(o_ref.dtype)
        return carry

    lax.fori_loop(0, n_chunk, chunk_body, 0, unroll=True)
    _pool_body(pu_ref, wp_ref, psc_ref, yp_ref, pool_buf)


def pool_and_delta_net(p, gates, w_pool, pool_scale, a_log, dt_bias, norm_g, l, B, S, ts, col_block0, H):
    n_win, G, _ = w_pool.shape[1:]
    PW = n_win * G
    W = H * DN_HEAD_DIM
    nt = S // ts
    NB = gates.shape[1]
    C = DN_CHUNK
    vmem = (2 * 4 * ts * W * 4 + 2 * ts * NB * 4 + 2 * ts * W * 2
            + H * DN_HEAD_DIM * DN_HEAD_DIM * 4 + 2 * H * ts * V7X_LANES * 4 + H * V7X_SUBLANES * ts * 4
            + ts * W * 4 + 4 * ts * W * 2 + 48 * C * C * 4 * DN_HEADS_PER_STEP
            + 2 * ts * PW * 4 + (ts + POOL_HALO) * PW * 4 + 2 * n_win * G * G * 2 + 2 * ts * PW * 2)

    def pblock(k):
        return pl.BlockSpec((ts, W), lambda b, s: (b * nt + s, col_block0 + k))

    y_dn, y_pool = pl.pallas_call(
        _dn_body,
        grid=(B, nt),
        in_specs=[
            pblock(0), pblock(1), pblock(2), pblock(3),
            pl.BlockSpec((ts, NB), lambda b, s: (b * nt + s, 0)),
            pl.BlockSpec((None, 1, NB), lambda b, s: (l, 0, 0)),
            pl.BlockSpec((None, 1, NB), lambda b, s: (l, 0, 0)),
            pl.BlockSpec((None, 1, DN_HEAD_DIM), lambda b, s: (l, 0, 0)),
            pl.BlockSpec((ts, PW), lambda b, s: (b * nt + s, 0)),
            pl.BlockSpec((None, n_win, G, G), lambda b, s: (l, 0, 0, 0)),
            pl.BlockSpec((None, 1, PW), lambda b, s: (l, 0, 0)),
        ],
        out_specs=[
            pl.BlockSpec((ts, W), lambda b, s: (b * nt + s, 0)),
            pl.BlockSpec((ts, PW), lambda b, s: (b * nt + s, 0)),
        ],
        out_shape=[jax.ShapeDtypeStruct((B * S, W), BF16), jax.ShapeDtypeStruct((B * S, PW), BF16)],
        scratch_shapes=[
            pltpu.VMEM((H, DN_HEAD_DIM, DN_HEAD_DIM), F32),
            pltpu.VMEM((H, ts, V7X_LANES), F32),
            pltpu.VMEM((H, ts, V7X_LANES), F32),
            pltpu.VMEM((H, 1, ts), F32),
            pltpu.VMEM((ts // C, H, 1, V7X_LANES), F32),
            pltpu.VMEM((ts, W), F32),
            pltpu.VMEM((ts, W), BF16),
            pltpu.VMEM((ts, W), BF16),
            pltpu.VMEM((ts, W), BF16),
            pltpu.VMEM((ts, W), BF16),
            pltpu.VMEM((ts + POOL_HALO, PW), F32),
        ],
        compiler_params=_compiler_params(2, vmem),
        name="pool_and_delta_net",
    )(p, p, p, p, gates, a_log, dt_bias, norm_g, p, w_pool, pool_scale)
    return y_pool, y_dn


def _mix_out_body(x_ref, yp_ref, yd_ref, w1_ref, w2_ref, o_ref):
    o_ref[...] = x_ref[...] + _dot(yp_ref[...], w1_ref[...]) + _dot(yd_ref[...], w2_ref[...])


def mix_out(x, y_pool, y_dn, w, l, tm, tn):
    R, D = x.shape
    K1, K2 = y_pool.shape[1], y_dn.shape[1]
    assert K1 == K2
    vmem = 2 * (2 * tm * tn * 4 + tm * (K1 + K2) * 2 + (K1 + K2) * tn * 2) + 2 * tm * tn * 4
    return pl.pallas_call(
        _mix_out_body,
        grid=(R // tm, D // tn),
        in_specs=[
            pl.BlockSpec((tm, tn), lambda i, j: (i, j)),
            pl.BlockSpec((tm, K1), lambda i, j: (i, 0)),
            pl.BlockSpec((tm, K2), lambda i, j: (i, 0)),
            pl.BlockSpec((None, K1, tn), lambda i, j: (l, 0, j)),
            pl.BlockSpec((None, K2, tn), lambda i, j: (l, 1, j)),
        ],
        out_specs=pl.BlockSpec((tm, tn), lambda i, j: (i, j)),
        out_shape=jax.ShapeDtypeStruct((R, D), F32),
        compiler_params=_compiler_params(2, vmem),
        name="mix_out",
    )(x, y_pool, y_dn, w, w)


def _xattn_body(x_ref, g_ref, wq_ref, k_ref, v_ref, wo_ref, o_ref, h_ref, a_ref):
    j = pl.program_id(1)
    dh = wq_ref.shape[1] // XA_HEADS_PER_STEP
    tn = wo_ref.shape[1]
    n_att = XA_HEADS // XA_HEADS_PER_STEP

    def attention_step(first):
        tm = h_ref.shape[0]
        sub = min(XA_SUB_ROWS, tm)
        col0 = pl.multiple_of(j * (XA_HEADS_PER_STEP * dh), XA_HEADS_PER_STEP * dh)
        units = [(hh, r0) for hh in range(XA_HEADS_PER_STEP) for r0 in range(0, tm, sub)]

        def scores(unit):
            hh, r0 = unit
            if first and hh == 0:
                h_ref[r0:r0 + sub, :] = _rms_rows(x_ref[r0:r0 + sub, :], g_ref[...]).astype(BF16)
            q = _dot(h_ref[r0:r0 + sub, :], wq_ref[:, hh * dh:(hh + 1) * dh])
            return _dot_nt(q.astype(BF16), k_ref[:, hh * dh:(hh + 1) * dh]) * (dh ** -0.5)

        sc = scores(units[0])
        for n, (hh, r0) in enumerate(units):
            sc_next = scores(units[n + 1]) if n + 1 < len(units) else None
            e = jnp.exp(sc - jnp.max(sc, axis=-1, keepdims=True))
            pr = e / jnp.sum(e, axis=-1, keepdims=True)
            o = _dot(pr.astype(BF16), v_ref[:, hh * dh:(hh + 1) * dh])
            a_ref[r0:r0 + sub, pl.ds(col0 + hh * dh, dh)] = o.astype(BF16)
            sc = sc_next

    pl.when(j == 0)(functools.partial(attention_step, True))
    pl.when((j > 0) & (j < n_att))(functools.partial(attention_step, False))

    @pl.when(j >= n_att)
    def _():
        cols = pl.ds(pl.multiple_of((j - n_att) * tn, tn), tn)
        o_ref[...] = x_ref[:, cols] + _dot(a_ref[...], wo_ref[...])


def cross_attention(x, g, w_xq, kv, w_xo, l, B, S, M, tm, tn):
    R, D = x.shape
    n_att = XA_HEADS // XA_HEADS_PER_STEP
    wh = D // n_att
    tiles_per_batch = S // tm
    att = lambda j: jnp.minimum(j, n_att - 1)
    otile = lambda j: jnp.maximum(j - n_att, 0)
    vmem = (2 * tm * D * 4 + 2 * tm * D * 2 + 4 * D * wh * 2 + 4 * M * wh * 2 + 4 * D * tn * 2 + 3 * tm * tn * 4
            + 3 * tm * wh * 4 + 3 * tm * M * 4)
    return pl.pallas_call(
        _xattn_body,
        grid=(R // tm, n_att + D // tn),
        in_specs=[
            pl.BlockSpec((tm, D), lambda i, j: (i, 0)),
            pl.BlockSpec((None, 1, D), lambda i, j: (l, 0, 0)),
            pl.BlockSpec((None, D, wh), lambda i, j: (l, 0, att(j))),
            pl.BlockSpec((None, M, wh), lambda i, j: (l, i // tiles_per_batch, att(j))),
            pl.BlockSpec((None, M, wh), lambda i, j: (l, i // tiles_per_batch, n_att + att(j))),
            pl.BlockSpec((None, D, tn), lambda i, j: (l, 0, otile(j))),
        ],
        out_specs=pl.BlockSpec((tm, tn), lambda i, j: (i, otile(j))),
        out_shape=jax.ShapeDtypeStruct((R, D), F32),
        scratch_shapes=[pltpu.VMEM((tm, D), BF16), pltpu.VMEM((tm, D), BF16)],
        compiler_params=_compiler_params(2, vmem),
        name="cross_attention",
    )(x, g, w_xq, kv, kv, w_xo)


def _ffn_body(x_ref, g_ref, wg_ref, wu_ref, cw_ref, cb_ref, wd_ref, og_ref, o_ref, h_ref, gbuf, carry, *,
              tiles_per_seq, norm_output):
    i = pl.program_id(0)
    j = pl.program_id(1)
    tm = x_ref.shape[0]
    K = cw_ref.shape[0]
    halo = V7X_SUBLANES

    def column_step(first):
        gbuf[0:halo, :] = jnp.where((i % tiles_per_seq) == 0, 0.0, carry[j])
        sub = min(FFN_FIRST_SUB_ROWS if first else FFN_SUB_ROWS, tm)

        def gate_up(r0):
            rows = slice(r0, r0 + sub)
            if first:
                x = x_ref[rows, :]
                h_ref[rows, :] = _rms_rows(x, g_ref[...]).astype(BF16)
                o_ref[rows, :] = x
            h = h_ref[rows, :]
            return _dot(h, wg_ref[...]), _dot(h, wu_ref[...])

        gate, up = gate_up(0)
        for r0 in range(0, tm, sub):
            nxt = gate_up(r0 + sub) if r0 + sub < tm else None
            gbuf[halo + r0:halo + r0 + sub, :] = gate
            conv = gate * cw_ref[K - 1:K, :] + cb_ref[...]
            for t in range(K - 2, -1, -1):
                conv = conv + gbuf[halo - K + 1 + t + r0:halo - K + 1 + t + r0 + sub, :] * cw_ref[t:t + 1, :]
            act = _silu(conv) * up
            o_ref[r0:r0 + sub, :] += _dot(act.astype(BF16), wd_ref[...])
            if nxt is not None:
                gate, up = nxt
        carry[j] = gbuf[tm:tm + halo, :]

    pl.when(j == 0)(functools.partial(column_step, True))
    pl.when(j > 0)(functools.partial(column_step, False))

    if norm_output:
        @pl.when(j == pl.num_programs(1) - 1)
        def _():
            _norm_rows_into(o_ref, o_ref, og_ref)


def conv_glu_ffn(x, g, w_gate, w_up, conv_w, conv_b, w_down, out_g, l, S, tm, tf, norm_output):
    R, D = x.shape
    F = w_gate.shape[2]
    K = conv_w.shape[1]
    nf = F // tf
    vmem = (4 * tm * D * 4 + tm * D * 2 + 2 * 3 * D * tf * 2 + (tm + V7X_SUBLANES) * tf * 4
            + nf * V7X_SUBLANES * tf * 4 + 4 * tm * tf * 4)
    return pl.pallas_call(
        functools.partial(_ffn_body, tiles_per_seq=S // tm, norm_output=norm_output),
        grid=(R // tm, nf),
        in_specs=[
            pl.BlockSpec((tm, D), lambda i, j: (i, 0)),
            pl.BlockSpec((None, 1, D), lambda i, j: (l, 0, 0)),
            pl.BlockSpec((None, D, tf), lambda i, j: (l, 0, j)),
            pl.BlockSpec((None, D, tf), lambda i, j: (l, 0, j)),
            pl.BlockSpec((None, K, tf), lambda i, j: (l, 0, j)),
            pl.BlockSpec((None, 1, tf), lambda i, j: (l, 0, j)),
            pl.BlockSpec((None, tf, D), lambda i, j: (l, j, 0)),
            pl.BlockSpec((1, D), lambda i, j: (0, 0)),
        ],
        out_specs=pl.BlockSpec((tm, D), lambda i, j: (i, 0)),
        out_shape=jax.ShapeDtypeStruct((R, D), F32),
        scratch_shapes=[
            pltpu.VMEM((tm, D), BF16),
            pltpu.VMEM((tm + V7X_SUBLANES, tf), F32),
            pltpu.VMEM((nf, V7X_SUBLANES, tf), F32),
        ],
        compiler_params=_compiler_params(2, vmem),
        name="conv_glu_ffn",
    )(x, g, w_gate, w_up, conv_w, conv_b, w_down, out_g)


def _tiles(S):
    return dict(tm=min(1024, S), ts_mix=min(512, S))


def kernel(x, mem, mix_norm_g, w_in, w_pool, pool_scale, dn_conv_w, dn_a_log, dn_dt_bias, dn_norm_g, w_mix_out,
           xa_norm_g, mem_norm_g, w_xq, w_xkv, w_xo, ffn_norm_g, w_gate, w_up, ffn_conv_w, ffn_conv_b, w_down,
           final_norm_g):
    B, S, D = x.shape
    M = mem.shape[1]
    depth = w_in.shape[0]
    H = dn_a_log.shape[1]
    dn_w = H * DN_HEAD_DIM
    pool_w = w_pool.shape[1] * w_pool.shape[2]
    main = pool_w + 4 * dn_w
    assert pool_w == dn_w and w_in.shape[2] == main + 2 * H and 2 * H <= V7X_LANES
    t = _tiles(S)

    tn = 512
    w_in_b = w_in[:, :, :main].astype(BF16)
    w_gates_b = jnp.pad(w_in[:, :, main:].astype(BF16), ((0, 0), (0, 0), (0, V7X_LANES - 2 * H)))
    pad_gate = lambda a: jnp.pad(a, ((0, 0), (H, V7X_LANES - 2 * H)))[:, None, :]
    a_log_p, dt_bias_p = pad_gate(dn_a_log), pad_gate(dn_dt_bias)
    w_pool_b = w_pool.astype(BF16)
    w_mix_b = w_mix_out.astype(BF16)
    w_xq_b, w_xkv_b, w_xo_b = w_xq.astype(BF16), w_xkv.astype(BF16), w_xo.astype(BF16)
    w_gate_b, w_up_b, w_down_b = w_gate.astype(BF16), w_up.astype(BF16), w_down.astype(BF16)
    row = lambda a: a[:, None, :]

    xr = x.reshape(B * S, D)
    kv = norm_matmul(mem.reshape(B * M, D), mem_norm_g[None, :], w_xkv_b, BF16, tm=min(1024, B * M), tn=tn)
    for l in range(depth):
        p, gates = in_proj(xr, mix_norm_g[l][None, :], w_in_b, w_gates_b, dn_conv_w, l, S, tm=t["tm"], tn=tn,
                           conv_col0=pool_w)
        y_pool, y_dn = pool_and_delta_net(p, gates, w_pool_b, row(pool_scale), a_log_p, dt_bias_p, row(dn_norm_g),
                                          l, B, S, t["ts_mix"], col_block0=pool_w // dn_w, H=H)
        xr = mix_out(xr, y_pool, y_dn, w_mix_b, l, tm=t["tm"] // 2, tn=D)
        xr = cross_attention(xr, row(xa_norm_g), w_xq_b, kv, w_xo_b, l, B, S, M, tm=t["tm"], tn=2 * tn)
        xr = conv_glu_ffn(xr, row(ffn_norm_g), w_gate_b, w_up_b, ffn_conv_w, row(ffn_conv_b), w_down_b,
                          final_norm_g[None, :], l, S, tm=t["tm"], tf=tn, norm_output=(l == depth - 1))
    return xr.reshape(B, S, D)
```

```python
import functools

import jax
import jax.numpy as jnp
from jax import lax
from jax.experimental import pallas as pl
from jax.experimental.pallas import tpu as pltpu

F32 = jnp.float32
BF16 = jnp.bfloat16

EPS = 1e-6
POOL_WINDOWS = (2, 4, 8, 16)
POOL_HALO = 16
DN_HEAD_DIM = 128
DN_CHUNK = 128
DN_HEADS_PER_STEP = 4
XA_HEADS = 4
XA_HEADS_PER_STEP = 2

V7X_VMEM_BYTES = 64 * 1024 * 1024
V7X_LANES = 128
V7X_SUBLANES = 8
NORM_ROWS = 128
XA_SUB_ROWS = 512
FFN_SUB_ROWS = 512
FFN_FIRST_SUB_ROWS = 256
NORM_SUB_ROWS = 256
IN_PROJ_SUB_ROWS = 128


def _compiler_params(n_axes, vmem_bytes):
    limit = min(int(vmem_bytes * 1.25) + (4 << 20), V7X_VMEM_BYTES * 7 // 8)
    return pltpu.CompilerParams(dimension_semantics=("arbitrary",) * n_axes, vmem_limit_bytes=limit)


def _dot(a, b):
    return jnp.dot(a, b, preferred_element_type=F32)


def _dot_nt(a, b):
    return lax.dot_general(a, b, (((1,), (1,)), ((), ())), preferred_element_type=F32)


def _silu(x):
    return x * (1.0 / (1.0 + jnp.exp(-x)))


def _rms_rows(x, g):
    ms = jnp.mean(x * x, axis=-1, keepdims=True)
    return x * lax.rsqrt(ms + EPS) * g


def _rows_shifted_down(a, prev, s):
    n = prev.shape[0]
    rows = lax.broadcasted_iota(jnp.int32, prev.shape, 0)
    rolled = [pltpu.roll(piece, s, axis=0) for piece in [prev] + [a[r:r + n, :] for r in range(0, a.shape[0], n)]]
    return jnp.concatenate([jnp.where(rows < s, lo, hi) for lo, hi in zip(rolled[:-1], rolled[1:])], axis=0)


def _norm_rows_into(h_ref, x_ref, g_ref):
    g = g_ref[...]

    def body(r, c):
        rows = pl.ds(pl.multiple_of(r * NORM_ROWS, NORM_ROWS), NORM_ROWS)
        h_ref[rows, :] = _rms_rows(x_ref[rows, :], g).astype(h_ref.dtype)
        return c

    lax.fori_loop(0, x_ref.shape[0] // NORM_ROWS, body, 0)


def _norm_matmul_body(x_ref, g_ref, w_ref, o_ref, h_ref):
    @pl.when((pl.program_id(1) == 0) & (pl.program_id(2) == 0))
    def _():
        _norm_rows_into(h_ref, x_ref, g_ref)

    o_ref[...] = _dot(h_ref[...], w_ref[...]).astype(o_ref.dtype)


def norm_matmul(x, g, w, out_dtype, tm, tn):
    R, K = x.shape
    L, _, N = w.shape
    osz = jnp.dtype(out_dtype).itemsize
    vmem = 2 * tm * K * 4 + tm * K * 2 + 2 * K * tn * 2 + 2 * tm * tn * osz + tm * tn * 4
    return pl.pallas_call(
        _norm_matmul_body,
        grid=(R // tm, L, N // tn),
        in_specs=[
            pl.BlockSpec((tm, K), lambda i, l, j: (i, 0)),
            pl.BlockSpec((1, K), lambda i, l, j: (0, 0)),
            pl.BlockSpec((None, K, tn), lambda i, l, j: (l, 0, j)),
        ],
        out_specs=pl.BlockSpec((None, tm, tn), lambda i, l, j: (l, i, j)),
        out_shape=jax.ShapeDtypeStruct((L, R, N), out_dtype),
        scratch_shapes=[pltpu.VMEM((tm, K), BF16)],
        compiler_params=_compiler_params(3, vmem),
        name="norm_matmul",
    )(x, g, w)


def _in_proj_body(x_ref, g_ref, w_ref, wg_ref, cw_ref, p_ref, gate_ref, h_ref, carry, *,
                  n_main, tiles_per_seq, conv0, tiles_per_part):
    i = pl.program_id(0)
    j = pl.program_id(1)
    tm, tn = p_ref.shape
    K = cw_ref.shape[0]
    halo = V7X_SUBLANES

    is_conv = (j >= conv0) & (j < conv0 + 3 * tiles_per_part)

    @pl.when(j == 0)
    def _():
        sub = min(NORM_SUB_ROWS, tm)
        for r0 in range(0, tm, sub):
            h = _rms_rows(x_ref[r0:r0 + sub, :], g_ref[...]).astype(BF16)
            h_ref[r0:r0 + sub, :] = h
            p_ref[r0:r0 + sub, :] = _dot(h, w_ref[...])

    @pl.when(j == n_main)
    def _():
        gate_ref[...] = _dot(h_ref[...], wg_ref[...])

    @pl.when(jnp.logical_not(is_conv) & (j > 0) & (j < n_main))
    def _():
        p_ref[...] = _dot(h_ref[...], w_ref[...])

    @pl.when(is_conv)
    def _():
        cj = j - conv0
        part = cj // tiles_per_part
        q_scale = jnp.where(part == 0, DN_HEAD_DIM ** -0.5, 1.0)
        prev = jnp.where((i % tiles_per_seq) == 0, 0.0, carry[cj])
        sub = min(IN_PROJ_SUB_ROWS, tm)
        for r0 in range(0, tm, sub):
            acc = _dot(h_ref[r0:r0 + sub, :], w_ref[...])
            y = acc * cw_ref[K - 1:K, :]
            for s in range(1, K):
                y = y + _rows_shifted_down(acc, prev, s) * cw_ref[K - 1 - s:K - s, :]
            prev = acc[sub - halo:sub, :]
            y = _silu(y)
            for c in range(0, tn, DN_HEAD_DIM):
                yh = y[:, c:c + DN_HEAD_DIM]
                inv = lax.rsqrt(jnp.sum(yh * yh, axis=-1, keepdims=True) + EPS) * q_scale
                p_ref[r0:r0 + sub, c:c + DN_HEAD_DIM] = yh * jnp.where(part == 2, 1.0, inv)
        carry[cj] = prev


def in_proj(x, g, w, w_gates, conv_w, l, S, tm, tn, conv_col0):
    R, D = x.shape
    N = w.shape[2]
    NG = w_gates.shape[2]
    K, conv_cols = conv_w.shape[1:]
    assert N % tn == 0 and conv_col0 % tn == 0 and (conv_cols // 3) % tn == 0 and tn % DN_HEAD_DIM == 0
    n_main, conv0, n_conv = N // tn, conv_col0 // tn, conv_cols // tn
    assert conv0 >= 1
    main = lambda j: jnp.minimum(j, n_main - 1)
    vmem = (2 * tm * D * 4 + tm * D * 2 + 2 * D * tn * 2 + 2 * D * NG * 2 + 3 * tm * tn * 4 + 2 * tm * NG * 4
            + n_conv * V7X_SUBLANES * tn * 4 + 8 * IN_PROJ_SUB_ROWS * tn * 4)
    return pl.pallas_call(
        functools.partial(_in_proj_body, n_main=n_main, tiles_per_seq=S // tm, conv0=conv0,
                          tiles_per_part=n_conv // 3),
        grid=(R // tm, n_main + 1),
        in_specs=[
            pl.BlockSpec((tm, D), lambda i, j: (i, 0)),
            pl.BlockSpec((1, D), lambda i, j: (0, 0)),
            pl.BlockSpec((None, D, tn), lambda i, j: (l, 0, main(j))),
            pl.BlockSpec((None, D, NG), lambda i, j: (l, 0, 0)),
            pl.BlockSpec((None, K, tn), lambda i, j: (l, 0, jnp.clip(j - conv0, 0, n_conv - 1))),
        ],
        out_specs=[
            pl.BlockSpec((tm, tn), lambda i, j: (i, main(j))),
            pl.BlockSpec((tm, NG), lambda i, j: (i, 0)),
        ],
        out_shape=[jax.ShapeDtypeStruct((R, N), F32), jax.ShapeDtypeStruct((R, NG), F32)],
        scratch_shapes=[
            pltpu.VMEM((tm, D), BF16),
            pltpu.VMEM((n_conv, V7X_SUBLANES, tn), F32),
        ],
        compiler_params=_compiler_params(2, vmem),
        name="in_proj",
    )(x, g, w, w_gates, conv_w)


def _pool_mix(u_ref, buf, mixed_ref):
    s = pl.program_id(1)
    ts = u_ref.shape[0]
    G = u_ref.shape[1] // len(POOL_WINDOWS)

    @pl.when(s == 0)
    def _():
        buf[0:POOL_HALO, :] = jnp.zeros((POOL_HALO, buf.shape[1]), F32)

    buf[POOL_HALO:POOL_HALO + ts, :] = u_ref[...]
    pos = (s * ts + 1 + lax.broadcasted_iota(jnp.int32, (ts, 1), 0)).astype(F32)
    for i, w in enumerate(POOL_WINDOWS):
        cols = slice(i * G, (i + 1) * G)
        u = buf[POOL_HALO:POOL_HALO + ts, cols]
        acc = u
        for k in range(1, w):
            acc = acc + buf[POOL_HALO - k:POOL_HALO - k + ts, cols]
        mixed_ref[:, cols] = (acc / jnp.minimum(pos, float(w)) - u).astype(mixed_ref.dtype)
    buf[0:POOL_HALO, :] = buf[ts:ts + POOL_HALO, :]


def _pool_project(mixed_ref, wp_ref, sc_ref, o_ref):
    G = wp_ref.shape[1]
    for i in range(len(POOL_WINDOWS)):
        cols = slice(i * G, (i + 1) * G)
        o_ref[:, cols] = (_dot(mixed_ref[:, cols], wp_ref[i]) * sc_ref[:, cols]).astype(o_ref.dtype)


def _unit_lower_inverses(ls, eye, ii, jj):
    C = ls[0].shape[0]
    size = 2
    same = (ii // size) == (jj // size)
    xs = [eye - jnp.where(same, l_mat, 0.0) for l_mat in ls]
    while size < C:
        inner, same = same, (ii // (2 * size)) == (jj // (2 * size))
        offs = [jnp.where(same & jnp.logical_not(inner), l_mat, 0.0).astype(BF16) for l_mat in ls]
        xbs = [x.astype(BF16) for x in xs]
        ys = [_dot(off, xb) for off, xb in zip(offs, xbs)]
        xs = [x - _dot(xb, y.astype(BF16)) for x, xb, y in zip(xs, xbs, ys)]
        size *= 2
    return xs


def _dn_body(q_ref, k_ref, v_ref, z_ref, ba_ref, alog_ref, dtb_ref, ng_ref, pu_ref, wp_ref, psc_ref, o_ref, yp_ref,
             state, beta_rep, gc_rep, grow_s, cdec_s, u_s, w_s, qd_s, at_s, kdt_s, pool_buf, mixed_s):
    s = pl.program_id(1)
    ts, W = q_ref.shape
    H = W // DN_HEAD_DIM
    C = DN_CHUNK
    n_chunk = ts // C

    @pl.when(s == 0)
    def _():
        state[...] = jnp.zeros(state.shape, F32)

    _pool_mix(pu_ref, pool_buf, mixed_s)

    ba = ba_ref[...]
    beta = 1.0 / (1.0 + jnp.exp(-ba))
    xs = ba + dtb_ref[...]
    softplus = jnp.maximum(xs, 0.0) + jnp.log1p(jnp.exp(-jnp.abs(xs)))
    g = -jnp.exp(alog_ref[...]) * softplus
    row_in_chunk = lax.broadcasted_iota(jnp.int32, (ts, V7X_LANES), 0) % C
    gc = g
    sh = 1
    while sh < C:
        gc = gc + jnp.where(row_in_chunk >= sh, pltpu.roll(gc, sh, axis=0), 0.0)
        sh *= 2
    gct = gc.T
    lane = lax.broadcasted_iota(jnp.int32, (ts, V7X_LANES), 1)
    for h in range(H):
        beta_rep[h] = jnp.broadcast_to(
            jnp.sum(jnp.where(lane == h, beta, 0.0), axis=-1, keepdims=True), (ts, V7X_LANES))
        gc_rep[h] = jnp.broadcast_to(
            jnp.sum(jnp.where(lane == H + h, gc, 0.0), axis=-1, keepdims=True), (ts, V7X_LANES))
        grow_s[h] = gct[H + h:H + h + 1, :]

    ii = lax.broadcasted_iota(jnp.int32, (C, C), 0)
    jj = lax.broadcasted_iota(jnp.int32, (C, C), 1)
    eye = (ii == jj).astype(F32)

    def head_group_body(hg, carry):
        pairs = []
        for hh in range(DN_HEADS_PER_STEP):
            h = hg * DN_HEADS_PER_STEP + hh
            cols = pl.ds(pl.multiple_of(h * DN_HEAD_DIM, DN_HEAD_DIM), DN_HEAD_DIM)
            bcol, gcol, grow_all = beta_rep[h], gc_rep[h], grow_s[h]
            egc = jnp.exp(gcol)
            for c in range(n_chunk):
                r = slice(c * C, (c + 1) * C)
                pairs.append(dict(h=h, c=c, r=r, cols=cols, q=q_ref[r, cols], k=k_ref[r, cols], v=v_ref[r, cols],
                                  b=bcol[r], g=gcol[r], e=egc[r], grow=grow_all[:, r]))
        kbs = [p["k"].astype(BF16) for p in pairs]
        decs = [jnp.exp(jnp.minimum(p["g"] - p["grow"], 0.0)) for p in pairs]
        kks = [_dot_nt(kb, kb) for kb in kbs]
        qks = [_dot_nt(p["q"].astype(BF16), kb) for p, kb in zip(pairs, kbs)]
        ls = [jnp.where(ii > jj, kk * dec, 0.0) * p["b"] for kk, dec, p in zip(kks, decs, pairs)]
        for p, qk, dec in zip(pairs, qks, decs):
            r, cols = p["r"], p["cols"]
            at_s[r, cols] = jnp.where(ii >= jj, qk * dec, 0.0).astype(BF16)
            qd_s[r, cols] = (p["q"] * p["e"]).astype(BF16)
            kd = p["k"] * jnp.exp(p["g"][C - 1:C, :] - p["g"])
            kdt_s[r, cols] = kd.T.astype(BF16)
            cdec_s[p["c"], p["h"]] = p["e"][C - 1:C, :]
        t_invs = _unit_lower_inverses(ls, eye, ii, jj)
        rhs = [jnp.concatenate([p["v"] * p["b"], p["k"] * (p["b"] * p["e"])], axis=1) for p in pairs]
        sols = [_dot(t_inv.astype(BF16), b.astype(BF16)) for t_inv, b in zip(t_invs, rhs)]
        for p, sol in zip(pairs, sols):
            u_s[p["r"], p["cols"]] = sol[:, :DN_HEAD_DIM]
            w_s[p["r"], p["cols"]] = sol[:, DN_HEAD_DIM:].astype(BF16)
        return carry

    lax.fori_loop(0, H // DN_HEADS_PER_STEP, head_group_body, 0, unroll=True)

    def chunk_body(c, carry):
        r0 = pl.multiple_of(c * C, C)
        rows = pl.ds(r0, C)
        hs = range(H)
        cols = [slice(h * DN_HEAD_DIM, (h + 1) * DN_HEAD_DIM) for h in hs]
        sts = [state[h] for h in hs]
        sbs = [st.astype(BF16) for st in sts]
        wss = [_dot(w_s[rows, cols[h]], sbs[h]) for h in hs]
        qss = [_dot(qd_s[rows, cols[h]], sbs[h]) for h in hs]
        vbs = [(u_s[rows, cols[h]] - wss[h]).astype(BF16) for h in hs]
        avs = [_dot(at_s[rows, cols[h]], vbs[h]) for h in hs]
        kvs = [_dot(kdt_s[rows, cols[h]], vbs[h]) for h in hs]
        for h in hs:
            state[h] = sts[h] * cdec_s[c, h] + kvs[h]
            gated = _rms_rows(qss[h] + avs[h], ng_ref[...]) * _silu(z_ref[rows, cols[h]])
            o_ref[rows, cols[h]] = gated.astype(o_ref.dtype)
        return carry

    lax.fori_loop(0, n_chunk, chunk_body, 0, unroll=True)
    _pool_project(mixed_s, wp_ref, psc_ref, yp_ref)


def pool_and_delta_net(p, gates, w_pool, pool_scale, a_log, dt_bias, norm_g, l, B, S, ts, col_block0, H):
    n_win, G, _ = w_pool.shape[1:]
    PW = n_win * G
    W = H * DN_HEAD_DIM
    nt = S // ts
    NB = gates.shape[1]
    C = DN_CHUNK
    vmem = (2 * 4 * ts * W * 4 + 2 * ts * NB * 4 + 2 * ts * W * 2
            + H * DN_HEAD_DIM * DN_HEAD_DIM * 4 + 2 * H * ts * V7X_LANES * 4 + H * V7X_SUBLANES * ts * 4
            + ts * W * 4 + 4 * ts * W * 2 + 48 * C * C * 4 * DN_HEADS_PER_STEP
            + 2 * ts * PW * 4 + (ts + POOL_HALO) * PW * 4 + 2 * n_win * G * G * 2 + 2 * ts * PW * 2)

    def pblock(k):
        return pl.BlockSpec((ts, W), lambda b, s: (b * nt + s, col_block0 + k))

    y_dn, y_pool = pl.pallas_call(
        _dn_body,
        grid=(B, nt),
        in_specs=[
            pblock(0), pblock(1), pblock(2), pblock(3),
            pl.BlockSpec((ts, NB), lambda b, s: (b * nt + s, 0)),
            pl.BlockSpec((None, 1, NB), lambda b, s: (l, 0, 0)),
            pl.BlockSpec((None, 1, NB), lambda b, s: (l, 0, 0)),
            pl.BlockSpec((None, 1, DN_HEAD_DIM), lambda b, s: (l, 0, 0)),
            pl.BlockSpec((ts, PW), lambda b, s: (b * nt + s, 0)),
            pl.BlockSpec((None, n_win, G, G), lambda b, s: (l, 0, 0, 0)),
            pl.BlockSpec((None, 1, PW), lambda b, s: (l, 0, 0)),
        ],
        out_specs=[
            pl.BlockSpec((ts, W), lambda b, s: (b * nt + s, 0)),
            pl.BlockSpec((ts, PW), lambda b, s: (b * nt + s, 0)),
        ],
        out_shape=[jax.ShapeDtypeStruct((B * S, W), BF16), jax.ShapeDtypeStruct((B * S, PW), BF16)],
        scratch_shapes=[
            pltpu.VMEM((H, DN_HEAD_DIM, DN_HEAD_DIM), F32),
            pltpu.VMEM((H, ts, V7X_LANES), F32),
            pltpu.VMEM((H, ts, V7X_LANES), F32),
            pltpu.VMEM((H, 1, ts), F32),
            pltpu.VMEM((ts // C, H, 1, V7X_LANES), F32),
            pltpu.VMEM((ts, W), F32),
            pltpu.VMEM((ts, W), BF16),
            pltpu.VMEM((ts, W), BF16),
            pltpu.VMEM((ts, W), BF16),
            pltpu.VMEM((ts, W), BF16),
            pltpu.VMEM((ts + POOL_HALO, PW), F32),
            pltpu.VMEM((ts, PW), BF16),
        ],
        compiler_params=_compiler_params(2, vmem),
        name="pool_and_delta_net",
    )(p, p, p, p, gates, a_log, dt_bias, norm_g, p, w_pool, pool_scale)
    return y_pool, y_dn


def _mix_out_body(x_ref, yp_ref, yd_ref, w1_ref, w2_ref, o_ref):
    o_ref[...] = x_ref[...] + _dot(yp_ref[...], w1_ref[...]) + _dot(yd_ref[...], w2_ref[...])


def mix_out(x, y_pool, y_dn, w, l, tm, tn):
    R, D = x.shape
    K1, K2 = y_pool.shape[1], y_dn.shape[1]
    assert K1 == K2
    vmem = 2 * (2 * tm * tn * 4 + tm * (K1 + K2) * 2 + (K1 + K2) * tn * 2) + 2 * tm * tn * 4
    return pl.pallas_call(
        _mix_out_body,
        grid=(R // tm, D // tn),
        in_specs=[
            pl.BlockSpec((tm, tn), lambda i, j: (i, j)),
            pl.BlockSpec((tm, K1), lambda i, j: (i, 0)),
            pl.BlockSpec((tm, K2), lambda i, j: (i, 0)),
            pl.BlockSpec((None, K1, tn), lambda i, j: (l, 0, j)),
            pl.BlockSpec((None, K2, tn), lambda i, j: (l, 1, j)),
        ],
        out_specs=pl.BlockSpec((tm, tn), lambda i, j: (i, j)),
        out_shape=jax.ShapeDtypeStruct((R, D), F32),
        compiler_params=_compiler_params(2, vmem),
        name="mix_out",
    )(x, y_pool, y_dn, w, w)


def _xattn_body(x_ref, g_ref, wq_ref, k_ref, v_ref, wo_ref, o_ref, h_ref, a_ref):
    j = pl.program_id(1)
    dh = wq_ref.shape[1] // XA_HEADS_PER_STEP
    tn = wo_ref.shape[1]
    n_att = XA_HEADS // XA_HEADS_PER_STEP

    def attention_step(first):
        tm = h_ref.shape[0]
        sub = min(XA_SUB_ROWS, tm)
        col0 = pl.multiple_of(j * (XA_HEADS_PER_STEP * dh), XA_HEADS_PER_STEP * dh)
        units = [(hh, r0) for hh in range(XA_HEADS_PER_STEP) for r0 in range(0, tm, sub)]

        def scores(unit):
            hh, r0 = unit
            if first and hh == 0:
                h_ref[r0:r0 + sub, :] = _rms_rows(x_ref[r0:r0 + sub, :], g_ref[...]).astype(BF16)
            q = _dot(h_ref[r0:r0 + sub, :], wq_ref[:, hh * dh:(hh + 1) * dh])
            return _dot_nt(q.astype(BF16), k_ref[:, hh * dh:(hh + 1) * dh]) * (dh ** -0.5)

        sc = scores(units[0])
        for n, (hh, r0) in enumerate(units):
            sc_next = scores(units[n + 1]) if n + 1 < len(units) else None
            e = jnp.exp(sc - jnp.max(sc, axis=-1, keepdims=True))
            pr = e / jnp.sum(e, axis=-1, keepdims=True)
            o = _dot(pr.astype(BF16), v_ref[:, hh * dh:(hh + 1) * dh])
            a_ref[r0:r0 + sub, pl.ds(col0 + hh * dh, dh)] = o.astype(BF16)
            sc = sc_next

    pl.when(j == 0)(functools.partial(attention_step, True))
    pl.when((j > 0) & (j < n_att))(functools.partial(attention_step, False))

    @pl.when(j >= n_att)
    def _():
        cols = pl.ds(pl.multiple_of((j - n_att) * tn, tn), tn)
        o_ref[...] = x_ref[:, cols] + _dot(a_ref[...], wo_ref[...])


def cross_attention(x, g, w_xq, kv, w_xo, l, B, S, M, tm, tn):
    R, D = x.shape
    n_att = XA_HEADS // XA_HEADS_PER_STEP
    wh = D // n_att
    tiles_per_batch = S // tm
    att = lambda j: jnp.minimum(j, n_att - 1)
    otile = lambda j: jnp.maximum(j - n_att, 0)
    vmem = (2 * tm * D * 4 + 2 * tm * D * 2 + 4 * D * wh * 2 + 4 * M * wh * 2 + 4 * D * tn * 2 + 3 * tm * tn * 4
            + 3 * tm * wh * 4 + 3 * tm * M * 4)
    return pl.pallas_call(
        _xattn_body,
        grid=(R // tm, n_att + D // tn),
        in_specs=[
            pl.BlockSpec((tm, D), lambda i, j: (i, 0)),
            pl.BlockSpec((None, 1, D), lambda i, j: (l, 0, 0)),
            pl.BlockSpec((None, D, wh), lambda i, j: (l, 0, att(j))),
            pl.BlockSpec((None, M, wh), lambda i, j: (l, i // tiles_per_batch, att(j))),
            pl.BlockSpec((None, M, wh), lambda i, j: (l, i // tiles_per_batch, n_att + att(j))),
            pl.BlockSpec((None, D, tn), lambda i, j: (l, 0, otile(j))),
        ],
        out_specs=pl.BlockSpec((tm, tn), lambda i, j: (i, otile(j))),
        out_shape=jax.ShapeDtypeStruct((R, D), F32),
        scratch_shapes=[pltpu.VMEM((tm, D), BF16), pltpu.VMEM((tm, D), BF16)],
        compiler_params=_compiler_params(2, vmem),
        name="cross_attention",
    )(x, g, w_xq, kv, kv, w_xo)


def _ffn_body(x_ref, g_ref, wg_ref, wu_ref, cw_ref, cb_ref, wd_ref, og_ref, o_ref, h_ref, gbuf, carry, *,
              tiles_per_seq, norm_output):
    i = pl.program_id(0)
    j = pl.program_id(1)
    tm = x_ref.shape[0]
    K = cw_ref.shape[0]
    halo = V7X_SUBLANES

    def column_step(first):
        gbuf[0:halo, :] = jnp.where((i % tiles_per_seq) == 0, 0.0, carry[j])
        sub = min(FFN_FIRST_SUB_ROWS if first else FFN_SUB_ROWS, tm)

        def gate_up(r0):
            rows = slice(r0, r0 + sub)
            if first:
                x = x_ref[rows, :]
                h_ref[rows, :] = _rms_rows(x, g_ref[...]).astype(BF16)
                o_ref[rows, :] = x
            h = h_ref[rows, :]
            return _dot(h, wg_ref[...]), _dot(h, wu_ref[...])

        gate, up = gate_up(0)
        for r0 in range(0, tm, sub):
            nxt = gate_up(r0 + sub) if r0 + sub < tm else None
            gbuf[halo + r0:halo + r0 + sub, :] = gate
            conv = gate * cw_ref[K - 1:K, :] + cb_ref[...]
            for t in range(K - 2, -1, -1):
                conv = conv + gbuf[halo - K + 1 + t + r0:halo - K + 1 + t + r0 + sub, :] * cw_ref[t:t + 1, :]
            act = _silu(conv) * up
            o_ref[r0:r0 + sub, :] += _dot(act.astype(BF16), wd_ref[...])
            if nxt is not None:
                gate, up = nxt
        carry[j] = gbuf[tm:tm + halo, :]

    pl.when(j == 0)(functools.partial(column_step, True))
    pl.when(j > 0)(functools.partial(column_step, False))

    if norm_output:
        @pl.when(j == pl.num_programs(1) - 1)
        def _():
            _norm_rows_into(o_ref, o_ref, og_ref)


def conv_glu_ffn(x, g, w_gate, w_up, conv_w, conv_b, w_down, out_g, l, S, tm, tf, norm_output):
    R, D = x.shape
    F = w_gate.shape[2]
    K = conv_w.shape[1]
    nf = F // tf
    vmem = (4 * tm * D * 4 + tm * D * 2 + 2 * 3 * D * tf * 2 + (tm + V7X_SUBLANES) * tf * 4
            + nf * V7X_SUBLANES * tf * 4 + 4 * tm * tf * 4)
    return pl.pallas_call(
        functools.partial(_ffn_body, tiles_per_seq=S // tm, norm_output=norm_output),
        grid=(R // tm, nf),
        in_specs=[
            pl.BlockSpec((tm, D), lambda i, j: (i, 0)),
            pl.BlockSpec((None, 1, D), lambda i, j: (l, 0, 0)),
            pl.BlockSpec((None, D, tf), lambda i, j: (l, 0, j)),
            pl.BlockSpec((None, D, tf), lambda i, j: (l, 0, j)),
            pl.BlockSpec((None, K, tf), lambda i, j: (l, 0, j)),
            pl.BlockSpec((None, 1, tf), lambda i, j: (l, 0, j)),
            pl.BlockSpec((None, tf, D), lambda i, j: (l, j, 0)),
            pl.BlockSpec((1, D), lambda i, j: (0, 0)),
        ],
        out_specs=pl.BlockSpec((tm, D), lambda i, j: (i, 0)),
        out_shape=jax.ShapeDtypeStruct((R, D), F32),
        scratch_shapes=[
            pltpu.VMEM((tm, D), BF16),
            pltpu.VMEM((tm + V7X_SUBLANES, tf), F32),
            pltpu.VMEM((nf, V7X_SUBLANES, tf), F32),
        ],
        compiler_params=_compiler_params(2, vmem),
        name="conv_glu_ffn",
    )(x, g, w_gate, w_up, conv_w, conv_b, w_down, out_g)


def _tiles(S):
    return dict(tm=min(1024, S), ts_mix=min(512, S))


def kernel(x, mem, mix_norm_g, w_in, w_pool, pool_scale, dn_conv_w, dn_a_log, dn_dt_bias, dn_norm_g, w_mix_out,
           xa_norm_g, mem_norm_g, w_xq, w_xkv, w_xo, ffn_norm_g, w_gate, w_up, ffn_conv_w, ffn_conv_b, w_down,
           final_norm_g):
    B, S, D = x.shape
    M = mem.shape[1]
    depth = w_in.shape[0]
    H = dn_a_log.shape[1]
    dn_w = H * DN_HEAD_DIM
    pool_w = w_pool.shape[1] * w_pool.shape[2]
    main = pool_w + 4 * dn_w
    assert pool_w == dn_w and w_in.shape[2] == main + 2 * H and 2 * H <= V7X_LANES
    t = _tiles(S)

    tn = 512
    w_in_b = w_in[:, :, :main].astype(BF16)
    w_gates_b = jnp.pad(w_in[:, :, main:].astype(BF16), ((0, 0), (0, 0), (0, V7X_LANES - 2 * H)))
    pad_gate = lambda a: jnp.pad(a, ((0, 0), (H, V7X_LANES - 2 * H)))[:, None, :]
    a_log_p, dt_bias_p = pad_gate(dn_a_log), pad_gate(dn_dt_bias)
    w_pool_b = w_pool.astype(BF16)
    w_mix_b = w_mix_out.astype(BF16)
    w_xq_b, w_xkv_b, w_xo_b = w_xq.astype(BF16), w_xkv.astype(BF16), w_xo.astype(BF16)
    w_gate_b, w_up_b, w_down_b = w_gate.astype(BF16), w_up.astype(BF16), w_down.astype(BF16)
    row = lambda a: a[:, None, :]

    xr = x.reshape(B * S, D)
    kv = norm_matmul(mem.reshape(B * M, D), mem_norm_g[None, :], w_xkv_b, BF16, tm=min(1024, B * M), tn=tn)
    for l in range(depth):
        p, gates = in_proj(xr, mix_norm_g[l][None, :], w_in_b, w_gates_b, dn_conv_w, l, S, tm=t["tm"], tn=tn,
                           conv_col0=pool_w)
        y_pool, y_dn = pool_and_delta_net(p, gates, w_pool_b, row(pool_scale), a_log_p, dt_bias_p, row(dn_norm_g),
                                          l, B, S, t["ts_mix"], col_block0=pool_w // dn_w, H=H)
        xr = mix_out(xr, y_pool, y_dn, w_mix_b, l, tm=t["tm"] // 2, tn=D)
        xr = cross_attention(xr, row(xa_norm_g), w_xq_b, kv, w_xo_b, l, B, S, M, tm=t["tm"], tn=2 * tn)
        xr = conv_glu_ffn(xr, row(ffn_norm_g), w_gate_b, w_up_b, ffn_conv_w, row(ffn_conv_b), w_down_b,
                          final_norm_g[None, :], l, S, tm=t["tm"], tf=tn, norm_output=(l == depth - 1))
    return xr.reshape(B, S, D)
```

```python
import functools

import jax
import jax.numpy as jnp
from jax import lax
from jax.experimental import pallas as pl
from jax.experimental.pallas import tpu as pltpu

F32 = jnp.float32
BF16 = jnp.bfloat16

EPS = 1e-6
POOL_WINDOWS = (2, 4, 8, 16)
POOL_HALO = 16
DN_HEAD_DIM = 128
DN_CHUNK = 128
DN_HEADS_PER_STEP = 4
XA_HEADS = 4
XA_HEADS_PER_STEP = 2

V7X_VMEM_BYTES = 64 * 1024 * 1024
V7X_LANES = 128
V7X_SUBLANES = 8
NORM_ROWS = 128
XA_SUB_ROWS = 1024
FFN_SUB_ROWS = 512
FFN_FIRST_SUB_ROWS = 256
NORM_SUB_ROWS = 256
IN_PROJ_SUB_ROWS = 128


def _compiler_params(n_axes, vmem_bytes):
    limit = min(int(vmem_bytes * 1.25) + (4 << 20), V7X_VMEM_BYTES * 7 // 8)
    return pltpu.CompilerParams(dimension_semantics=("arbitrary",) * n_axes, vmem_limit_bytes=limit)


def _dot(a, b):
    return jnp.dot(a, b, preferred_element_type=F32)


def _dot_nt(a, b):
    return lax.dot_general(a, b, (((1,), (1,)), ((), ())), preferred_element_type=F32)


def _silu(x):
    return x * (1.0 / (1.0 + jnp.exp(-x)))


def _rms_rows(x, g):
    ms = jnp.mean(x * x, axis=-1, keepdims=True)
    return x * lax.rsqrt(ms + EPS) * g


def _rows_shifted_down(a, prev, s):
    n = prev.shape[0]
    rows = lax.broadcasted_iota(jnp.int32, prev.shape, 0)
    rolled = [pltpu.roll(piece, s, axis=0) for piece in [prev] + [a[r:r + n, :] for r in range(0, a.shape[0], n)]]
    return jnp.concatenate([jnp.where(rows < s, lo, hi) for lo, hi in zip(rolled[:-1], rolled[1:])], axis=0)


def _norm_rows_into(h_ref, x_ref, g_ref):
    g = g_ref[...]

    def body(r, c):
        rows = pl.ds(pl.multiple_of(r * NORM_ROWS, NORM_ROWS), NORM_ROWS)
        h_ref[rows, :] = _rms_rows(x_ref[rows, :], g).astype(h_ref.dtype)
        return c

    lax.fori_loop(0, x_ref.shape[0] // NORM_ROWS, body, 0)


def _norm_matmul_body(x_ref, g_ref, w_ref, o_ref, h_ref):
    @pl.when((pl.program_id(1) == 0) & (pl.program_id(2) == 0))
    def _():
        _norm_rows_into(h_ref, x_ref, g_ref)

    o_ref[...] = _dot(h_ref[...], w_ref[...]).astype(o_ref.dtype)


def norm_matmul(x, g, w, out_dtype, tm, tn):
    R, K = x.shape
    L, _, N = w.shape
    osz = jnp.dtype(out_dtype).itemsize
    vmem = 2 * tm * K * 4 + tm * K * 2 + 2 * K * tn * 2 + 2 * tm * tn * osz + tm * tn * 4
    return pl.pallas_call(
        _norm_matmul_body,
        grid=(R // tm, L, N // tn),
        in_specs=[
            pl.BlockSpec((tm, K), lambda i, l, j: (i, 0)),
            pl.BlockSpec((1, K), lambda i, l, j: (0, 0)),
            pl.BlockSpec((None, K, tn), lambda i, l, j: (l, 0, j)),
        ],
        out_specs=pl.BlockSpec((None, tm, tn), lambda i, l, j: (l, i, j)),
        out_shape=jax.ShapeDtypeStruct((L, R, N), out_dtype),
        scratch_shapes=[pltpu.VMEM((tm, K), BF16)],
        compiler_params=_compiler_params(3, vmem),
        name="norm_matmul",
    )(x, g, w)


def _in_proj_body(x_ref, g_ref, w_ref, wg_ref, cw_ref, p_ref, gate_ref, h_ref, carry, *,
                  n_main, tiles_per_seq, conv0, tiles_per_part):
    i = pl.program_id(0)
    j = pl.program_id(1)
    tm, tn = p_ref.shape
    K = cw_ref.shape[0]
    halo = V7X_SUBLANES

    is_conv = (j >= conv0) & (j < conv0 + 3 * tiles_per_part)

    @pl.when(j == 0)
    def _():
        sub = min(NORM_SUB_ROWS, tm)
        for r0 in range(0, tm, sub):
            h = _rms_rows(x_ref[r0:r0 + sub, :], g_ref[...]).astype(BF16)
            h_ref[r0:r0 + sub, :] = h
            p_ref[r0:r0 + sub, :] = _dot(h, w_ref[...])

    @pl.when(j == n_main)
    def _():
        gate_ref[...] = _dot(h_ref[...], wg_ref[...])

    @pl.when(jnp.logical_not(is_conv) & (j > 0) & (j < n_main))
    def _():
        p_ref[...] = _dot(h_ref[...], w_ref[...])

    @pl.when(is_conv)
    def _():
        cj = j - conv0
        part = cj // tiles_per_part
        q_scale = jnp.where(part == 0, DN_HEAD_DIM ** -0.5, 1.0)
        prev = jnp.where((i % tiles_per_seq) == 0, 0.0, carry[cj])
        sub = min(IN_PROJ_SUB_ROWS, tm)
        for r0 in range(0, tm, sub):
            acc = _dot(h_ref[r0:r0 + sub, :], w_ref[...])
            y = acc * cw_ref[K - 1:K, :]
            for s in range(1, K):
                y = y + _rows_shifted_down(acc, prev, s) * cw_ref[K - 1 - s:K - s, :]
            prev = acc[sub - halo:sub, :]
            y = _silu(y)
            for c in range(0, tn, DN_HEAD_DIM):
                yh = y[:, c:c + DN_HEAD_DIM]
                inv = lax.rsqrt(jnp.sum(yh * yh, axis=-1, keepdims=True) + EPS) * q_scale
                p_ref[r0:r0 + sub, c:c + DN_HEAD_DIM] = yh * jnp.where(part == 2, 1.0, inv)
        carry[cj] = prev


def in_proj(x, g, w, w_gates, conv_w, l, S, tm, tn, conv_col0):
    R, D = x.shape
    N = w.shape[2]
    NG = w_gates.shape[2]
    K, conv_cols = conv_w.shape[1:]
    assert N % tn == 0 and conv_col0 % tn == 0 and (conv_cols // 3) % tn == 0 and tn % DN_HEAD_DIM == 0
    n_main, conv0, n_conv = N // tn, conv_col0 // tn, conv_cols // tn
    assert conv0 >= 1
    main = lambda j: jnp.minimum(j, n_main - 1)
    vmem = (2 * tm * D * 4 + tm * D * 2 + 2 * D * tn * 2 + 2 * D * NG * 2 + 3 * tm * tn * 4 + 2 * tm * NG * 4
            + n_conv * V7X_SUBLANES * tn * 4 + 8 * IN_PROJ_SUB_ROWS * tn * 4)
    return pl.pallas_call(
        functools.partial(_in_proj_body, n_main=n_main, tiles_per_seq=S // tm, conv0=conv0,
                          tiles_per_part=n_conv // 3),
        grid=(R // tm, n_main + 1),
        in_specs=[
            pl.BlockSpec((tm, D), lambda i, j: (i, 0)),
            pl.BlockSpec((1, D), lambda i, j: (0, 0)),
            pl.BlockSpec((None, D, tn), lambda i, j: (l, 0, main(j))),
            pl.BlockSpec((None, D, NG), lambda i, j: (l, 0, 0)),
            pl.BlockSpec((None, K, tn), lambda i, j: (l, 0, jnp.clip(j - conv0, 0, n_conv - 1))),
        ],
        out_specs=[
            pl.BlockSpec((tm, tn), lambda i, j: (i, main(j))),
            pl.BlockSpec((tm, NG), lambda i, j: (i, 0)),
        ],
        out_shape=[jax.ShapeDtypeStruct((R, N), F32), jax.ShapeDtypeStruct((R, NG), F32)],
        scratch_shapes=[
            pltpu.VMEM((tm, D), BF16),
            pltpu.VMEM((n_conv, V7X_SUBLANES, tn), F32),
        ],
        compiler_params=_compiler_params(2, vmem),
        name="in_proj",
    )(x, g, w, w_gates, conv_w)


def _pool_mix(u_ref, buf, mixed_ref):
    s = pl.program_id(1)
    ts = u_ref.shape[0]
    G = u_ref.shape[1] // len(POOL_WINDOWS)

    @pl.when(s == 0)
    def _():
        buf[0:POOL_HALO, :] = jnp.zeros((POOL_HALO, buf.shape[1]), F32)

    buf[POOL_HALO:POOL_HALO + ts, :] = u_ref[...]
    pos = (s * ts + 1 + lax.broadcasted_iota(jnp.int32, (ts, 1), 0)).astype(F32)
    for i, w in enumerate(POOL_WINDOWS):
        cols = slice(i * G, (i + 1) * G)
        u = buf[POOL_HALO:POOL_HALO + ts, cols]
        acc = u
        for k in range(1, w):
            acc = acc + buf[POOL_HALO - k:POOL_HALO - k + ts, cols]
        mixed_ref[:, cols] = (acc / jnp.minimum(pos, float(w)) - u).astype(mixed_ref.dtype)
    buf[0:POOL_HALO, :] = buf[ts:ts + POOL_HALO, :]


def _pool_project(mixed_ref, wp_ref, sc_ref, o_ref):
    G = wp_ref.shape[1]
    for i in range(len(POOL_WINDOWS)):
        cols = slice(i * G, (i + 1) * G)
        o_ref[:, cols] = (_dot(mixed_ref[:, cols], wp_ref[i]) * sc_ref[:, cols]).astype(o_ref.dtype)


def _unit_lower_inverses(ls, eye, ii, jj):
    C = ls[0].shape[0]
    size = 2
    same = (ii // size) == (jj // size)
    xs = [eye - jnp.where(same, l_mat, 0.0) for l_mat in ls]
    while size < C:
        inner, same = same, (ii // (2 * size)) == (jj // (2 * size))
        offs = [jnp.where(same & jnp.logical_not(inner), l_mat, 0.0).astype(BF16) for l_mat in ls]
        xbs = [x.astype(BF16) for x in xs]
        ys = [_dot(off, xb) for off, xb in zip(offs, xbs)]
        xs = [x - _dot(xb, y.astype(BF16)) for x, xb, y in zip(xs, xbs, ys)]
        size *= 2
    return xs


def _dn_body(q_ref, k_ref, v_ref, z_ref, ba_ref, alog_ref, dtb_ref, ng_ref, pu_ref, wp_ref, psc_ref, o_ref, yp_ref,
             state, beta_rep, gc_rep, grow_s, cdec_s, u_s, w_s, qd_s, at_s, kdt_s, pool_buf, mixed_s):
    s = pl.program_id(1)
    ts, W = q_ref.shape
    H = W // DN_HEAD_DIM
    C = DN_CHUNK
    n_chunk = ts // C

    @pl.when(s == 0)
    def _():
        state[...] = jnp.zeros(state.shape, F32)

    _pool_mix(pu_ref, pool_buf, mixed_s)

    ba = ba_ref[...]
    beta = 1.0 / (1.0 + jnp.exp(-ba))
    xs = ba + dtb_ref[...]
    softplus = jnp.maximum(xs, 0.0) + jnp.log1p(jnp.exp(-jnp.abs(xs)))
    g = -jnp.exp(alog_ref[...]) * softplus
    row_in_chunk = lax.broadcasted_iota(jnp.int32, (ts, V7X_LANES), 0) % C
    gc = g
    sh = 1
    while sh < C:
        gc = gc + jnp.where(row_in_chunk >= sh, pltpu.roll(gc, sh, axis=0), 0.0)
        sh *= 2
    gct = gc.T
    lane = lax.broadcasted_iota(jnp.int32, (ts, V7X_LANES), 1)
    for h in range(H):
        beta_rep[h] = jnp.broadcast_to(
            jnp.sum(jnp.where(lane == h, beta, 0.0), axis=-1, keepdims=True), (ts, V7X_LANES))
        gc_rep[h] = jnp.broadcast_to(
            jnp.sum(jnp.where(lane == H + h, gc, 0.0), axis=-1, keepdims=True), (ts, V7X_LANES))
        grow_s[h] = gct[H + h:H + h + 1, :]

    ii = lax.broadcasted_iota(jnp.int32, (C, C), 0)
    jj = lax.broadcasted_iota(jnp.int32, (C, C), 1)
    eye = (ii == jj).astype(F32)

    def head_group_body(hg, carry):
        pairs = []
        for hh in range(DN_HEADS_PER_STEP):
            h = hg * DN_HEADS_PER_STEP + hh
            cols = pl.ds(pl.multiple_of(h * DN_HEAD_DIM, DN_HEAD_DIM), DN_HEAD_DIM)
            bcol, gcol, grow_all = beta_rep[h], gc_rep[h], grow_s[h]
            egc = jnp.exp(gcol)
            for c in range(n_chunk):
                r = slice(c * C, (c + 1) * C)
                pairs.append(dict(h=h, c=c, r=r, cols=cols, q=q_ref[r, cols], k=k_ref[r, cols], v=v_ref[r, cols],
                                  b=bcol[r], g=gcol[r], e=egc[r], grow=grow_all[:, r]))
        kbs = [p["k"].astype(BF16) for p in pairs]
        decs = [jnp.exp(jnp.minimum(p["g"] - p["grow"], 0.0)) for p in pairs]
        kks = [_dot_nt(kb, kb) for kb in kbs]
        qks = [_dot_nt(p["q"].astype(BF16), kb) for p, kb in zip(pairs, kbs)]
        ls = [jnp.where(ii > jj, kk * dec, 0.0) * p["b"] for kk, dec, p in zip(kks, decs, pairs)]
        for p, qk, dec in zip(pairs, qks, decs):
            r, cols = p["r"], p["cols"]
            at_s[r, cols] = jnp.where(ii >= jj, qk * dec, 0.0).astype(BF16)
            qd_s[r, cols] = (p["q"] * p["e"]).astype(BF16)
            kd = p["k"] * jnp.exp(p["g"][C - 1:C, :] - p["g"])
            kdt_s[r, cols] = kd.T.astype(BF16)
            cdec_s[p["c"], p["h"]] = p["e"][C - 1:C, :]
        t_invs = _unit_lower_inverses(ls, eye, ii, jj)
        rhs = [jnp.concatenate([p["v"] * p["b"], p["k"] * (p["b"] * p["e"])], axis=1) for p in pairs]
        sols = [_dot(t_inv.astype(BF16), b.astype(BF16)) for t_inv, b in zip(t_invs, rhs)]
        for p, sol in zip(pairs, sols):
            u_s[p["r"], p["cols"]] = sol[:, :DN_HEAD_DIM]
            w_s[p["r"], p["cols"]] = sol[:, DN_HEAD_DIM:].astype(BF16)
        return carry

    lax.fori_loop(0, H // DN_HEADS_PER_STEP, head_group_body, 0, unroll=True)

    def chunk_body(c, carry):
        r0 = pl.multiple_of(c * C, C)
        rows = pl.ds(r0, C)
        hs = range(H)
        cols = [slice(h * DN_HEAD_DIM, (h + 1) * DN_HEAD_DIM) for h in hs]
        sts = [state[h] for h in hs]
        sbs = [st.astype(BF16) for st in sts]
        wss = [_dot(w_s[rows, cols[h]], sbs[h]) for h in hs]
        qss = [_dot(qd_s[rows, cols[h]], sbs[h]) for h in hs]
        vbs = [(u_s[rows, cols[h]] - wss[h]).astype(BF16) for h in hs]
        avs = [_dot(at_s[rows, cols[h]], vbs[h]) for h in hs]
        kvs = [_dot(kdt_s[rows, cols[h]], vbs[h]) for h in hs]
        for h in hs:
            state[h] = sts[h] * cdec_s[c, h] + kvs[h]
            gated = _rms_rows(qss[h] + avs[h], ng_ref[...]) * _silu(z_ref[rows, cols[h]])
            o_ref[rows, cols[h]] = gated.astype(o_ref.dtype)
        return carry

    lax.fori_loop(0, n_chunk, chunk_body, 0, unroll=True)
    _pool_project(mixed_s, wp_ref, psc_ref, yp_ref)


def pool_and_delta_net(p, gates, w_pool, pool_scale, a_log, dt_bias, norm_g, l, B, S, ts, col_block0, H):
    n_win, G, _ = w_pool.shape[1:]
    PW = n_win * G
    W = H * DN_HEAD_DIM
    nt = S // ts
    NB = gates.shape[1]
    C = DN_CHUNK
    vmem = (2 * 4 * ts * W * 4 + 2 * ts * NB * 4 + 2 * ts * W * 2
            + H * DN_HEAD_DIM * DN_HEAD_DIM * 4 + 2 * H * ts * V7X_LANES * 4 + H * V7X_SUBLANES * ts * 4
            + ts * W * 4 + 4 * ts * W * 2 + 48 * C * C * 4 * DN_HEADS_PER_STEP
            + 2 * ts * PW * 4 + (ts + POOL_HALO) * PW * 4 + 2 * n_win * G * G * 2 + 2 * ts * PW * 2)

    def pblock(k):
        return pl.BlockSpec((ts, W), lambda b, s: (b * nt + s, col_block0 + k))

    y_dn, y_pool = pl.pallas_call(
        _dn_body,
        grid=(B, nt),
        in_specs=[
            pblock(0), pblock(1), pblock(2), pblock(3),
            pl.BlockSpec((ts, NB), lambda b, s: (b * nt + s, 0)),
            pl.BlockSpec((None, 1, NB), lambda b, s: (l, 0, 0)),
            pl.BlockSpec((None, 1, NB), lambda b, s: (l, 0, 0)),
            pl.BlockSpec((None, 1, DN_HEAD_DIM), lambda b, s: (l, 0, 0)),
            pl.BlockSpec((ts, PW), lambda b, s: (b * nt + s, 0)),
            pl.BlockSpec((None, n_win, G, G), lambda b, s: (l, 0, 0, 0)),
            pl.BlockSpec((None, 1, PW), lambda b, s: (l, 0, 0)),
        ],
        out_specs=[
            pl.BlockSpec((ts, W), lambda b, s: (b * nt + s, 0)),
            pl.BlockSpec((ts, PW), lambda b, s: (b * nt + s, 0)),
        ],
        out_shape=[jax.ShapeDtypeStruct((B * S, W), BF16), jax.ShapeDtypeStruct((B * S, PW), BF16)],
        scratch_shapes=[
            pltpu.VMEM((H, DN_HEAD_DIM, DN_HEAD_DIM), F32),
            pltpu.VMEM((H, ts, V7X_LANES), F32),
            pltpu.VMEM((H, ts, V7X_LANES), F32),
            pltpu.VMEM((H, 1, ts), F32),
            pltpu.VMEM((ts // C, H, 1, V7X_LANES), F32),
            pltpu.VMEM((ts, W), F32),
            pltpu.VMEM((ts, W), BF16),
            pltpu.VMEM((ts, W), BF16),
            pltpu.VMEM((ts, W), BF16),
            pltpu.VMEM((ts, W), BF16),
            pltpu.VMEM((ts + POOL_HALO, PW), F32),
            pltpu.VMEM((ts, PW), BF16),
        ],
        compiler_params=_compiler_params(2, vmem),
        name="pool_and_delta_net",
    )(p, p, p, p, gates, a_log, dt_bias, norm_g, p, w_pool, pool_scale)
    return y_pool, y_dn


def _mix_out_body(x_ref, yp_ref, yd_ref, w1_ref, w2_ref, o_ref):
    o_ref[...] = x_ref[...] + _dot(yp_ref[...], w1_ref[...]) + _dot(yd_ref[...], w2_ref[...])


def mix_out(x, y_pool, y_dn, w, l, tm, tn):
    R, D = x.shape
    K1, K2 = y_pool.shape[1], y_dn.shape[1]
    assert K1 == K2
    vmem = 2 * (2 * tm * tn * 4 + tm * (K1 + K2) * 2 + (K1 + K2) * tn * 2) + 2 * tm * tn * 4
    return pl.pallas_call(
        _mix_out_body,
        grid=(R // tm, D // tn),
        in_specs=[
            pl.BlockSpec((tm, tn), lambda i, j: (i, j)),
            pl.BlockSpec((tm, K1), lambda i, j: (i, 0)),
            pl.BlockSpec((tm, K2), lambda i, j: (i, 0)),
            pl.BlockSpec((None, K1, tn), lambda i, j: (l, 0, j)),
            pl.BlockSpec((None, K2, tn), lambda i, j: (l, 1, j)),
        ],
        out_specs=pl.BlockSpec((tm, tn), lambda i, j: (i, j)),
        out_shape=jax.ShapeDtypeStruct((R, D), F32),
        compiler_params=_compiler_params(2, vmem),
        name="mix_out",
    )(x, y_pool, y_dn, w, w)


def _xattn_body(x_ref, g_ref, wq_ref, k_ref, v_ref, wo_ref, o_ref, h_ref, a_ref):
    j = pl.program_id(1)
    dh = wq_ref.shape[1] // XA_HEADS_PER_STEP
    tn = wo_ref.shape[1]
    n_att = XA_HEADS // XA_HEADS_PER_STEP

    def attention_step(first):
        tm = h_ref.shape[0]
        sub = min(XA_SUB_ROWS, tm)
        col0 = pl.multiple_of(j * (XA_HEADS_PER_STEP * dh), XA_HEADS_PER_STEP * dh)
        units = [(hh, r0) for hh in range(XA_HEADS_PER_STEP) for r0 in range(0, tm, sub)]

        def scores(unit):
            hh, r0 = unit
            if first and hh == 0:
                h_ref[r0:r0 + sub, :] = _rms_rows(x_ref[r0:r0 + sub, :], g_ref[...]).astype(BF16)
            q = _dot(h_ref[r0:r0 + sub, :], wq_ref[:, hh * dh:(hh + 1) * dh])
            return _dot_nt(q.astype(BF16), k_ref[:, hh * dh:(hh + 1) * dh]) * (dh ** -0.5)

        sc = scores(units[0])
        for n, (hh, r0) in enumerate(units):
            sc_next = scores(units[n + 1]) if n + 1 < len(units) else None
            e = jnp.exp(sc - jnp.max(sc, axis=-1, keepdims=True))
            pr = e / jnp.sum(e, axis=-1, keepdims=True)
            o = _dot(pr.astype(BF16), v_ref[:, hh * dh:(hh + 1) * dh])
            a_ref[r0:r0 + sub, pl.ds(col0 + hh * dh, dh)] = o.astype(BF16)
            sc = sc_next

    pl.when(j == 0)(functools.partial(attention_step, True))
    pl.when((j > 0) & (j < n_att))(functools.partial(attention_step, False))

    @pl.when(j >= n_att)
    def _():
        cols = pl.ds(pl.multiple_of((j - n_att) * tn, tn), tn)
        o_ref[...] = x_ref[:, cols] + _dot(a_ref[...], wo_ref[...])


def cross_attention(x, g, w_xq, kv, w_xo, l, B, S, M, tm, tn):
    R, D = x.shape
    n_att = XA_HEADS // XA_HEADS_PER_STEP
    wh = D // n_att
    tiles_per_batch = S // tm
    att = lambda j: jnp.minimum(j, n_att - 1)
    otile = lambda j: jnp.maximum(j - n_att, 0)
    vmem = (2 * tm * D * 4 + 2 * tm * D * 2 + 4 * D * wh * 2 + 4 * M * wh * 2 + 4 * D * tn * 2 + 3 * tm * tn * 4
            + 3 * tm * wh * 4 + 3 * tm * M * 4)
    return pl.pallas_call(
        _xattn_body,
        grid=(R // tm, n_att + D // tn),
        in_specs=[
            pl.BlockSpec((tm, D), lambda i, j: (i, 0)),
            pl.BlockSpec((None, 1, D), lambda i, j: (l, 0, 0)),
            pl.BlockSpec((None, D, wh), lambda i, j: (l, 0, att(j))),
            pl.BlockSpec((None, M, wh), lambda i, j: (l, i // tiles_per_batch, att(j))),
            pl.BlockSpec((None, M, wh), lambda i, j: (l, i // tiles_per_batch, n_att + att(j))),
            pl.BlockSpec((None, D, tn), lambda i, j: (l, 0, otile(j))),
        ],
        out_specs=pl.BlockSpec((tm, tn), lambda i, j: (i, otile(j))),
        out_shape=jax.ShapeDtypeStruct((R, D), F32),
        scratch_shapes=[pltpu.VMEM((tm, D), BF16), pltpu.VMEM((tm, D), BF16)],
        compiler_params=_compiler_params(2, vmem),
        name="cross_attention",
    )(x, g, w_xq, kv, kv, w_xo)


def _ffn_body(x_ref, g_ref, wg_ref, wu_ref, cw_ref, cb_ref, wd_ref, og_ref, o_ref, h_ref, gbuf, carry, *,
              tiles_per_seq, norm_output):
    i = pl.program_id(0)
    j = pl.program_id(1)
    tm = x_ref.shape[0]
    K = cw_ref.shape[0]
    halo = V7X_SUBLANES

    def column_step(first):
        gbuf[0:halo, :] = jnp.where((i % tiles_per_seq) == 0, 0.0, carry[j])
        sub = min(FFN_FIRST_SUB_ROWS if first else FFN_SUB_ROWS, tm)

        def gate_up(r0):
            rows = slice(r0, r0 + sub)
            if first:
                x = x_ref[rows, :]
                h_ref[rows, :] = _rms_rows(x, g_ref[...]).astype(BF16)
                o_ref[rows, :] = x
            h = h_ref[rows, :]
            return _dot(h, wg_ref[...]), _dot(h, wu_ref[...])

        gate, up = gate_up(0)
        for r0 in range(0, tm, sub):
            nxt = gate_up(r0 + sub) if r0 + sub < tm else None
            gbuf[halo + r0:halo + r0 + sub, :] = gate
            conv = gate * cw_ref[K - 1:K, :] + cb_ref[...]
            for t in range(K - 2, -1, -1):
                conv = conv + gbuf[halo - K + 1 + t + r0:halo - K + 1 + t + r0 + sub, :] * cw_ref[t:t + 1, :]
            act = _silu(conv) * up
            o_ref[r0:r0 + sub, :] += _dot(act.astype(BF16), wd_ref[...])
            if nxt is not None:
                gate, up = nxt
        carry[j] = gbuf[tm:tm + halo, :]

    pl.when(j == 0)(functools.partial(column_step, True))
    pl.when(j > 0)(functools.partial(column_step, False))

    if norm_output:
        @pl.when(j == pl.num_programs(1) - 1)
        def _():
            _norm_rows_into(o_ref, o_ref, og_ref)


def conv_glu_ffn(x, g, w_gate, w_up, conv_w, conv_b, w_down, out_g, l, S, tm, tf, norm_output):
    R, D = x.shape
    F = w_gate.shape[2]
    K = conv_w.shape[1]
    nf = F // tf
    vmem = (4 * tm * D * 4 + tm * D * 2 + 2 * 3 * D * tf * 2 + (tm + V7X_SUBLANES) * tf * 4
            + nf * V7X_SUBLANES * tf * 4 + 4 * tm * tf * 4)
    return pl.pallas_call(
        functools.partial(_ffn_body, tiles_per_seq=S // tm, norm_output=norm_output),
        grid=(R // tm, nf),
        in_specs=[
            pl.BlockSpec((tm, D), lambda i, j: (i, 0)),
            pl.BlockSpec((None, 1, D), lambda i, j: (l, 0, 0)),
            pl.BlockSpec((None, D, tf), lambda i, j: (l, 0, j)),
            pl.BlockSpec((None, D, tf), lambda i, j: (l, 0, j)),
            pl.BlockSpec((None, K, tf), lambda i, j: (l, 0, j)),
            pl.BlockSpec((None, 1, tf), lambda i, j: (l, 0, j)),
            pl.BlockSpec((None, tf, D), lambda i, j: (l, j, 0)),
            pl.BlockSpec((1, D), lambda i, j: (0, 0)),
        ],
        out_specs=pl.BlockSpec((tm, D), lambda i, j: (i, 0)),
        out_shape=jax.ShapeDtypeStruct((R, D), F32),
        scratch_shapes=[
            pltpu.VMEM((tm, D), BF16),
            pltpu.VMEM((tm + V7X_SUBLANES, tf), F32),
            pltpu.VMEM((nf, V7X_SUBLANES, tf), F32),
        ],
        compiler_params=_compiler_params(2, vmem),
        name="conv_glu_ffn",
    )(x, g, w_gate, w_up, conv_w, conv_b, w_down, out_g)


def _tiles(S):
    return dict(tm=min(1024, S), ts_mix=min(512, S))


def kernel(x, mem, mix_norm_g, w_in, w_pool, pool_scale, dn_conv_w, dn_a_log, dn_dt_bias, dn_norm_g, w_mix_out,
           xa_norm_g, mem_norm_g, w_xq, w_xkv, w_xo, ffn_norm_g, w_gate, w_up, ffn_conv_w, ffn_conv_b, w_down,
           final_norm_g):
    B, S, D = x.shape
    M = mem.shape[1]
    depth = w_in.shape[0]
    H = dn_a_log.shape[1]
    dn_w = H * DN_HEAD_DIM
    pool_w = w_pool.shape[1] * w_pool.shape[2]
    main = pool_w + 4 * dn_w
    assert pool_w == dn_w and w_in.shape[2] == main + 2 * H and 2 * H <= V7X_LANES
    t = _tiles(S)

    tn = 512
    w_in_b = w_in[:, :, :main].astype(BF16)
    w_gates_b = jnp.pad(w_in[:, :, main:].astype(BF16), ((0, 0), (0, 0), (0, V7X_LANES - 2 * H)))
    pad_gate = lambda a: jnp.pad(a, ((0, 0), (H, V7X_LANES - 2 * H)))[:, None, :]
    a_log_p, dt_bias_p = pad_gate(dn_a_log), pad_gate(dn_dt_bias)
    w_pool_b = w_pool.astype(BF16)
    w_mix_b = w_mix_out.astype(BF16)
    w_xq_b, w_xkv_b, w_xo_b = w_xq.astype(BF16), w_xkv.astype(BF16), w_xo.astype(BF16)
    w_gate_b, w_up_b, w_down_b = w_gate.astype(BF16), w_up.astype(BF16), w_down.astype(BF16)
    row = lambda a: a[:, None, :]

    xr = x.reshape(B * S, D)
    kv = norm_matmul(mem.reshape(B * M, D), mem_norm_g[None, :], w_xkv_b, BF16, tm=min(1024, B * M), tn=2 * tn)
    for l in range(depth):
        p, gates = in_proj(xr, mix_norm_g[l][None, :], w_in_b, w_gates_b, dn_conv_w, l, S, tm=t["tm"], tn=tn,
                           conv_col0=pool_w)
        y_pool, y_dn = pool_and_delta_net(p, gates, w_pool_b, row(pool_scale), a_log_p, dt_bias_p, row(dn_norm_g),
                                          l, B, S, t["ts_mix"], col_block0=pool_w // dn_w, H=H)
        xr = mix_out(xr, y_pool, y_dn, w_mix_b, l, tm=t["tm"] // 2, tn=D)
        xr = cross_attention(xr, row(xa_norm_g), w_xq_b, kv, w_xo_b, l, B, S, M, tm=t["tm"], tn=2 * tn)
        xr = conv_glu_ffn(xr, row(ffn_norm_g), w_gate_b, w_up_b, ffn_conv_w, row(ffn_conv_b), w_down_b,
                          final_norm_g[None, :], l, S, tm=t["tm"], tf=tn, norm_output=(l == depth - 1))
    return xr.reshape(B, S, D)
```

```python
import functools

import jax
import jax.numpy as jnp
from jax import lax
from jax.experimental import pallas as pl
from jax.experimental.pallas import tpu as pltpu

F32 = jnp.float32
BF16 = jnp.bfloat16

EPS = 1e-6
POOL_WINDOWS = (2, 4, 8, 16)
POOL_HALO = 16
DN_HEAD_DIM = 128
DN_CHUNK = 128
DN_HEADS_PER_STEP = 4
XA_HEADS = 4
XA_HEADS_PER_STEP = 2

V7X_VMEM_BYTES = 64 * 1024 * 1024
V7X_LANES = 128
V7X_SUBLANES = 8
NORM_ROWS = 128
XA_SUB_ROWS = 1024
FFN_SUB_ROWS = 1024
FFN_FIRST_SUB_ROWS = 256
NORM_SUB_ROWS = 256
IN_PROJ_SUB_ROWS = 128


def _compiler_params(n_axes, vmem_bytes):
    limit = min(int(vmem_bytes * 1.25) + (4 << 20), V7X_VMEM_BYTES * 7 // 8)
    return pltpu.CompilerParams(dimension_semantics=("arbitrary",) * n_axes, vmem_limit_bytes=limit)


def _dot(a, b):
    return jnp.dot(a, b, preferred_element_type=F32)


def _dot_nt(a, b):
    return lax.dot_general(a, b, (((1,), (1,)), ((), ())), preferred_element_type=F32)


def _silu(x):
    return x * (1.0 / (1.0 + jnp.exp(-x)))


def _rms_rows(x, g):
    ms = jnp.mean(x * x, axis=-1, keepdims=True)
    return x * lax.rsqrt(ms + EPS) * g


def _rows_shifted_down(a, prev, s):
    n = prev.shape[0]
    rows = lax.broadcasted_iota(jnp.int32, prev.shape, 0)
    rolled = [pltpu.roll(piece, s, axis=0) for piece in [prev] + [a[r:r + n, :] for r in range(0, a.shape[0], n)]]
    return jnp.concatenate([jnp.where(rows < s, lo, hi) for lo, hi in zip(rolled[:-1], rolled[1:])], axis=0)


def _norm_rows_into(h_ref, x_ref, g_ref):
    g = g_ref[...]

    def body(r, c):
        rows = pl.ds(pl.multiple_of(r * NORM_ROWS, NORM_ROWS), NORM_ROWS)
        h_ref[rows, :] = _rms_rows(x_ref[rows, :], g).astype(h_ref.dtype)
        return c

    lax.fori_loop(0, x_ref.shape[0] // NORM_ROWS, body, 0)


def _norm_matmul_body(x_ref, g_ref, w_ref, o_ref, h_ref):
    @pl.when((pl.program_id(1) == 0) & (pl.program_id(2) == 0))
    def _():
        _norm_rows_into(h_ref, x_ref, g_ref)

    o_ref[...] = _dot(h_ref[...], w_ref[...]).astype(o_ref.dtype)


def norm_matmul(x, g, w, out_dtype, tm, tn):
    R, K = x.shape
    L, _, N = w.shape
    osz = jnp.dtype(out_dtype).itemsize
    vmem = 2 * tm * K * 4 + tm * K * 2 + 2 * K * tn * 2 + 2 * tm * tn * osz + tm * tn * 4
    return pl.pallas_call(
        _norm_matmul_body,
        grid=(R // tm, L, N // tn),
        in_specs=[
            pl.BlockSpec((tm, K), lambda i, l, j: (i, 0)),
            pl.BlockSpec((1, K), lambda i, l, j: (0, 0)),
            pl.BlockSpec((None, K, tn), lambda i, l, j: (l, 0, j)),
        ],
        out_specs=pl.BlockSpec((None, tm, tn), lambda i, l, j: (l, i, j)),
        out_shape=jax.ShapeDtypeStruct((L, R, N), out_dtype),
        scratch_shapes=[pltpu.VMEM((tm, K), BF16)],
        compiler_params=_compiler_params(3, vmem),
        name="norm_matmul",
    )(x, g, w)


def _in_proj_body(x_ref, g_ref, w_ref, wg_ref, cw_ref, p_ref, gate_ref, h_ref, carry, *,
                  n_main, tiles_per_seq, conv0, tiles_per_part):
    i = pl.program_id(0)
    j = pl.program_id(1)
    tm, tn = p_ref.shape
    K = cw_ref.shape[0]
    halo = V7X_SUBLANES

    is_conv = (j >= conv0) & (j < conv0 + 3 * tiles_per_part)

    @pl.when(j == 0)
    def _():
        sub = min(NORM_SUB_ROWS, tm)
        for r0 in range(0, tm, sub):
            h = _rms_rows(x_ref[r0:r0 + sub, :], g_ref[...]).astype(BF16)
            h_ref[r0:r0 + sub, :] = h
            p_ref[r0:r0 + sub, :] = _dot(h, w_ref[...])

    @pl.when(j == n_main)
    def _():
        gate_ref[...] = _dot(h_ref[...], wg_ref[...])

    @pl.when(jnp.logical_not(is_conv) & (j > 0) & (j < n_main))
    def _():
        p_ref[...] = _dot(h_ref[...], w_ref[...])

    @pl.when(is_conv)
    def _():
        cj = j - conv0
        part = cj // tiles_per_part
        q_scale = jnp.where(part == 0, DN_HEAD_DIM ** -0.5, 1.0)
        prev = jnp.where((i % tiles_per_seq) == 0, 0.0, carry[cj])
        sub = min(IN_PROJ_SUB_ROWS, tm)
        for r0 in range(0, tm, sub):
            acc = _dot(h_ref[r0:r0 + sub, :], w_ref[...])
            y = acc * cw_ref[K - 1:K, :]
            for s in range(1, K):
                y = y + _rows_shifted_down(acc, prev, s) * cw_ref[K - 1 - s:K - s, :]
            prev = acc[sub - halo:sub, :]
            y = _silu(y)
            for c in range(0, tn, DN_HEAD_DIM):
                yh = y[:, c:c + DN_HEAD_DIM]
                inv = lax.rsqrt(jnp.sum(yh * yh, axis=-1, keepdims=True) + EPS) * q_scale
                p_ref[r0:r0 + sub, c:c + DN_HEAD_DIM] = yh * jnp.where(part == 2, 1.0, inv)
        carry[cj] = prev


def in_proj(x, g, w, w_gates, conv_w, l, S, tm, tn, conv_col0):
    R, D = x.shape
    N = w.shape[2]
    NG = w_gates.shape[2]
    K, conv_cols = conv_w.shape[1:]
    assert N % tn == 0 and conv_col0 % tn == 0 and (conv_cols // 3) % tn == 0 and tn % DN_HEAD_DIM == 0
    n_main, conv0, n_conv = N // tn, conv_col0 // tn, conv_cols // tn
    assert conv0 >= 1
    main = lambda j: jnp.minimum(j, n_main - 1)
    vmem = (2 * tm * D * 4 + tm * D * 2 + 2 * D * tn * 2 + 2 * D * NG * 2 + 3 * tm * tn * 4 + 2 * tm * NG * 4
            + n_conv * V7X_SUBLANES * tn * 4 + 8 * IN_PROJ_SUB_ROWS * tn * 4)
    return pl.pallas_call(
        functools.partial(_in_proj_body, n_main=n_main, tiles_per_seq=S // tm, conv0=conv0,
                          tiles_per_part=n_conv // 3),
        grid=(R // tm, n_main + 1),
        in_specs=[
            pl.BlockSpec((tm, D), lambda i, j: (i, 0)),
            pl.BlockSpec((1, D), lambda i, j: (0, 0)),
            pl.BlockSpec((None, D, tn), lambda i, j: (l, 0, main(j))),
            pl.BlockSpec((None, D, NG), lambda i, j: (l, 0, 0)),
            pl.BlockSpec((None, K, tn), lambda i, j: (l, 0, jnp.clip(j - conv0, 0, n_conv - 1))),
        ],
        out_specs=[
            pl.BlockSpec((tm, tn), lambda i, j: (i, main(j))),
            pl.BlockSpec((tm, NG), lambda i, j: (i, 0)),
        ],
        out_shape=[jax.ShapeDtypeStruct((R, N), F32), jax.ShapeDtypeStruct((R, NG), F32)],
        scratch_shapes=[
            pltpu.VMEM((tm, D), BF16),
            pltpu.VMEM((n_conv, V7X_SUBLANES, tn), F32),
        ],
        compiler_params=_compiler_params(2, vmem),
        name="in_proj",
    )(x, g, w, w_gates, conv_w)


def _pool_mix(u_ref, buf, mixed_ref):
    s = pl.program_id(1)
    ts = u_ref.shape[0]
    G = u_ref.shape[1] // len(POOL_WINDOWS)

    @pl.when(s == 0)
    def _():
        buf[0:POOL_HALO, :] = jnp.zeros((POOL_HALO, buf.shape[1]), F32)

    buf[POOL_HALO:POOL_HALO + ts, :] = u_ref[...]
    pos = (s * ts + 1 + lax.broadcasted_iota(jnp.int32, (ts, 1), 0)).astype(F32)
    for i, w in enumerate(POOL_WINDOWS):
        cols = slice(i * G, (i + 1) * G)
        u = buf[POOL_HALO:POOL_HALO + ts, cols]
        acc = u
        for k in range(1, w):
            acc = acc + buf[POOL_HALO - k:POOL_HALO - k + ts, cols]
        mixed_ref[:, cols] = (acc / jnp.minimum(pos, float(w)) - u).astype(mixed_ref.dtype)
    buf[0:POOL_HALO, :] = buf[ts:ts + POOL_HALO, :]


def _pool_project(mixed_ref, wp_ref, sc_ref, o_ref):
    G = wp_ref.shape[1]
    for i in range(len(POOL_WINDOWS)):
        cols = slice(i * G, (i + 1) * G)
        o_ref[:, cols] = (_dot(mixed_ref[:, cols], wp_ref[i]) * sc_ref[:, cols]).astype(o_ref.dtype)


def _unit_lower_inverses(ls, eye, ii, jj):
    C = ls[0].shape[0]
    size = 2
    same = (ii // size) == (jj // size)
    xs = [eye - jnp.where(same, l_mat, 0.0) for l_mat in ls]
    while size < C:
        inner, same = same, (ii // (2 * size)) == (jj // (2 * size))
        offs = [jnp.where(same & jnp.logical_not(inner), l_mat, 0.0).astype(BF16) for l_mat in ls]
        xbs = [x.astype(BF16) for x in xs]
        ys = [_dot(off, xb) for off, xb in zip(offs, xbs)]
        xs = [x - _dot(xb, y.astype(BF16)) for x, xb, y in zip(xs, xbs, ys)]
        size *= 2
    return xs


def _dn_body(q_ref, k_ref, v_ref, z_ref, ba_ref, alog_ref, dtb_ref, ng_ref, pu_ref, wp_ref, psc_ref, o_ref, yp_ref,
             state, beta_rep, gc_rep, grow_s, cdec_s, u_s, w_s, qd_s, at_s, kdt_s, pool_buf, mixed_s):
    s = pl.program_id(1)
    ts, W = q_ref.shape
    H = W // DN_HEAD_DIM
    C = DN_CHUNK
    n_chunk = ts // C

    @pl.when(s == 0)
    def _():
        state[...] = jnp.zeros(state.shape, F32)

    _pool_mix(pu_ref, pool_buf, mixed_s)

    ba = ba_ref[...]
    beta = 1.0 / (1.0 + jnp.exp(-ba))
    xs = ba + dtb_ref[...]
    softplus = jnp.maximum(xs, 0.0) + jnp.log1p(jnp.exp(-jnp.abs(xs)))
    g = -jnp.exp(alog_ref[...]) * softplus
    row_in_chunk = lax.broadcasted_iota(jnp.int32, (ts, V7X_LANES), 0) % C
    gc = g
    sh = 1
    while sh < C:
        gc = gc + jnp.where(row_in_chunk >= sh, pltpu.roll(gc, sh, axis=0), 0.0)
        sh *= 2
    gct = gc.T
    lane = lax.broadcasted_iota(jnp.int32, (ts, V7X_LANES), 1)
    for h in range(H):
        beta_rep[h] = jnp.broadcast_to(
            jnp.sum(jnp.where(lane == h, beta, 0.0), axis=-1, keepdims=True), (ts, V7X_LANES))
        gc_rep[h] = jnp.broadcast_to(
            jnp.sum(jnp.where(lane == H + h, gc, 0.0), axis=-1, keepdims=True), (ts, V7X_LANES))
        grow_s[h] = gct[H + h:H + h + 1, :]

    ii = lax.broadcasted_iota(jnp.int32, (C, C), 0)
    jj = lax.broadcasted_iota(jnp.int32, (C, C), 1)
    eye = (ii == jj).astype(F32)

    def head_group_body(hg, carry):
        pairs = []
        for hh in range(DN_HEADS_PER_STEP):
            h = hg * DN_HEADS_PER_STEP + hh
            cols = pl.ds(pl.multiple_of(h * DN_HEAD_DIM, DN_HEAD_DIM), DN_HEAD_DIM)
            bcol, gcol, grow_all = beta_rep[h], gc_rep[h], grow_s[h]
            egc = jnp.exp(gcol)
            for c in range(n_chunk):
                r = slice(c * C, (c + 1) * C)
                pairs.append(dict(h=h, c=c, r=r, cols=cols, q=q_ref[r, cols], k=k_ref[r, cols], v=v_ref[r, cols],
                                  b=bcol[r], g=gcol[r], e=egc[r], grow=grow_all[:, r]))
        kbs = [p["k"].astype(BF16) for p in pairs]
        decs = [jnp.exp(jnp.minimum(p["g"] - p["grow"], 0.0)) for p in pairs]
        kks = [_dot_nt(kb, kb) for kb in kbs]
        qks = [_dot_nt(p["q"].astype(BF16), kb) for p, kb in zip(pairs, kbs)]
        ls = [jnp.where(ii > jj, kk * dec, 0.0) * p["b"] for kk, dec, p in zip(kks, decs, pairs)]
        for p, qk, dec in zip(pairs, qks, decs):
            r, cols = p["r"], p["cols"]
            at_s[r, cols] = jnp.where(ii >= jj, qk * dec, 0.0).astype(BF16)
            qd_s[r, cols] = (p["q"] * p["e"]).astype(BF16)
            kd = p["k"] * jnp.exp(p["g"][C - 1:C, :] - p["g"])
            kdt_s[r, cols] = kd.T.astype(BF16)
            cdec_s[p["c"], p["h"]] = p["e"][C - 1:C, :]
        t_invs = _unit_lower_inverses(ls, eye, ii, jj)
        rhs = [jnp.concatenate([p["v"] * p["b"], p["k"] * (p["b"] * p["e"])], axis=1) for p in pairs]
        sols = [_dot(t_inv.astype(BF16), b.astype(BF16)) for t_inv, b in zip(t_invs, rhs)]
        for p, sol in zip(pairs, sols):
            u_s[p["r"], p["cols"]] = sol[:, :DN_HEAD_DIM]
            w_s[p["r"], p["cols"]] = sol[:, DN_HEAD_DIM:].astype(BF16)
        return carry

    lax.fori_loop(0, H // DN_HEADS_PER_STEP, head_group_body, 0, unroll=True)

    def chunk_body(c, carry):
        r0 = pl.multiple_of(c * C, C)
        rows = pl.ds(r0, C)
        hs = range(H)
        cols = [slice(h * DN_HEAD_DIM, (h + 1) * DN_HEAD_DIM) for h in hs]
        sts = [state[h] for h in hs]
        sbs = [st.astype(BF16) for st in sts]
        wss = [_dot(w_s[rows, cols[h]], sbs[h]) for h in hs]
        qss = [_dot(qd_s[rows, cols[h]], sbs[h]) for h in hs]
        vbs = [(u_s[rows, cols[h]] - wss[h]).astype(BF16) for h in hs]
        avs = [_dot(at_s[rows, cols[h]], vbs[h]) for h in hs]
        kvs = [_dot(kdt_s[rows, cols[h]], vbs[h]) for h in hs]
        for h in hs:
            state[h] = sts[h] * cdec_s[c, h] + kvs[h]
            gated = _rms_rows(qss[h] + avs[h], ng_ref[...]) * _silu(z_ref[rows, cols[h]])
            o_ref[rows, cols[h]] = gated.astype(o_ref.dtype)
        return carry

    lax.fori_loop(0, n_chunk, chunk_body, 0, unroll=True)
    _pool_project(mixed_s, wp_ref, psc_ref, yp_ref)


def pool_and_delta_net(p, gates, w_pool, pool_scale, a_log, dt_bias, norm_g, l, B, S, ts, col_block0, H):
    n_win, G, _ = w_pool.shape[1:]
    PW = n_win * G
    W = H * DN_HEAD_DIM
    nt = S // ts
    NB = gates.shape[1]
    C = DN_CHUNK
    vmem = (2 * 4 * ts * W * 4 + 2 * ts * NB * 4 + 2 * ts * W * 2
            + H * DN_HEAD_DIM * DN_HEAD_DIM * 4 + 2 * H * ts * V7X_LANES * 4 + H * V7X_SUBLANES * ts * 4
            + ts * W * 4 + 4 * ts * W * 2 + 48 * C * C * 4 * DN_HEADS_PER_STEP
            + 2 * ts * PW * 4 + (ts + POOL_HALO) * PW * 4 + 2 * n_win * G * G * 2 + 2 * ts * PW * 2)

    def pblock(k):
        return pl.BlockSpec((ts, W), lambda b, s: (b * nt + s, col_block0 + k))

    y_dn, y_pool = pl.pallas_call(
        _dn_body,
        grid=(B, nt),
        in_specs=[
            pblock(0), pblock(1), pblock(2), pblock(3),
            pl.BlockSpec((ts, NB), lambda b, s: (b * nt + s, 0)),
            pl.BlockSpec((None, 1, NB), lambda b, s: (l, 0, 0)),
            pl.BlockSpec((None, 1, NB), lambda b, s: (l, 0, 0)),
            pl.BlockSpec((None, 1, DN_HEAD_DIM), lambda b, s: (l, 0, 0)),
            pl.BlockSpec((ts, PW), lambda b, s: (b * nt + s, 0)),
            pl.BlockSpec((None, n_win, G, G), lambda b, s: (l, 0, 0, 0)),
            pl.BlockSpec((None, 1, PW), lambda b, s: (l, 0, 0)),
        ],
        out_specs=[
            pl.BlockSpec((ts, W), lambda b, s: (b * nt + s, 0)),
            pl.BlockSpec((ts, PW), lambda b, s: (b * nt + s, 0)),
        ],
        out_shape=[jax.ShapeDtypeStruct((B * S, W), BF16), jax.ShapeDtypeStruct((B * S, PW), BF16)],
        scratch_shapes=[
            pltpu.VMEM((H, DN_HEAD_DIM, DN_HEAD_DIM), F32),
            pltpu.VMEM((H, ts, V7X_LANES), F32),
            pltpu.VMEM((H, ts, V7X_LANES), F32),
            pltpu.VMEM((H, 1, ts), F32),
            pltpu.VMEM((ts // C, H, 1, V7X_LANES), F32),
            pltpu.VMEM((ts, W), F32),
            pltpu.VMEM((ts, W), BF16),
            pltpu.VMEM((ts, W), BF16),
            pltpu.VMEM((ts, W), BF16),
            pltpu.VMEM((ts, W), BF16),
            pltpu.VMEM((ts + POOL_HALO, PW), F32),
            pltpu.VMEM((ts, PW), BF16),
        ],
        compiler_params=_compiler_params(2, vmem),
        name="pool_and_delta_net",
    )(p, p, p, p, gates, a_log, dt_bias, norm_g, p, w_pool, pool_scale)
    return y_pool, y_dn


def _mix_out_body(x_ref, yp_ref, yd_ref, w1_ref, w2_ref, o_ref):
    o_ref[...] = x_ref[...] + _dot(yp_ref[...], w1_ref[...]) + _dot(yd_ref[...], w2_ref[...])


def mix_out(x, y_pool, y_dn, w, l, tm, tn):
    R, D = x.shape
    K1, K2 = y_pool.shape[1], y_dn.shape[1]
    assert K1 == K2
    vmem = 2 * (2 * tm * tn * 4 + tm * (K1 + K2) * 2 + (K1 + K2) * tn * 2) + 2 * tm * tn * 4
    return pl.pallas_call(
        _mix_out_body,
        grid=(R // tm, D // tn),
        in_specs=[
            pl.BlockSpec((tm, tn), lambda i, j: (i, j)),
            pl.BlockSpec((tm, K1), lambda i, j: (i, 0)),
            pl.BlockSpec((tm, K2), lambda i, j: (i, 0)),
            pl.BlockSpec((None, K1, tn), lambda i, j: (l, 0, j)),
            pl.BlockSpec((None, K2, tn), lambda i, j: (l, 1, j)),
        ],
        out_specs=pl.BlockSpec((tm, tn), lambda i, j: (i, j)),
        out_shape=jax.ShapeDtypeStruct((R, D), F32),
        compiler_params=_compiler_params(2, vmem),
        name="mix_out",
    )(x, y_pool, y_dn, w, w)


def _xattn_body(x_ref, g_ref, wq_ref, k_ref, v_ref, wo_ref, o_ref, h_ref, a_ref):
    j = pl.program_id(1)
    dh = wq_ref.shape[1] // XA_HEADS_PER_STEP
    tn = wo_ref.shape[1]
    n_att = XA_HEADS // XA_HEADS_PER_STEP

    def attention_step(first):
        tm = h_ref.shape[0]
        sub = min(XA_SUB_ROWS, tm)
        col0 = pl.multiple_of(j * (XA_HEADS_PER_STEP * dh), XA_HEADS_PER_STEP * dh)
        units = [(hh, r0) for hh in range(XA_HEADS_PER_STEP) for r0 in range(0, tm, sub)]

        def scores(unit):
            hh, r0 = unit
            if first and hh == 0:
                h_ref[r0:r0 + sub, :] = _rms_rows(x_ref[r0:r0 + sub, :], g_ref[...]).astype(BF16)
            q = _dot(h_ref[r0:r0 + sub, :], wq_ref[:, hh * dh:(hh + 1) * dh])
            return _dot_nt(q.astype(BF16), k_ref[:, hh * dh:(hh + 1) * dh]) * (dh ** -0.5)

        sc = scores(units[0])
        for n, (hh, r0) in enumerate(units):
            sc_next = scores(units[n + 1]) if n + 1 < len(units) else None
            e = jnp.exp(sc - jnp.max(sc, axis=-1, keepdims=True))
            pr = e / jnp.sum(e, axis=-1, keepdims=True)
            o = _dot(pr.astype(BF16), v_ref[:, hh * dh:(hh + 1) * dh])
            a_ref[r0:r0 + sub, pl.ds(col0 + hh * dh, dh)] = o.astype(BF16)
            sc = sc_next

    pl.when(j == 0)(functools.partial(attention_step, True))
    pl.when((j > 0) & (j < n_att))(functools.partial(attention_step, False))

    @pl.when(j >= n_att)
    def _():
        cols = pl.ds(pl.multiple_of((j - n_att) * tn, tn), tn)
        o_ref[...] = x_ref[:, cols] + _dot(a_ref[...], wo_ref[...])


def cross_attention(x, g, w_xq, kv, w_xo, l, B, S, M, tm, tn):
    R, D = x.shape
    n_att = XA_HEADS // XA_HEADS_PER_STEP
    wh = D // n_att
    tiles_per_batch = S // tm
    att = lambda j: jnp.minimum(j, n_att - 1)
    otile = lambda j: jnp.maximum(j - n_att, 0)
    vmem = (2 * tm * D * 4 + 2 * tm * D * 2 + 4 * D * wh * 2 + 4 * M * wh * 2 + 4 * D * tn * 2 + 3 * tm * tn * 4
            + 3 * tm * wh * 4 + 3 * tm * M * 4)
    return pl.pallas_call(
        _xattn_body,
        grid=(R // tm, n_att + D // tn),
        in_specs=[
            pl.BlockSpec((tm, D), lambda i, j: (i, 0)),
            pl.BlockSpec((None, 1, D), lambda i, j: (l, 0, 0)),
            pl.BlockSpec((None, D, wh), lambda i, j: (l, 0, att(j))),
            pl.BlockSpec((None, M, wh), lambda i, j: (l, i // tiles_per_batch, att(j))),
            pl.BlockSpec((None, M, wh), lambda i, j: (l, i // tiles_per_batch, n_att + att(j))),
            pl.BlockSpec((None, D, tn), lambda i, j: (l, 0, otile(j))),
        ],
        out_specs=pl.BlockSpec((tm, tn), lambda i, j: (i, otile(j))),
        out_shape=jax.ShapeDtypeStruct((R, D), F32),
        scratch_shapes=[pltpu.VMEM((tm, D), BF16), pltpu.VMEM((tm, D), BF16)],
        compiler_params=_compiler_params(2, vmem),
        name="cross_attention",
    )(x, g, w_xq, kv, kv, w_xo)


def _ffn_body(x_ref, g_ref, wg_ref, wu_ref, cw_ref, cb_ref, wd_ref, og_ref, o_ref, h_ref, gbuf, carry, *,
              tiles_per_seq, norm_output):
    i = pl.program_id(0)
    j = pl.program_id(1)
    tm = x_ref.shape[0]
    K = cw_ref.shape[0]
    halo = V7X_SUBLANES

    def column_step(first):
        gbuf[0:halo, :] = jnp.where((i % tiles_per_seq) == 0, 0.0, carry[j])
        sub = min(FFN_FIRST_SUB_ROWS if first else FFN_SUB_ROWS, tm)

        def gate_up(r0):
            rows = slice(r0, r0 + sub)
            if first:
                x = x_ref[rows, :]
                h_ref[rows, :] = _rms_rows(x, g_ref[...]).astype(BF16)
                o_ref[rows, :] = x
            h = h_ref[rows, :]
            return _dot(h, wg_ref[...]), _dot(h, wu_ref[...])

        gate, up = gate_up(0)
        for r0 in range(0, tm, sub):
            nxt = gate_up(r0 + sub) if r0 + sub < tm else None
            gbuf[halo + r0:halo + r0 + sub, :] = gate
            conv = gate * cw_ref[K - 1:K, :] + cb_ref[...]
            for t in range(K - 2, -1, -1):
                conv = conv + gbuf[halo - K + 1 + t + r0:halo - K + 1 + t + r0 + sub, :] * cw_ref[t:t + 1, :]
            act = _silu(conv) * up
            o_ref[r0:r0 + sub, :] += _dot(act.astype(BF16), wd_ref[...])
            if nxt is not None:
                gate, up = nxt
        carry[j] = gbuf[tm:tm + halo, :]

    pl.when(j == 0)(functools.partial(column_step, True))
    pl.when(j > 0)(functools.partial(column_step, False))

    if norm_output:
        @pl.when(j == pl.num_programs(1) - 1)
        def _():
            _norm_rows_into(o_ref, o_ref, og_ref)


def conv_glu_ffn(x, g, w_gate, w_up, conv_w, conv_b, w_down, out_g, l, S, tm, tf, norm_output):
    R, D = x.shape
    F = w_gate.shape[2]
    K = conv_w.shape[1]
    nf = F // tf
    vmem = (4 * tm * D * 4 + tm * D * 2 + 2 * 3 * D * tf * 2 + (tm + V7X_SUBLANES) * tf * 4
            + nf * V7X_SUBLANES * tf * 4 + 4 * tm * tf * 4)
    return pl.pallas_call(
        functools.partial(_ffn_body, tiles_per_seq=S // tm, norm_output=norm_output),
        grid=(R // tm, nf),
        in_specs=[
            pl.BlockSpec((tm, D), lambda i, j: (i, 0)),
            pl.BlockSpec((None, 1, D), lambda i, j: (l, 0, 0)),
            pl.BlockSpec((None, D, tf), lambda i, j: (l, 0, j)),
            pl.BlockSpec((None, D, tf), lambda i, j: (l, 0, j)),
            pl.BlockSpec((None, K, tf), lambda i, j: (l, 0, j)),
            pl.BlockSpec((None, 1, tf), lambda i, j: (l, 0, j)),
            pl.BlockSpec((None, tf, D), lambda i, j: (l, j, 0)),
            pl.BlockSpec((1, D), lambda i, j: (0, 0)),
        ],
        out_specs=pl.BlockSpec((tm, D), lambda i, j: (i, 0)),
        out_shape=jax.ShapeDtypeStruct((R, D), F32),
        scratch_shapes=[
            pltpu.VMEM((tm, D), BF16),
            pltpu.VMEM((tm + V7X_SUBLANES, tf), F32),
            pltpu.VMEM((nf, V7X_SUBLANES, tf), F32),
        ],
        compiler_params=_compiler_params(2, vmem),
        name="conv_glu_ffn",
    )(x, g, w_gate, w_up, conv_w, conv_b, w_down, out_g)


def _tiles(S):
    return dict(tm=min(1024, S), ts_mix=min(512, S))


def kernel(x, mem, mix_norm_g, w_in, w_pool, pool_scale, dn_conv_w, dn_a_log, dn_dt_bias, dn_norm_g, w_mix_out,
           xa_norm_g, mem_norm_g, w_xq, w_xkv, w_xo, ffn_norm_g, w_gate, w_up, ffn_conv_w, ffn_conv_b, w_down,
           final_norm_g):
    B, S, D = x.shape
    M = mem.shape[1]
    depth = w_in.shape[0]
    H = dn_a_log.shape[1]
    dn_w = H * DN_HEAD_DIM
    pool_w = w_pool.shape[1] * w_pool.shape[2]
    main = pool_w + 4 * dn_w
    assert pool_w == dn_w and w_in.shape[2] == main + 2 * H and 2 * H <= V7X_LANES
    t = _tiles(S)

    tn = 512
    w_in_b = w_in[:, :, :main].astype(BF16)
    w_gates_b = jnp.pad(w_in[:, :, main:].astype(BF16), ((0, 0), (0, 0), (0, V7X_LANES - 2 * H)))
    pad_gate = lambda a: jnp.pad(a, ((0, 0), (H, V7X_LANES - 2 * H)))[:, None, :]
    a_log_p, dt_bias_p = pad_gate(dn_a_log), pad_gate(dn_dt_bias)
    w_pool_b = w_pool.astype(BF16)
    w_mix_b = w_mix_out.astype(BF16)
    w_xq_b, w_xkv_b, w_xo_b = w_xq.astype(BF16), w_xkv.astype(BF16), w_xo.astype(BF16)
    w_gate_b, w_up_b, w_down_b = w_gate.astype(BF16), w_up.astype(BF16), w_down.astype(BF16)
    row = lambda a: a[:, None, :]

    xr = x.reshape(B * S, D)
    kv = norm_matmul(mem.reshape(B * M, D), mem_norm_g[None, :], w_xkv_b, BF16, tm=min(1024, B * M), tn=2 * tn)
    for l in range(depth):
        p, gates = in_proj(xr, mix_norm_g[l][None, :], w_in_b, w_gates_b, dn_conv_w, l, S, tm=t["tm"], tn=tn,
                           conv_col0=pool_w)
        y_pool, y_dn = pool_and_delta_net(p, gates, w_pool_b, row(pool_scale), a_log_p, dt_bias_p, row(dn_norm_g),
                                          l, B, S, t["ts_mix"], col_block0=pool_w // dn_w, H=H)
        xr = mix_out(xr, y_pool, y_dn, w_mix_b, l, tm=t["tm"] // 2, tn=D)
        xr = cross_attention(xr, row(xa_norm_g), w_xq_b, kv, w_xo_b, l, B, S, M, tm=t["tm"], tn=2 * tn)
        xr = conv_glu_ffn(xr, row(ffn_norm_g), w_gate_b, w_up_b, ffn_conv_w, row(ffn_conv_b), w_down_b,
                          final_norm_g[None, :], l, S, tm=t["tm"], tf=tn, norm_output=(l == depth - 1))
    return xr.reshape(B, S, D)
```
